```python
import numpy as np
import jax
import jax.numpy as jnp
from jax import lax

D_MODEL = 1024
BATCH = 16
SEQ = 2048
DEPTH = 2

HEAD_DIM = 64
ROT_DIM = HEAD_DIM // 4
ROPE_THETA = 500000.0
NORM_EPS = 1e-6
Q_BLOCK = 128
D_FF = 2816
NEG_INF = -1e30

A_GROUPS = 4
A_CHUNK = 128
A_WIDTH = A_GROUPS * HEAD_DIM
B_HEADS = 12
B_KV_HEADS = 3
B_WIDTH = B_HEADS * HEAD_DIM
B_KV_WIDTH = B_KV_HEADS * HEAD_DIM
CMP_LEN = 32
CMP_STRIDE = 16
CMP_HIDDEN = 256
SLC_BLOCK = 64
SLC_TOPN = 8
WINDOW = 512
N_BRANCH = 3
FORCE_SCORE = 1e4
C_HEADS = 8
C_WIDTH = C_HEADS * HEAD_DIM
IDX_HEADS = 4
IDX_DIM = 32
IDX_ROT = IDX_DIM // 4
DSA_TOPK = 256
D_HEADS = 8
D_WIDTH = D_HEADS * HEAD_DIM
MOBA_BLOCK = 256
MOBA_TOPK = 3

EVEN_SPLITS = (2 * A_WIDTH, B_WIDTH) + (B_KV_WIDTH,) * 6 + (B_HEADS * N_BRANCH,)
ODD_SPLITS = (C_WIDTH, HEAD_DIM, HEAD_DIM, IDX_HEADS * IDX_DIM, IDX_DIM, IDX_HEADS, D_WIDTH, D_WIDTH, D_WIDTH)
EVEN_IN = sum(EVEN_SPLITS)
ODD_IN = sum(ODD_SPLITS)
MIX_OUT = A_WIDTH + B_WIDTH
N_EVEN = (DEPTH + 1) // 2
N_ODD = DEPTH // 2

kernel_name = 'hybrid_gmlp_nsa_dsa_moba_macaron'


def rmsnorm(x, g):
    xf = x.astype(jnp.float32)
    y = xf * lax.rsqrt(jnp.mean(xf * xf, axis=-1, keepdims=True) + NORM_EPS)
    return (y * g.astype(jnp.float32)).astype(x.dtype)


def rope_tables(positions, rot_dim):
    inv_freq = ROPE_THETA ** (-jnp.arange(0, rot_dim, 2, dtype=jnp.float32) / rot_dim)
    ang = positions.astype(jnp.float32)[..., None] * inv_freq
    return jnp.cos(ang), jnp.sin(ang)


def partial_rope(x, cos, sin):
    half = cos.shape[-1]
    bshape = cos.shape[:2] + (1,) * (x.ndim - 3) + (half,)
    c = cos.reshape(bshape).astype(x.dtype)
    s = sin.reshape(bshape).astype(x.dtype)
    x1, x2, rest = x[..., :half], x[..., half:2 * half], x[..., 2 * half:]
    return jnp.concatenate([x1 * c - x2 * s, x2 * c + x1 * s, rest], axis=-1)


def masked_softmax(s, mask):
    s = jnp.where(mask, s.astype(jnp.float32), NEG_INF)
    return jnp.where(mask, jax.nn.softmax(s, axis=-1), 0.0)


def swiglu_ffn(x, g, w_gate, w_up, w_down):
    h = rmsnorm(x, g)
    return (jax.nn.silu(h @ w_gate) * (h @ w_up)) @ w_down


def split_cols(z, sizes):
    return jnp.split(z, [int(c) for c in np.cumsum(sizes)[:-1]], axis=-1)


def rows(a, b, q0):
    return lax.dynamic_slice_in_dim(a[b], q0, Q_BLOCK, axis=0)


def map_query_blocks(fn, batch, seq):
    n_qb = seq // Q_BLOCK

    def step(i):
        return fn(i // n_qb, (i % n_qb) * Q_BLOCK)

    out = lax.map(step, jnp.arange(batch * n_qb, dtype=jnp.int32))
    return out.reshape((batch, seq) + out.shape[2:])


def chunked_gmlp(z, sgu_norm, sgu_w, sgu_b):
    bsz, seq, _ = z.shape
    u, v = jnp.split(jax.nn.gelu(z), 2, axis=-1)
    v = rmsnorm(v, sgu_norm).reshape(bsz, seq // A_CHUNK, A_CHUNK, A_GROUPS, HEAD_DIM)
    causal = jnp.tril(jnp.ones((A_CHUNK, A_CHUNK), dtype=bool))
    w = jnp.where(causal, sgu_w, jnp.zeros_like(sgu_w))
    mixed = jnp.einsum('gts,bcsgd->bctgd', w, v) + sgu_b.T[:, :, None]
    return u * mixed.reshape(bsz, seq, A_WIDTH)


def compress_blocks(t, pos_emb, w1, w2):
    bsz, seq = t.shape[:2]
    n_cmp = (seq - CMP_LEN) // CMP_STRIDE + 1
    idx = jnp.arange(n_cmp)[:, None] * CMP_STRIDE + jnp.arange(CMP_LEN)[None, :]
    blk = t[:, idx] + pos_emb[:, None, :]
    blk = jnp.moveaxis(blk, 3, 2).reshape(bsz, n_cmp, B_KV_HEADS, CMP_LEN * HEAD_DIM)
    return jax.nn.gelu(blk @ w1) @ w2


def cmp_to_slc_overlap(n_cmp, n_slc):
    c0 = np.arange(n_cmp) * CMP_STRIDE
    s0 = np.arange(n_slc) * SLC_BLOCK
    m = (c0[:, None] < s0[None, :] + SLC_BLOCK) & (c0[:, None] + CMP_LEN > s0[None, :])
    return jnp.asarray(m, dtype=jnp.float32)


def nsa_attention(q_rot, q_raw, k_cmp, v_cmp, k_slc, v_slc, k_win, v_win, gates):
    bsz, seq = q_rot.shape[:2]
    grp = B_HEADS // B_KV_HEADS
    scale = HEAD_DIM ** -0.5
    n_cmp = k_cmp.shape[1]
    n_slc = seq // SLC_BLOCK
    top_n = min(SLC_TOPN, n_slc)
    overlap = cmp_to_slc_overlap(n_cmp, n_slc)
    cmp_end = jnp.arange(n_cmp) * CMP_STRIDE + CMP_LEN - 1
    blk_id = jnp.arange(n_slc)
    k_blk = k_slc.reshape(bsz, n_slc, SLC_BLOCK, B_KV_HEADS, HEAD_DIM).transpose(0, 3, 1, 2, 4)
    v_blk = v_slc.reshape(bsz, n_slc, SLC_BLOCK, B_KV_HEADS, HEAD_DIM).transpose(0, 3, 1, 2, 4)
    pad = jnp.zeros((bsz, WINDOW, B_KV_HEADS, HEAD_DIM), k_win.dtype)
    k_win_p = jnp.concatenate([pad, k_win], axis=1)
    v_win_p = jnp.concatenate([pad, v_win], axis=1)
    g_idx = jnp.arange(B_KV_HEADS)[:, None, None]
    win_off = jnp.arange(Q_BLOCK + WINDOW) - WINDOW

    def block(b, q0):
        t = q0 + jnp.arange(Q_BLOCK)
        qr = rows(q_rot, b, q0).reshape(Q_BLOCK, B_KV_HEADS, grp, HEAD_DIM)
        qn = rows(q_raw, b, q0).reshape(Q_BLOCK, B_KV_HEADS, grp, HEAD_DIM)
        s_c = jnp.einsum('qgrd,ngd->grqn', qn, k_cmp[b]) * scale
        p_c = masked_softmax(s_c, cmp_end[None, :] <= t[:, None])
        o_c = jnp.einsum('grqn,ngd->qgrd', p_c.astype(v_cmp.dtype), v_cmp[b])
        imp = jnp.einsum('grqn,nj->gqj', p_c, overlap)
        admissible = blk_id[None, :] * SLC_BLOCK <= t[:, None]
        forced = (blk_id[None, :] == 0) | (blk_id[None, :] == t[:, None] // SLC_BLOCK)
        imp = jnp.where(admissible, jnp.where(forced, FORCE_SCORE, imp), NEG_INF)
        _, sel = lax.top_k(imp, top_n)
        sel_ok = jnp.take_along_axis(jnp.broadcast_to(admissible, imp.shape), sel, axis=-1)
        ks = k_blk[b][g_idx, sel].reshape(B_KV_HEADS, Q_BLOCK, top_n * SLC_BLOCK, HEAD_DIM)
        vs = v_blk[b][g_idx, sel].reshape(B_KV_HEADS, Q_BLOCK, top_n * SLC_BLOCK, HEAD_DIM)
        tok = (sel[..., None] * SLC_BLOCK + jnp.arange(SLC_BLOCK)).reshape(B_KV_HEADS, Q_BLOCK, top_n * SLC_BLOCK)
        m_s = (tok <= t[None, :, None]) & jnp.repeat(sel_ok, SLC_BLOCK, axis=-1)
        s_s = jnp.einsum('qgrd,gqkd->grqk', qr, ks) * scale
        p_s = masked_softmax(s_s, m_s[:, None])
        o_s = jnp.einsum('grqk,gqkd->qgrd', p_s.astype(vs.dtype), vs)
        kw = lax.dynamic_slice_in_dim(k_win_p[b], q0, Q_BLOCK + WINDOW, axis=0)
        vw = lax.dynamic_slice_in_dim(v_win_p[b], q0, Q_BLOCK + WINDOW, axis=0)
        s_pos = q0 + win_off
        m_w = (s_pos[None, :] <= t[:, None]) & (s_pos[None, :] > t[:, None] - WINDOW) & (s_pos[None, :] >= 0)
        s_w = jnp.einsum('qgrd,kgd->grqk', qr, kw) * scale
        p_w = masked_softmax(s_w, m_w)
        o_w = jnp.einsum('grqk,kgd->qgrd', p_w.astype(vw.dtype), vw)
        g = rows(gates, b, q0).reshape(Q_BLOCK, B_KV_HEADS, grp, N_BRANCH)
        o = g[..., 0:1] * o_c + g[..., 1:2] * o_s + g[..., 2:3] * o_w
        return o.reshape(Q_BLOCK, B_WIDTH)

    return map_query_blocks(block, bsz, seq)


def dsa_attention(q, k, v, q_idx, k_idx, w_idx):
    bsz, seq = q.shape[:2]
    top_k = min(DSA_TOPK, seq // 4)
    scale = HEAD_DIM ** -0.5
    key_pos = jnp.arange(seq)

    def block(b, q0):
        t = q0 + jnp.arange(Q_BLOCK)
        qb = rows(q, b, q0)
        logits = jnp.einsum('qhd,sd->qhs', rows(q_idx, b, q0), k_idx[b]).astype(jnp.float32) * IDX_DIM ** -0.5
        score = jnp.einsum('qh,qhs->qs', rows(w_idx, b, q0).astype(jnp.float32), jax.nn.relu(logits)) * IDX_HEADS ** -0.5
        score = jnp.where(key_pos[None, :] <= t[:, None], score, NEG_INF)
        _, sel = lax.top_k(score, top_k)
        ok = sel <= t[:, None]
        ks = k[b][sel]
        vs = v[b][sel]
        s = jnp.einsum('qhd,qkd->hqk', qb, ks) * scale
        p = masked_softmax(s, ok[None])
        o = jnp.einsum('hqk,qkd->qhd', p.astype(vs.dtype), vs)
        return o.reshape(Q_BLOCK, C_WIDTH)

    return map_query_blocks(block, bsz, seq)


def moba_attention(q, k, v):
    bsz, seq = q.shape[:2]
    n_blk = -(-seq // MOBA_BLOCK)
    pad = n_blk * MOBA_BLOCK - seq
    scale = HEAD_DIM ** -0.5
    k_blk = jnp.pad(k, ((0, 0), (0, pad), (0, 0), (0, 0))).reshape(bsz, n_blk, MOBA_BLOCK, D_HEADS, HEAD_DIM).transpose(0, 3, 1, 2, 4)
    v_blk = jnp.pad(v, ((0, 0), (0, pad), (0, 0), (0, 0))).reshape(bsz, n_blk, MOBA_BLOCK, D_HEADS, HEAD_DIM).transpose(0, 3, 1, 2, 4)
    k_mean = k_blk.mean(axis=3)
    top_k = min(MOBA_TOPK, n_blk - 1)
    blk_id = jnp.arange(n_blk)
    h_idx = jnp.arange(D_HEADS)[:, None, None]

    def block(b, q0):
        t = q0 + jnp.arange(Q_BLOCK)
        qb = rows(q, b, q0)
        own = q0 // MOBA_BLOCK
        k_own = lax.dynamic_index_in_dim(k_blk[b], own, axis=1, keepdims=False)
        v_own = lax.dynamic_index_in_dim(v_blk[b], own, axis=1, keepdims=False)
        own_pos = own * MOBA_BLOCK + jnp.arange(MOBA_BLOCK)
        s_own = jnp.einsum('qhd,hkd->hqk', qb, k_own) * scale
        m_own = jnp.broadcast_to(own_pos[None, :] <= t[:, None], s_own.shape)
        if top_k == 0:
            p = masked_softmax(s_own, m_own)
            o = jnp.einsum('hqk,hkd->qhd', p.astype(v_own.dtype), v_own)
        else:
            gate = jnp.einsum('qhd,hjd->hqj', qb, k_mean[b]).astype(jnp.float32)
            gate = jnp.where(blk_id < own, gate, NEG_INF)
            _, sel = lax.top_k(gate, top_k)
            ok = sel < own
            n_sel = top_k * MOBA_BLOCK
            ks = k_blk[b][h_idx, sel].reshape(D_HEADS, Q_BLOCK, n_sel, HEAD_DIM)
            vs = v_blk[b][h_idx, sel].reshape(D_HEADS, Q_BLOCK, n_sel, HEAD_DIM)
            s_sel = jnp.einsum('qhd,hqkd->hqk', qb, ks) * scale
            m_sel = jnp.repeat(ok, MOBA_BLOCK, axis=-1)
            p = masked_softmax(jnp.concatenate([s_sel, s_own], axis=-1), jnp.concatenate([m_sel, m_own], axis=-1))
            o = (jnp.einsum('hqk,hqkd->qhd', p[..., :n_sel].astype(vs.dtype), vs)
                 + jnp.einsum('hqk,hkd->qhd', p[..., n_sel:].astype(v_own.dtype), v_own))
        return o.reshape(Q_BLOCK, D_WIDTH)

    return map_query_blocks(block, bsz, seq)


def even_mixer(h, cos, sin, w_in, sgu_norm, sgu_w, sgu_b, cmp_pos_k, cmp_w1_k, cmp_w2_k,
               cmp_pos_v, cmp_w1_v, cmp_w2_v, w_out):
    bsz, seq, _ = h.shape
    a_in, q, kc, vc, ksl, vsl, kw, vw, gl = split_cols(h @ w_in, EVEN_SPLITS)

    def kv(t):
        return t.reshape(bsz, seq, B_KV_HEADS, HEAD_DIM)

    q_raw = q.reshape(bsz, seq, B_HEADS, HEAD_DIM)
    a_out = chunked_gmlp(a_in, sgu_norm, sgu_w, sgu_b)
    b_out = nsa_attention(partial_rope(q_raw, cos, sin), q_raw,
                          compress_blocks(kv(kc), cmp_pos_k, cmp_w1_k, cmp_w2_k),
                          compress_blocks(kv(vc), cmp_pos_v, cmp_w1_v, cmp_w2_v),
                          partial_rope(kv(ksl), cos, sin), kv(vsl),
                          partial_rope(kv(kw), cos, sin), kv(vw),
                          jax.nn.sigmoid(gl).reshape(bsz, seq, B_HEADS, N_BRANCH))
    return jnp.concatenate([a_out, b_out], axis=-1) @ w_out


def odd_mixer(h, cos, sin, cos_i, sin_i, w_in, w_out):
    bsz, seq, _ = h.shape
    qc, kc, vc, qi, ki, wi, qd, kd, vd = split_cols(h @ w_in, ODD_SPLITS)
    c_out = dsa_attention(partial_rope(qc.reshape(bsz, seq, C_HEADS, HEAD_DIM), cos, sin),
                          partial_rope(kc, cos, sin), vc,
                          partial_rope(qi.reshape(bsz, seq, IDX_HEADS, IDX_DIM), cos_i, sin_i),
                          partial_rope(ki, cos_i, sin_i), wi)
    d_out = moba_attention(partial_rope(qd.reshape(bsz, seq, D_HEADS, HEAD_DIM), cos, sin),
                           partial_rope(kd.reshape(bsz, seq, D_HEADS, HEAD_DIM), cos, sin),
                           vd.reshape(bsz, seq, D_HEADS, HEAD_DIM))
    return jnp.concatenate([c_out, d_out], axis=-1) @ w_out


def setup_inputs(seed: int = 0) -> dict:
    key = jax.random.key(seed)
    keys = iter(jax.random.split(key, 40))

    def nrm(shape, scale):
        return jax.random.normal(next(keys), shape, jnp.float32) * scale

    def gain(shape):
        return 1.0 + nrm(shape, 0.1)

    x = nrm((BATCH, SEQ, D_MODEL), 1.0)
    offset = jax.random.randint(next(keys), (BATCH, 1), 0, 4096, dtype=jnp.int32)
    positions = offset + jnp.arange(SEQ, dtype=jnp.int32)[None, :]
    return {
        'x': x,
        'positions': positions,
        'ffn1_norm': gain((DEPTH, D_MODEL)),
        'ffn1_w_gate': nrm((DEPTH, D_MODEL, D_FF), D_MODEL ** -0.5),
        'ffn1_w_up': nrm((DEPTH, D_MODEL, D_FF), D_MODEL ** -0.5),
        'ffn1_w_down': nrm((DEPTH, D_FF, D_MODEL), D_FF ** -0.5),
        'mix_norm': gain((DEPTH, D_MODEL)),
        'ffn2_norm': gain((DEPTH, D_MODEL)),
        'ffn2_w_gate': nrm((DEPTH, D_MODEL, D_FF), D_MODEL ** -0.5),
        'ffn2_w_up': nrm((DEPTH, D_MODEL, D_FF), D_MODEL ** -0.5),
        'ffn2_w_down': nrm((DEPTH, D_FF, D_MODEL), D_FF ** -0.5),
        'ev_w_in': nrm((N_EVEN, D_MODEL, EVEN_IN), D_MODEL ** -0.5),
        'ev_sgu_norm': gain((N_EVEN, A_WIDTH)),
        'ev_sgu_w': nrm((N_EVEN, A_GROUPS, A_CHUNK, A_CHUNK), A_CHUNK ** -0.5),
        'ev_sgu_b': gain((N_EVEN, A_GROUPS, A_CHUNK)),
        'ev_cmp_pos_k': nrm((N_EVEN, CMP_LEN, HEAD_DIM), 0.1),
        'ev_cmp_w1_k': nrm((N_EVEN, CMP_LEN * HEAD_DIM, CMP_HIDDEN), (CMP_LEN * HEAD_DIM) ** -0.5),
        'ev_cmp_w2_k': nrm((N_EVEN, CMP_HIDDEN, HEAD_DIM), CMP_HIDDEN ** -0.5),
        'ev_cmp_pos_v': nrm((N_EVEN, CMP_LEN, HEAD_DIM), 0.1),
        'ev_cmp_w1_v': nrm((N_EVEN, CMP_LEN * HEAD_DIM, CMP_HIDDEN), (CMP_LEN * HEAD_DIM) ** -0.5),
        'ev_cmp_w2_v': nrm((N_EVEN, CMP_HIDDEN, HEAD_DIM), CMP_HIDDEN ** -0.5),
        'ev_w_out': nrm((N_EVEN, MIX_OUT, D_MODEL), MIX_OUT ** -0.5),
        'od_w_in': nrm((N_ODD, D_MODEL, ODD_IN), D_MODEL ** -0.5),
        'od_w_out': nrm((N_ODD, MIX_OUT, D_MODEL), MIX_OUT ** -0.5),
        'final_norm': gain((D_MODEL,)),
    }


def reference(x, positions, ffn1_norm, ffn1_w_gate, ffn1_w_up, ffn1_w_down, mix_norm,
              ffn2_norm, ffn2_w_gate, ffn2_w_up, ffn2_w_down, ev_w_in, ev_sgu_norm, ev_sgu_w,
              ev_sgu_b, ev_cmp_pos_k, ev_cmp_w1_k, ev_cmp_w2_k, ev_cmp_pos_v, ev_cmp_w1_v,
              ev_cmp_w2_v, ev_w_out, od_w_in, od_w_out, final_norm):
    cos, sin = rope_tables(positions, ROT_DIM)
    cos_i, sin_i = rope_tables(positions, IDX_ROT)
    for i in range(DEPTH):
        x = x + 0.5 * swiglu_ffn(x, ffn1_norm[i], ffn1_w_gate[i], ffn1_w_up[i], ffn1_w_down[i])
        h = rmsnorm(x, mix_norm[i])
        if i % 2 == 0:
            e = i // 2
            x = x + even_mixer(h, cos, sin, ev_w_in[e], ev_sgu_norm[e], ev_sgu_w[e], ev_sgu_b[e],
                               ev_cmp_pos_k[e], ev_cmp_w1_k[e], ev_cmp_w2_k[e],
                               ev_cmp_pos_v[e], ev_cmp_w1_v[e], ev_cmp_w2_v[e], ev_w_out[e])
        else:
            o = i // 2
            x = x + odd_mixer(h, cos, sin, cos_i, sin_i, od_w_in[o], od_w_out[o])
        x = x + 0.5 * swiglu_ffn(x, ffn2_norm[i], ffn2_w_gate[i], ffn2_w_up[i], ffn2_w_down[i])
    return rmsnorm(x, final_norm)
```

```python
import functools

import numpy as np
import jax
import jax.numpy as jnp
from jax import lax
from jax.experimental import pallas as pl
from jax.experimental.pallas import tpu as pltpu

F32 = jnp.float32
BF16 = jnp.bfloat16

D_MODEL = 1024
HEAD_DIM = 64
ROT_DIM = HEAD_DIM // 4
ROPE_THETA = 500000.0
NORM_EPS = 1e-6
Q_BLOCK = 128
D_FF = 2816
NEG_INF = -1e30

A_GROUPS = 4
A_CHUNK = 128
A_WIDTH = A_GROUPS * HEAD_DIM
B_HEADS = 12
B_KV_HEADS = 3
B_GROUP = B_HEADS // B_KV_HEADS
B_WIDTH = B_HEADS * HEAD_DIM
B_KV_WIDTH = B_KV_HEADS * HEAD_DIM
CMP_LEN = 32
CMP_STRIDE = 16
CMP_HIDDEN = 256
SLC_BLOCK = 64
SLC_TOPN = 8
WINDOW = 512
N_BRANCH = 3
FORCE_SCORE = 1e4
C_HEADS = 8
C_WIDTH = C_HEADS * HEAD_DIM
IDX_HEADS = 4
IDX_DIM = 32
IDX_ROT = IDX_DIM // 4
DSA_TOPK = 256
D_HEADS = 8
D_WIDTH = D_HEADS * HEAD_DIM
MOBA_BLOCK = 256
MOBA_TOPK = 3

LANES = 128
ATT_SCALE = HEAD_DIM ** -0.5
VMEM_LIMIT = 56 * 1024 * 1024
TM = 512
FF_CHUNK = D_FF // 2
SEL_TILE = 256
BISECT_ITERS = 40


def _dot(a, b):
    return jnp.dot(a, b, preferred_element_type=F32)


def _dot_nt(a, b):
    return lax.dot_general(a, b, (((1,), (1,)), ((), ())), preferred_element_type=F32)


def _iota(shape, dim):
    return lax.broadcasted_iota(jnp.int32, shape, dim)


def _rms(x, g):
    ms = jnp.mean(x * x, axis=-1, keepdims=True)
    return x * lax.rsqrt(ms + NORM_EPS) * g


def _gelu(x):
    return x * (0.5 * (1.0 + jnp.tanh(0.7978845608028654 * (x + 0.044715 * (x * x * x)))))


def _params(*sem):
    return pltpu.CompilerParams(dimension_semantics=sem, vmem_limit_bytes=VMEM_LIMIT)


def _full(shape):
    n = len(shape)
    return pl.BlockSpec(shape, lambda *_: (0,) * n)


def _resident(shape):
    n = len(shape)
    return pl.BlockSpec(shape, lambda *_: (0,) * n, pipeline_mode=pl.Buffered(1))


def _ffn_kernel(x_ref, g_ref, wg_ref, wu_ref, wd_ref, *rest, final):
    o_ref = rest[-1]
    x = x_ref[...]
    h = _rms(x, g_ref[...]).astype(BF16)
    acc = jnp.zeros_like(x)
    for c in range(D_FF // FF_CHUNK):
        sl = slice(c * FF_CHUNK, (c + 1) * FF_CHUNK)
        gate = _dot(h, wg_ref[:, sl])
        up = _dot(h, wu_ref[:, sl])
        act = (gate * jax.nn.sigmoid(gate) * up).astype(BF16)
        acc = acc + _dot(act, wd_ref[sl, :])
    y = x + 0.5 * acc
    if final:
        y = _rms(y, rest[0][...])
    o_ref[...] = y


def _ffn(x, g, wg, wu, wd, final_g=None):
    t = x.shape[0]
    final = final_g is not None
    ins = [x, g.reshape(1, D_MODEL), wg.astype(BF16), wu.astype(BF16), wd.astype(BF16)]
    specs = [pl.BlockSpec((TM, D_MODEL), lambda i: (i, 0)), _full((1, D_MODEL)),
             _resident((D_MODEL, D_FF)), _resident((D_MODEL, D_FF)), _resident((D_FF, D_MODEL))]
    if final:
        ins.append(final_g.reshape(1, D_MODEL))
        specs.append(_full((1, D_MODEL)))
    return pl.pallas_call(
        functools.partial(_ffn_kernel, final=final),
        grid=(t // TM,),
        in_specs=specs,
        out_specs=pl.BlockSpec((TM, D_MODEL), lambda i: (i, 0)),
        out_shape=jax.ShapeDtypeStruct((t, D_MODEL), F32),
        compiler_params=_params("parallel"),
        name="ffn_final" if final else "ffn",
    )(*ins)


def _rope_tiles(z, tab_ref, shift):
    cs, s1, s2 = tab_ref[0], tab_ref[1], tab_ref[2]
    outs = []
    for c in range(z.shape[1] // LANES):
        zt = z[:, c * LANES:(c + 1) * LANES]
        up = pltpu.roll(zt, LANES - shift, axis=1)
        dn = pltpu.roll(zt, shift, axis=1)
        outs.append(zt * cs + up * s1 + dn * s2)
    return outs[0] if len(outs) == 1 else jnp.concatenate(outs, axis=1)


def _proj_kernel(*refs, kinds):
    n_sec = len(kinds)
    x_ref, g_ref, t64_ref, t32_ref = refs[:4]
    w_refs = refs[4:4 + n_sec]
    o_refs = refs[4 + n_sec:]
    h = _rms(x_ref[...], g_ref[...]).astype(BF16)
    oi = 0
    for kind, w_ref in zip(kinds, w_refs):
        z = _dot(h, w_ref[...])
        if kind == "plain":
            o_refs[oi][...] = z
        elif kind == "sigmoid":
            o_refs[oi][...] = jax.nn.sigmoid(z)
        elif kind == "rope64":
            o_refs[oi][...] = _rope_tiles(z, t64_ref, ROT_DIM // 2)
        elif kind == "rope32":
            o_refs[oi][...] = _rope_tiles(z, t32_ref, IDX_ROT // 2)
        elif kind == "both64":
            o_refs[oi][...] = z
            oi += 1
            o_refs[oi][...] = _rope_tiles(z, t64_ref, ROT_DIM // 2)
        oi += 1


def _proj(x, g, tab64, tab32, weights, kinds):
    t = x.shape[0]
    widths = []
    for kind, w in zip(kinds, weights):
        widths.extend([w.shape[1]] * (2 if kind == "both64" else 1))
    specs = [pl.BlockSpec((TM, D_MODEL), lambda i: (i, 0)), _full((1, D_MODEL)),
             pl.BlockSpec((3, TM, LANES), lambda i: (0, i, 0)),
             pl.BlockSpec((3, TM, LANES), lambda i: (0, i, 0))]
    specs += [_resident(w.shape) for w in weights]
    return pl.pallas_call(
        functools.partial(_proj_kernel, kinds=tuple(kinds)),
        grid=(t // TM,),
        in_specs=specs,
        out_specs=[pl.BlockSpec((TM, n), lambda i: (i, 0)) for n in widths],
        out_shape=[jax.ShapeDtypeStruct((t, n), F32) for n in widths],
        compiler_params=_params("parallel"),
        name="mixer_in_proj",
    )(x, g.reshape(1, D_MODEL), tab64, tab32, *[w.astype(BF16) for w in weights])


def _rope_tables(positions, rot_dim, period):
    half = rot_dim // 2
    inv_freq = ROPE_THETA ** (-jnp.arange(0, rot_dim, 2, dtype=F32) / rot_dim)
    ang = positions.astype(F32).reshape(-1, 1) * inv_freq
    cos, sin = jnp.cos(ang), jnp.sin(ang)
    t = cos.shape[0]
    one = jnp.ones((t, period - 2 * half), F32)
    zero = jnp.zeros((t, period - 2 * half), F32)
    zh = jnp.zeros((t, half), F32)
    cs = jnp.concatenate([cos, cos, one], axis=1)
    s1 = jnp.concatenate([-sin, zh, zero], axis=1)
    s2 = jnp.concatenate([zh, sin, zero], axis=1)
    rep = LANES // period
    return jnp.stack([jnp.tile(a, (1, rep)) for a in (cs, s1, s2)])


def _outproj_kernel(x_ref, a_ref, b_ref, wa_ref, wb_ref, o_ref):
    y = _dot(a_ref[...].astype(BF16), wa_ref[...]) + _dot(b_ref[...].astype(BF16), wb_ref[...])
    o_ref[...] = x_ref[...] + y


def _outproj(x, a, b, w_out):
    t = x.shape[0]
    na, nb = a.shape[1], b.shape[1]
    wa, wb = w_out[:na].astype(BF16), w_out[na:].astype(BF16)
    return pl.pallas_call(
        _outproj_kernel,
        grid=(t // TM,),
        in_specs=[pl.BlockSpec((TM, D_MODEL), lambda i: (i, 0)),
                  pl.BlockSpec((TM, na), lambda i: (i, 0)),
                  pl.BlockSpec((TM, nb), lambda i: (i, 0)),
                  _resident((na, D_MODEL)), _resident((nb, D_MODEL))],
        out_specs=pl.BlockSpec((TM, D_MODEL), lambda i: (i, 0)),
        out_shape=jax.ShapeDtypeStruct((t, D_MODEL), F32),
        compiler_params=_params("parallel"),
        name="mixer_out_proj",
    )(x, a, b, wa, wb)


def _gmlp_kernel(a_ref, n_ref, w_ref, b_ref, o_ref):
    z = _gelu(a_ref[...])
    u = z[:, :A_WIDTH]
    v = _rms(z[:, A_WIDTH:], n_ref[...]).astype(BF16)
    causal = _iota((A_CHUNK, A_CHUNK), 1) <= _iota((A_CHUNK, A_CHUNK), 0)
    lane_group = _iota((A_CHUNK, A_WIDTH), 1) // HEAD_DIM
    mixed = jnp.zeros((A_CHUNK, A_WIDTH), F32)
    for g in range(A_GROUPS):
        w = jnp.where(causal, w_ref[g], 0.0).astype(BF16)
        mixed = jnp.where(lane_group == g, _dot(w, v) + b_ref[g], mixed)
    o_ref[...] = u * mixed


def _gmlp(a_in, sgu_norm, sgu_w, sgu_b):
    t = a_in.shape[0]
    return pl.pallas_call(
        _gmlp_kernel,
        grid=(t // A_CHUNK,),
        in_specs=[pl.BlockSpec((A_CHUNK, 2 * A_WIDTH), lambda i: (i, 0)),
                  _full((1, A_WIDTH)), _full((A_GROUPS, A_CHUNK, A_CHUNK)),
                  _full((A_GROUPS, A_CHUNK, 1))],
        out_specs=pl.BlockSpec((A_CHUNK, A_WIDTH), lambda i: (i, 0)),
        out_shape=jax.ShapeDtypeStruct((t, A_WIDTH), F32),
        compiler_params=_params("parallel"),
        name="gmlp",
    )(a_in, sgu_norm.reshape(1, A_WIDTH), sgu_w, sgu_b.reshape(A_GROUPS, A_CHUNK, 1))


def _compress_one(t2, pos_ref, w1_ref, w2_ref):
    a = _dot((t2 + pos_ref[0:1, :]).astype(BF16), w1_ref[0])
    b = _dot((t2 + pos_ref[1:2, :]).astype(BF16), w1_ref[1])
    hid = _gelu(a + pltpu.roll(b, t2.shape[0] - 1, axis=0))
    out = _dot(hid.astype(BF16), w2_ref[...])
    row = _iota(out.shape, 0)
    return jnp.where(row < t2.shape[0] - 1, out, 0.0)


def _compress_kernel(kc_ref, vc_ref, pk_ref, w1k_ref, w2k_ref, pv_ref, w1v_ref, w2v_ref, ko_ref, vo_ref):
    for g in range(B_KV_HEADS):
        ko_ref[0, g] = _compress_one(kc_ref[0, g], pk_ref, w1k_ref, w2k_ref)
        vo_ref[0, g] = _compress_one(vc_ref[0, g], pv_ref, w1v_ref, w2v_ref)


def _compress(kc2, vc2, pos_k, w1_k, w2_k, pos_v, w1_v, w2_v):
    bsz, _, nrow, wide = kc2.shape
    half = wide
    def prep(pos, w1, w2):
        return (pos.reshape(2, half), w1.reshape(2, half, CMP_HIDDEN).astype(BF16), w2.astype(BF16))
    pk, w1k, w2k = prep(pos_k, w1_k, w2_k)
    pv, w1v, w2v = prep(pos_v, w1_v, w2_v)
    blk_in = pl.BlockSpec((1, B_KV_HEADS, nrow, wide), lambda b: (b, 0, 0, 0))
    blk_out = pl.BlockSpec((1, B_KV_HEADS, nrow, HEAD_DIM), lambda b: (b, 0, 0, 0))
    wspecs = [_full((2, half)), _full((2, half, CMP_HIDDEN)), _full((CMP_HIDDEN, HEAD_DIM))]
    return pl.pallas_call(
        _compress_kernel,
        grid=(bsz,),
        in_specs=[blk_in, blk_in] + wspecs + wspecs,
        out_specs=[blk_out, blk_out],
        out_shape=[jax.ShapeDtypeStruct((bsz, B_KV_HEADS, nrow, HEAD_DIM), F32)] * 2,
        compiler_params=_params("parallel"),
        name="nsa_compress",
    )(kc2, vc2, pk, w1k, w2k, pv, w1v, w2v)


def _rank_lt(vals, n, k):
    j = _iota(vals.shape, 0)
    rank = jnp.zeros(vals.shape, F32)
    for jp in range(n):
        row = vals[jp:jp + 1, :]
        beats = jnp.where(row > vals, 1.0, jnp.where(row == vals, jnp.where(j > jp, 1.0, 0.0), 0.0))
        rank = rank + beats
    return jnp.where(rank < k, 1.0, 0.0)


def _to_rows(sel_t):
    n = sel_t.shape[0]
    pad = jnp.concatenate([sel_t, jnp.zeros((LANES - n, LANES), F32)], axis=0)
    return pad.T.astype(BF16)


def _nsa_kernel(qr_ref, qn_ref, gt_ref, kc_ref, vc_ref, ks_ref, vs_ref, kw_ref, vw_ref, ovt_ref,
                o_ref, s_scr, w_scr):
    i = pl.program_id(1)
    q0 = i * Q_BLOCK
    rows = B_GROUP * Q_BLOCK
    n_slc = ovt_ref.shape[0]
    gates = gt_ref[0]
    t_row = q0 + (_iota((rows, LANES), 0) & (Q_BLOCK - 1))
    lane = _iota((rows, LANES), 1)

    for g in range(B_KV_HEADS):
        hs = slice(g * B_GROUP, (g + 1) * B_GROUP)
        qr = (qr_ref[0, hs].reshape(rows, HEAD_DIM) * ATT_SCALE).astype(BF16)
        qn = (qn_ref[0, hs].reshape(rows, HEAD_DIM) * ATT_SCALE).astype(BF16)

        s_c = _dot_nt(qn, kc_ref[0, g].astype(BF16))
        m_c = (lane * CMP_STRIDE + (CMP_LEN - 1)) <= t_row
        sm = jnp.where(m_c, s_c, NEG_INF)
        e = jnp.where(m_c, jnp.exp(sm - jnp.max(sm, axis=-1, keepdims=True)), 0.0)
        den = jnp.sum(e, axis=-1, keepdims=True)
        p_c = e / jnp.where(den > 0.0, den, 1.0)
        p_cb = p_c.astype(BF16)
        o_c = _dot(p_cb, vc_ref[0, g].astype(BF16))

        imp = jnp.zeros((n_slc, Q_BLOCK), F32)
        for r in range(B_GROUP):
            imp = imp + _dot_nt(ovt_ref[...], p_cb[r * Q_BLOCK:(r + 1) * Q_BLOCK])
        jb = _iota((n_slc, Q_BLOCK), 0)
        tq = q0 + _iota((n_slc, Q_BLOCK), 1)
        forced = (jb == 0) | (jb == (tq >> 6))
        imp = jnp.where(jb * SLC_BLOCK <= tq, jnp.where(forced, FORCE_SCORE, imp), NEG_INF)
        sel = _to_rows(_rank_lt(imp, n_slc, SLC_TOPN))

        n_tiles = (i + 2) // 2
        e_row = _iota((LANES, SEL_TILE), 0)
        e_col = _iota((LANES, SEL_TILE), 1)
        tq2 = q0 + _iota((Q_BLOCK, SEL_TILE), 0)
        key2 = _iota((Q_BLOCK, SEL_TILE), 1)

        def sel_pass1(c, mx):
            base = pl.multiple_of(c * SEL_TILE, SEL_TILE)
            s = _dot_nt(qr, ks_ref[0, g, pl.ds(base, SEL_TILE), :])
            expand = jnp.where(((e_col + base) >> 6) == e_row, 1.0, 0.0).astype(BF16)
            member = _dot(sel, expand)
            ok = jnp.where(key2 + base <= tq2, member, 0.0) > 0.5
            sm_ = jnp.where(ok[None], s.reshape(B_GROUP, Q_BLOCK, SEL_TILE), NEG_INF)
            sm_ = sm_.reshape(rows, SEL_TILE)
            s_scr[c] = sm_
            return jnp.maximum(mx, jnp.maximum(sm_[:, :LANES], sm_[:, LANES:]))

        mx = lax.fori_loop(0, n_tiles, sel_pass1, jnp.full((rows, LANES), NEG_INF, F32))
        m_s = jnp.max(mx, axis=-1, keepdims=True)

        def sel_pass2(c, carry):
            acc, den_ = carry
            base = pl.multiple_of(c * SEL_TILE, SEL_TILE)
            p = jnp.exp(s_scr[c] - m_s)
            acc = acc + _dot(p.astype(BF16), vs_ref[0, g, pl.ds(base, SEL_TILE), :])
            return acc, den_ + (p[:, :LANES] + p[:, LANES:])

        acc, den_ = lax.fori_loop(0, n_tiles, sel_pass2,
                                  (jnp.zeros((rows, HEAD_DIM), F32), jnp.zeros((rows, LANES), F32)))
        o_s = acc / jnp.sum(den_, axis=-1, keepdims=True)

        kt_lo = jnp.maximum(i - WINDOW // Q_BLOCK, 0)

        def win_pass1(kt, mx_):
            base = pl.multiple_of(kt * Q_BLOCK, Q_BLOCK)
            s = _dot_nt(qr, kw_ref[0, g, pl.ds(base, Q_BLOCK), :])
            key = lane + base
            ok = (key <= t_row) & (key > t_row - WINDOW)
            sm_ = jnp.where(ok, s, NEG_INF)
            w_scr[kt - kt_lo] = sm_
            return jnp.maximum(mx_, sm_)

        mxw = lax.fori_loop(kt_lo, i + 1, win_pass1, jnp.full((rows, LANES), NEG_INF, F32))
        m_w = jnp.max(mxw, axis=-1, keepdims=True)

        def win_pass2(kt, carry):
            acc_, d_ = carry
            base = pl.multiple_of(kt * Q_BLOCK, Q_BLOCK)
            p = jnp.exp(w_scr[kt - kt_lo] - m_w)
            acc_ = acc_ + _dot(p.astype(BF16), vw_ref[0, g, pl.ds(base, Q_BLOCK), :])
            return acc_, d_ + p

        accw, denw = lax.fori_loop(kt_lo, i + 1, win_pass2,
                                   (jnp.zeros((rows, HEAD_DIM), F32), jnp.zeros((rows, LANES), F32)))
        o_w = accw / jnp.sum(denw, axis=-1, keepdims=True)

        for r in range(B_GROUP):
            h = g * B_GROUP + r
            rs = slice(r * Q_BLOCK, (r + 1) * Q_BLOCK)
            c0 = h * N_BRANCH
            o_ref[0, h] = (gates[:, c0:c0 + 1] * o_c[rs] + gates[:, c0 + 1:c0 + 2] * o_s[rs]
                           + gates[:, c0 + 2:c0 + 3] * o_w[rs])


def _overlap_t(n_rows, n_slc):
    n = np.arange(n_rows)
    c0 = n * CMP_STRIDE
    s0 = np.arange(n_slc) * SLC_BLOCK
    m = (c0[None, :] < s0[:, None] + SLC_BLOCK) & (c0[None, :] + CMP_LEN > s0[:, None])
    m = m & (n[None, :] < n_rows - 1)
    return jnp.asarray(m, dtype=BF16)


def _nsa(q_rot, q_raw, gates, k_cmp, v_cmp, k_slc, v_slc, k_win, v_win):
    bsz, _, seq, _ = q_rot.shape
    n_qb = seq // Q_BLOCK
    n_slc = seq // SLC_BLOCK
    n_cmp_rows = k_cmp.shape[2]
    ovt = _overlap_t(n_cmp_rows, n_slc)
    q_spec = pl.BlockSpec((1, B_HEADS, Q_BLOCK, HEAD_DIM), lambda b, i: (b, 0, i, 0))
    cmp_spec = pl.BlockSpec((1, B_KV_HEADS, n_cmp_rows, HEAD_DIM), lambda b, i: (b, 0, 0, 0))
    kv_spec = pl.BlockSpec((1, B_KV_HEADS, seq, HEAD_DIM), lambda b, i: (b, 0, 0, 0))
    rows = B_GROUP * Q_BLOCK
    return pl.pallas_call(
        _nsa_kernel,
        grid=(bsz, n_qb),
        in_specs=[q_spec, q_spec, pl.BlockSpec((1, Q_BLOCK, LANES), lambda b, i: (b, i, 0)),
                  cmp_spec, cmp_spec, kv_spec, kv_spec, kv_spec, kv_spec, _full(ovt.shape)],
        out_specs=q_spec,
        out_shape=jax.ShapeDtypeStruct((bsz, B_HEADS, seq, HEAD_DIM), F32),
        scratch_shapes=[pltpu.VMEM((seq // SEL_TILE, rows, SEL_TILE), F32),
                        pltpu.VMEM((WINDOW // Q_BLOCK + 1, rows, LANES), F32)],
        compiler_params=_params("parallel", "arbitrary"),
        name="nsa_attention",
    )(q_rot, q_raw, gates, k_cmp, v_cmp, k_slc, v_slc, k_win, v_win, ovt)


def _dsa_kernel(q_ref, k_ref, v_ref, qi_ref, ki_ref, wi_ref, o_ref,
                sc_scr, mk_scr, s_scr, mx_scr, den_scr, acc_scr, *, top_k):
    i = pl.program_id(1)
    q0 = i * Q_BLOCK
    n_tiles = i + 1
    tile = (Q_BLOCK, LANES)
    key_sub = _iota(tile, 0)
    tq = q0 + _iota(tile, 1)
    w_idx = wi_ref[0]
    qi = [qi_ref[0, h].astype(BF16) for h in range(IDX_HEADS)]

    def score_body(c, carry):
        mn, mx = carry
        base = pl.multiple_of(c * LANES, LANES)
        kit = ki_ref[0, pl.ds(base, LANES), :]
        sc = jnp.zeros(tile, F32)
        for h in range(IDX_HEADS):
            lg = _dot_nt(kit, qi[h]) * (IDX_DIM ** -0.5)
            sc = sc + w_idx[h:h + 1, :] * jnp.maximum(lg, 0.0)
        sc = sc * (IDX_HEADS ** -0.5)
        valid = key_sub + base <= tq
        sc_scr[c] = jnp.where(valid, sc, NEG_INF)
        return (jnp.minimum(mn, jnp.where(valid, sc, -NEG_INF)),
                jnp.maximum(mx, jnp.where(valid, sc, NEG_INF)))

    mn, mx = lax.fori_loop(0, n_tiles, score_body,
                           (jnp.full(tile, -NEG_INF, F32), jnp.full(tile, NEG_INF, F32)))
    lo = jnp.min(mn, axis=0, keepdims=True)
    mx = jnp.max(mx, axis=0, keepdims=True)
    hi = mx + jnp.abs(mx) * 1e-3 + 1.0
    kf = jnp.float32(top_k)

    def bisect_body(_, carry):
        lo_, hi_ = carry
        mid = 0.5 * (lo_ + hi_)
        cnt = lax.fori_loop(0, n_tiles,
                            lambda c, a: a + jnp.where(sc_scr[c] >= mid, 1.0, 0.0),
                            jnp.zeros(tile, F32))
        ge = jnp.sum(cnt, axis=0, keepdims=True) >= kf
        return jnp.where(ge, mid, lo_), jnp.where(ge, hi_, mid)

    lo, hi = lax.fori_loop(0, BISECT_ITERS, bisect_body, (lo, hi))

    def thr_body(c, carry):
        vm, cg = carry
        sc = sc_scr[c]
        below = sc < hi
        return (jnp.maximum(vm, jnp.where(below, sc, NEG_INF)), cg + jnp.where(below, 0.0, 1.0))

    vm, cg = lax.fori_loop(0, n_tiles, thr_body, (jnp.full(tile, NEG_INF, F32), jnp.zeros(tile, F32)))
    thr = jnp.max(vm, axis=0, keepdims=True)
    need = kf - jnp.sum(cg, axis=0, keepdims=True)
    take_all = (tq + 1) <= top_k
    tri = jnp.where(_iota(tile, 1) <= _iota(tile, 0), 1.0, 0.0).astype(BF16)

    def mask_body(c, seen):
        base = c * LANES
        sc = sc_scr[c]
        eq = sc == thr
        prefix = _dot(tri, jnp.where(eq, 1.0, 0.0).astype(BF16)) + seen
        chosen = jnp.where(sc > thr, 1.0, jnp.where(eq, jnp.where(prefix <= need, 1.0, 0.0), 0.0))
        valid = jnp.where(key_sub + base <= tq, 1.0, 0.0)
        mk_scr[c] = jnp.where(take_all, valid, chosen).T
        return prefix[LANES - 1:LANES, :]

    lax.fori_loop(0, n_tiles, mask_body, jnp.zeros((1, LANES), F32))

    rows = C_HEADS * Q_BLOCK
    q = (q_ref[0].reshape(rows, HEAD_DIM) * ATT_SCALE).astype(BF16)
    mx_scr[...] = jnp.full((rows, LANES), NEG_INF, F32)

    def att_pass1(c, _):
        base = pl.multiple_of(c * LANES, LANES)
        s = _dot_nt(q, k_ref[0, pl.ds(base, LANES), :])
        ok = mk_scr[c] > 0.5
        sm = jnp.where(ok[None], s.reshape(C_HEADS, Q_BLOCK, LANES), NEG_INF).reshape(rows, LANES)
        s_scr[c] = sm
        mx_scr[...] = jnp.maximum(mx_scr[...], sm)
        return 0

    lax.fori_loop(0, n_tiles, att_pass1, 0)
    m = jnp.max(mx_scr[...], axis=-1, keepdims=True)
    den_scr[...] = jnp.zeros((rows, LANES), F32)
    acc_scr[...] = jnp.zeros((rows, HEAD_DIM), F32)

    def att_pass2(c, _):
        base = pl.multiple_of(c * LANES, LANES)
        p = jnp.exp(s_scr[c] - m)
        acc_scr[...] += _dot(p.astype(BF16), v_ref[0, pl.ds(base, LANES), :])
        den_scr[...] += p
        return 0

    lax.fori_loop(0, n_tiles, att_pass2, 0)
    o = acc_scr[...] / jnp.sum(den_scr[...], axis=-1, keepdims=True)
    o_ref[0] = o.reshape(C_HEADS, Q_BLOCK, HEAD_DIM)


def _dsa(q, k, v, q_idx, k_idx, w_idx_t):
    bsz, _, seq, _ = q.shape
    n_qb = seq // Q_BLOCK
    top_k = min(DSA_TOPK, seq // 4)
    rows = C_HEADS * Q_BLOCK
    return pl.pallas_call(
        functools.partial(_dsa_kernel, top_k=top_k),
        grid=(bsz, n_qb),
        in_specs=[pl.BlockSpec((1, C_HEADS, Q_BLOCK, HEAD_DIM), lambda b, i: (b, 0, i, 0)),
                  pl.BlockSpec((1, seq, HEAD_DIM), lambda b, i: (b, 0, 0)),
                  pl.BlockSpec((1, seq, HEAD_DIM), lambda b, i: (b, 0, 0)),
                  pl.BlockSpec((1, IDX_HEADS, Q_BLOCK, IDX_DIM), lambda b, i: (b, 0, i, 0)),
                  pl.BlockSpec((1, seq, IDX_DIM), lambda b, i: (b, 0, 0)),
                  pl.BlockSpec((1, IDX_HEADS, Q_BLOCK), lambda b, i: (b, 0, i))],
        out_specs=pl.BlockSpec((1, C_HEADS, Q_BLOCK, HEAD_DIM), lambda b, i: (b, 0, i, 0)),
        out_shape=jax.ShapeDtypeStruct((bsz, C_HEADS, seq, HEAD_DIM), F32),
        scratch_shapes=[pltpu.VMEM((n_qb, Q_BLOCK, LANES), F32),
                        pltpu.VMEM((n_qb, Q_BLOCK, LANES), F32),
                        pltpu.VMEM((n_qb, rows, LANES), F32),
                        pltpu.VMEM((rows, LANES), F32),
                        pltpu.VMEM((rows, LANES), F32),
                        pltpu.VMEM((rows, HEAD_DIM), F32)],
        compiler_params=_params("parallel", "arbitrary"),
        name="dsa_attention",
    )(q, k, v, q_idx, k_idx, w_idx_t)


def _kmean_kernel(k_ref, o_ref):
    x = k_ref[0]
    n_blk = x.shape[0] // MOBA_BLOCK
    o_ref[0] = jnp.sum(x.reshape(n_blk, MOBA_BLOCK, x.shape[1]), axis=1) * (1.0 / MOBA_BLOCK)


def _kmean(k):
    bsz, seq, width = k.shape
    n_blk = seq // MOBA_BLOCK
    return pl.pallas_call(
        _kmean_kernel,
        grid=(bsz,),
        in_specs=[pl.BlockSpec((1, seq, width), lambda b: (b, 0, 0))],
        out_specs=pl.BlockSpec((1, n_blk, width), lambda b: (b, 0, 0)),
        out_shape=jax.ShapeDtypeStruct((bsz, n_blk, width), F32),
        compiler_params=_params("parallel"),
        name="moba_kmean",
    )(k)


def _moba_kernel(q_ref, k_ref, v_ref, km_ref, o_ref, s_scr, *, top_k):
    i = pl.program_id(1)
    q0 = i * Q_BLOCK
    own = q0 // MOBA_BLOCK
    n_blk = km_ref.shape[2]
    jb = _iota((n_blk, Q_BLOCK), 0)
    e_row = _iota((LANES, MOBA_BLOCK), 0)
    tq = q0 + _iota((Q_BLOCK, MOBA_BLOCK), 0)
    key = _iota((Q_BLOCK, MOBA_BLOCK), 1)

    def head_body(h, _):
        qh = q_ref[0, h]
        qs = (qh * ATT_SCALE).astype(BF16)
        gate = _dot_nt(km_ref[0, h].astype(BF16), qh.astype(BF16))
        past = jb < own
        gate = jnp.where(past, gate, NEG_INF)
        sel_t = jnp.where(past, _rank_lt(gate, n_blk, top_k), 0.0)
        sel = _to_rows(sel_t)

        def pass1(j, mx):
            base = pl.multiple_of(j * MOBA_BLOCK, MOBA_BLOCK)
            s = _dot_nt(qs, k_ref[0, h, pl.ds(base, MOBA_BLOCK), :])
            member = _dot(sel, jnp.where(e_row == j, 1.0, 0.0).astype(BF16))
            causal = jnp.where(key + base <= tq, 1.0, 0.0)
            ok = jnp.where(j == own, causal, member) > 0.5
            sm = jnp.where(ok, s, NEG_INF)
            s_scr[j] = sm
            return jnp.maximum(mx, jnp.maximum(sm[:, :LANES], sm[:, LANES:]))

        mx = lax.fori_loop(0, own + 1, pass1, jnp.full((Q_BLOCK, LANES), NEG_INF, F32))
        m = jnp.max(mx, axis=-1, keepdims=True)

        def pass2(j, carry):
            acc, den = carry
            base = pl.multiple_of(j * MOBA_BLOCK, MOBA_BLOCK)
            p = jnp.exp(s_scr[j] - m)
            acc = acc + _dot(p.astype(BF16), v_ref[0, h, pl.ds(base, MOBA_BLOCK), :])
            return acc, den + (p[:, :LANES] + p[:, LANES:])

        acc, den = lax.fori_loop(0, own + 1, pass2,
                                 (jnp.zeros((Q_BLOCK, HEAD_DIM), F32), jnp.zeros((Q_BLOCK, LANES), F32)))
        o_ref[0, h] = acc / jnp.sum(den, axis=-1, keepdims=True)
        return 0

    lax.fori_loop(0, D_HEADS, head_body, 0)


def _moba(q, k, v, k_mean):
    bsz, _, seq, _ = q.shape
    n_qb = seq // Q_BLOCK
    n_blk = seq // MOBA_BLOCK
    top_k = min(MOBA_TOPK, n_blk - 1)
    q_spec = pl.BlockSpec((1, D_HEADS, Q_BLOCK, HEAD_DIM), lambda b, i: (b, 0, i, 0))
    kv_spec = pl.BlockSpec((1, D_HEADS, seq, HEAD_DIM), lambda b, i: (b, 0, 0, 0))
    return pl.pallas_call(
        functools.partial(_moba_kernel, top_k=top_k),
        grid=(bsz, n_qb),
        in_specs=[q_spec, kv_spec, kv_spec,
                  pl.BlockSpec((1, D_HEADS, n_blk, HEAD_DIM), lambda b, i: (b, 0, 0, 0))],
        out_specs=q_spec,
        out_shape=jax.ShapeDtypeStruct((bsz, D_HEADS, seq, HEAD_DIM), F32),
        scratch_shapes=[pltpu.VMEM((n_blk, Q_BLOCK, MOBA_BLOCK), F32)],
        compiler_params=_params("parallel", "arbitrary"),
        name="moba_attention",
    )(q, k, v, k_mean)


def _heads(z, bsz, seq, n_heads, dim=HEAD_DIM):
    return z.reshape(bsz, seq, n_heads, dim).transpose(0, 2, 1, 3)


def _unheads(o):
    bsz, n_heads, seq, dim = o.shape
    return o.transpose(0, 2, 1, 3).reshape(bsz * seq, n_heads * dim)


def _cols(w, start, size):
    return w[:, start:start + size]


def _even_mixer(x, bsz, seq, tab64, tab32, norm_g, w_in, sgu_norm, sgu_w, sgu_b,
                cmp_pos_k, cmp_w1_k, cmp_w2_k, cmp_pos_v, cmp_w1_v, cmp_w2_v, w_out):
    o_a, o_q = 0, 2 * A_WIDTH
    o_kc = o_q + B_WIDTH
    o_vc, o_ksl, o_vsl, o_kw, o_vw = (o_kc + B_KV_WIDTH * n for n in range(1, 6))
    o_gl = o_kc + 6 * B_KV_WIDTH
    n_gate = B_HEADS * N_BRANCH
    w_a = _cols(w_in, o_a, 2 * A_WIDTH)
    w_q = _cols(w_in, o_q, B_WIDTH)
    w_plain = jnp.concatenate([_cols(w_in, o, B_KV_WIDTH) for o in (o_kc, o_vc, o_vsl, o_vw)], axis=1)
    w_rope = jnp.concatenate([_cols(w_in, o, B_KV_WIDTH) for o in (o_ksl, o_kw)], axis=1)
    w_gate = jnp.pad(_cols(w_in, o_gl, n_gate), ((0, 0), (0, LANES - n_gate)))
    a_in, q_raw, q_rot, kv_plain, k_rope, gates = _proj(
        x, norm_g, tab64, tab32, [w_a, w_q, w_plain, w_rope, w_gate],
        ["plain", "both64", "plain", "rope64", "sigmoid"])

    a_out = _gmlp(a_in, sgu_norm, sgu_w, sgu_b)

    kc, vc, vsl, vw = (kv_plain[:, n * B_KV_WIDTH:(n + 1) * B_KV_WIDTH] for n in range(4))
    ksl, kw = (k_rope[:, n * B_KV_WIDTH:(n + 1) * B_KV_WIDTH] for n in range(2))

    def cmp_rows(t):
        t = t.reshape(bsz, seq // CMP_STRIDE, CMP_STRIDE, B_KV_HEADS, HEAD_DIM)
        return t.transpose(0, 3, 1, 2, 4).reshape(bsz, B_KV_HEADS, seq // CMP_STRIDE, CMP_STRIDE * HEAD_DIM)

    k_cmp, v_cmp = _compress(cmp_rows(kc), cmp_rows(vc), cmp_pos_k, cmp_w1_k, cmp_w2_k,
                             cmp_pos_v, cmp_w1_v, cmp_w2_v)

    def kv_heads(t):
        return _heads(t, bsz, seq, B_KV_HEADS).astype(BF16)

    b_out = _nsa(_heads(q_rot, bsz, seq, B_HEADS), _heads(q_raw, bsz, seq, B_HEADS),
                 gates.reshape(bsz, seq, LANES), k_cmp, v_cmp,
                 kv_heads(ksl), kv_heads(vsl), kv_heads(kw), kv_heads(vw))
    return _outproj(x, a_out, _unheads(b_out), w_out)


def _odd_mixer(x, bsz, seq, tab64, tab32, norm_g, w_in, w_out):
    sizes = (C_WIDTH, HEAD_DIM, HEAD_DIM, IDX_HEADS * IDX_DIM, IDX_DIM, IDX_HEADS, D_WIDTH, D_WIDTH, D_WIDTH)
    offs = np.concatenate([[0], np.cumsum(sizes)])
    w_qc, w_kc, w_vc, w_qi, w_ki, w_wi, w_qd, w_kd, w_vd = (
        _cols(w_in, int(o), int(s)) for o, s in zip(offs[:-1], sizes))
    zpad = lambda n: jnp.zeros((D_MODEL, n), w_in.dtype)
    w_rope = jnp.concatenate([w_qc, w_kc, zpad(HEAD_DIM), w_qd, w_kd], axis=1)
    w_ropei = jnp.concatenate([w_qi, w_ki, zpad(2 * LANES - IDX_HEADS * IDX_DIM - IDX_DIM)], axis=1)
    w_plain = jnp.concatenate([w_vc, w_wi, zpad(HEAD_DIM - IDX_HEADS), w_vd], axis=1)
    z_rope, z_ropei, z_plain = _proj(x, norm_g, tab64, tab32, [w_rope, w_ropei, w_plain],
                                     ["rope64", "rope32", "plain"])
    qc = z_rope[:, :C_WIDTH]
    kc = z_rope[:, C_WIDTH:C_WIDTH + HEAD_DIM]
    qd = z_rope[:, C_WIDTH + 2 * HEAD_DIM:C_WIDTH + 2 * HEAD_DIM + D_WIDTH]
    kd = z_rope[:, C_WIDTH + 2 * HEAD_DIM + D_WIDTH:]
    qi = z_ropei[:, :IDX_HEADS * IDX_DIM]
    ki = z_ropei[:, IDX_HEADS * IDX_DIM:IDX_HEADS * IDX_DIM + IDX_DIM]
    vc = z_plain[:, :HEAD_DIM]
    wi = z_plain[:, HEAD_DIM:HEAD_DIM + IDX_HEADS]
    vd = z_plain[:, 2 * HEAD_DIM:]

    c_out = _dsa(_heads(qc, bsz, seq, C_HEADS),
                 kc.reshape(bsz, seq, HEAD_DIM).astype(BF16), vc.reshape(bsz, seq, HEAD_DIM).astype(BF16),
                 _heads(qi, bsz, seq, IDX_HEADS, IDX_DIM), ki.reshape(bsz, seq, IDX_DIM).astype(BF16),
                 wi.reshape(bsz, seq, IDX_HEADS).transpose(0, 2, 1))

    k_mean = _kmean(kd.reshape(bsz, seq, D_WIDTH))
    k_mean = k_mean.reshape(bsz, -1, D_HEADS, HEAD_DIM).transpose(0, 2, 1, 3)
    d_out = _moba(_heads(qd, bsz, seq, D_HEADS), _heads(kd, bsz, seq, D_HEADS).astype(BF16),
                  _heads(vd, bsz, seq, D_HEADS).astype(BF16), k_mean)
    return _outproj(x, _unheads(c_out), _unheads(d_out), w_out)


def kernel(x, positions, ffn1_norm, ffn1_w_gate, ffn1_w_up, ffn1_w_down, mix_norm, ffn2_norm, ffn2_w_gate, ffn2_w_up, ffn2_w_down, ev_w_in, ev_sgu_norm, ev_sgu_w, ev_sgu_b, ev_cmp_pos_k, ev_cmp_w1_k, ev_cmp_w2_k, ev_cmp_pos_v, ev_cmp_w1_v, ev_cmp_w2_v, ev_w_out, od_w_in, od_w_out, final_norm):
    bsz, seq, _ = x.shape
    depth = ffn1_norm.shape[0]
    tab64 = _rope_tables(positions, ROT_DIM, HEAD_DIM)
    tab32 = _rope_tables(positions, IDX_ROT, IDX_DIM)
    x = x.reshape(bsz * seq, D_MODEL)
    for i in range(depth):
        x = _ffn(x, ffn1_norm[i], ffn1_w_gate[i], ffn1_w_up[i], ffn1_w_down[i])
        if i % 2 == 0:
            e = i // 2
            x = _even_mixer(x, bsz, seq, tab64, tab32, mix_norm[i], ev_w_in[e], ev_sgu_norm[e], ev_sgu_w[e],
                            ev_sgu_b[e], ev_cmp_pos_k[e], ev_cmp_w1_k[e], ev_cmp_w2_k[e],
                            ev_cmp_pos_v[e], ev_cmp_w1_v[e], ev_cmp_w2_v[e], ev_w_out[e])
        else:
            o = i // 2
            x = _odd_mixer(x, bsz, seq, tab64, tab32, mix_norm[i], od_w_in[o], od_w_out[o])
        x = _ffn(x, ffn2_norm[i], ffn2_w_gate[i], ffn2_w_up[i], ffn2_w_down[i],
                 final_g=final_norm if i == depth - 1 else None)
    return x.reshape(bsz, seq, D_MODEL)
```

```python
import functools

import numpy as np
import jax
import jax.numpy as jnp
from jax import lax
from jax.experimental import pallas as pl
from jax.experimental.pallas import tpu as pltpu

F32 = jnp.float32
BF16 = jnp.bfloat16

D_MODEL = 1024
HEAD_DIM = 64
ROT_DIM = HEAD_DIM // 4
ROPE_THETA = 500000.0
NORM_EPS = 1e-6
Q_BLOCK = 128
D_FF = 2816
NEG_INF = -1e30

A_GROUPS = 4
A_CHUNK = 128
A_WIDTH = A_GROUPS * HEAD_DIM
B_HEADS = 12
B_KV_HEADS = 3
B_GROUP = B_HEADS // B_KV_HEADS
B_WIDTH = B_HEADS * HEAD_DIM
B_KV_WIDTH = B_KV_HEADS * HEAD_DIM
CMP_LEN = 32
CMP_STRIDE = 16
CMP_HIDDEN = 256
SLC_BLOCK = 64
SLC_TOPN = 8
WINDOW = 512
N_BRANCH = 3
FORCE_SCORE = 1e4
C_HEADS = 8
C_WIDTH = C_HEADS * HEAD_DIM
IDX_HEADS = 4
IDX_DIM = 32
IDX_ROT = IDX_DIM // 4
DSA_TOPK = 256
D_HEADS = 8
D_WIDTH = D_HEADS * HEAD_DIM
MOBA_BLOCK = 256
MOBA_TOPK = 3

LANES = 128
ATT_SCALE = HEAD_DIM ** -0.5
VMEM_LIMIT = 56 * 1024 * 1024
TM = 512
FF_CHUNK = D_FF // 2
KT = 256
BISECT_ITERS = 32
ONES_LANE = HEAD_DIM
AUX_LANE = HEAD_DIM


def _dot(a, b):
    return jnp.dot(a, b, preferred_element_type=F32)


def _dot_nt(a, b):
    return lax.dot_general(a, b, (((1,), (1,)), ((), ())), preferred_element_type=F32)


def _bdot_nt(a, b):
    return lax.dot_general(a, b, (((2,), (2,)), ((0,), (0,))), preferred_element_type=F32)


def _bdot(a, b):
    return lax.dot_general(a, b, (((2,), (1,)), ((0,), (0,))), preferred_element_type=F32)


def _iota(shape, dim):
    return lax.broadcasted_iota(jnp.int32, shape, dim)


def _rms(x, g):
    ms = jnp.mean(x * x, axis=-1, keepdims=True)
    return x * lax.rsqrt(ms + NORM_EPS) * g


def _gelu(x):
    return x * (0.5 * (1.0 + jnp.tanh(0.7978845608028654 * (x + 0.044715 * (x * x * x)))))


def _params(*sem):
    return pltpu.CompilerParams(dimension_semantics=sem, vmem_limit_bytes=VMEM_LIMIT)


def _full(shape):
    n = len(shape)
    return pl.BlockSpec(shape, lambda *_: (0,) * n)


def _resident(shape):
    n = len(shape)
    return pl.BlockSpec(shape, lambda *_: (0,) * n, pipeline_mode=pl.Buffered(1))


def _ffn_kernel(x_ref, g_ref, wg_ref, wu_ref, wd_ref, *rest, final):
    o_ref = rest[-1]
    x = x_ref[...]
    h = _rms(x, g_ref[...]).astype(BF16)
    acc = jnp.zeros_like(x)
    for c in range(D_FF // FF_CHUNK):
        sl = slice(c * FF_CHUNK, (c + 1) * FF_CHUNK)
        gate = _dot(h, wg_ref[:, sl])
        up = _dot(h, wu_ref[:, sl])
        act = (gate * jax.nn.sigmoid(gate) * up).astype(BF16)
        acc = acc + _dot(act, wd_ref[sl, :])
    y = x + 0.5 * acc
    if final:
        y = _rms(y, rest[0][...])
    o_ref[...] = y


def _ffn(x, g, wg, wu, wd, final_g=None):
    t = x.shape[0]
    final = final_g is not None
    ins = [x, g.reshape(1, D_MODEL), wg.astype(BF16), wu.astype(BF16), wd.astype(BF16)]
    specs = [pl.BlockSpec((TM, D_MODEL), lambda i: (i, 0)), _full((1, D_MODEL)),
             _resident((D_MODEL, D_FF)), _resident((D_MODEL, D_FF)), _resident((D_FF, D_MODEL))]
    if final:
        ins.append(final_g.reshape(1, D_MODEL))
        specs.append(_full((1, D_MODEL)))
    return pl.pallas_call(
        functools.partial(_ffn_kernel, final=final),
        grid=(t // TM,),
        in_specs=specs,
        out_specs=pl.BlockSpec((TM, D_MODEL), lambda i: (i, 0)),
        out_shape=jax.ShapeDtypeStruct((t, D_MODEL), F32),
        compiler_params=_params("parallel"),
        name="ffn_final" if final else "ffn",
    )(*ins)


def _rope_tiles(z, tab_ref, shift):
    cs, s1, s2 = tab_ref[0], tab_ref[1], tab_ref[2]
    outs = []
    for c in range(z.shape[1] // LANES):
        zt = z[:, c * LANES:(c + 1) * LANES]
        up = pltpu.roll(zt, LANES - shift, axis=1)
        dn = pltpu.roll(zt, shift, axis=1)
        outs.append(zt * cs + up * s1 + dn * s2)
    return outs[0] if len(outs) == 1 else jnp.concatenate(outs, axis=1)


def _head_row(z, col, width, aux):
    tile = z[:, (col // LANES) * LANES:(col // LANES + 1) * LANES]
    off = col % LANES
    if off:
        tile = pltpu.roll(tile, LANES - off, axis=1)
    lane = _iota(tile.shape, 1)
    return jnp.where(lane < width, tile, aux)


def _seq_pos(sblk, rows):
    return sblk * rows + _iota((rows, LANES), 0)


def _even_proj_kernel(x_ref, g_ref, t64_ref, wa_ref, wq_ref, wp_ref, wr_ref, wg_ref,
                      a_ref, qn_ref, qr_ref, kc_ref, vc_ref, vs_ref, vw_ref, ks_ref, kw_ref, gt_ref,
                      *, n_sblk):
    sblk = pl.program_id(0) % n_sblk
    h = _rms(x_ref[...], g_ref[...]).astype(BF16)
    lane = _iota((TM, LANES), 1)
    ones_col = jnp.where(lane == ONES_LANE, 1.0, 0.0)
    slc_onehot = jnp.where(lane == AUX_LANE + (_seq_pos(sblk, TM) // SLC_BLOCK), 1.0, 0.0)

    a_ref[...] = _dot(h, wa_ref[...])
    zq = _dot(h, wq_ref[...])
    zr = _rope_tiles(zq, t64_ref, ROT_DIM // 2)
    for hh in range(B_HEADS):
        qn_ref[0, hh] = _head_row(zq, hh * HEAD_DIM, HEAD_DIM, 0.0).astype(BF16)
        qr_ref[0, hh] = _head_row(zr, hh * HEAD_DIM, HEAD_DIM, 0.0).astype(BF16)
    zp = _dot(h, wp_ref[...])
    kc_ref[...] = zp[:, :B_KV_WIDTH]
    vc_ref[...] = zp[:, B_KV_WIDTH:2 * B_KV_WIDTH]
    zk = _rope_tiles(_dot(h, wr_ref[...]), t64_ref, ROT_DIM // 2)
    for g in range(B_KV_HEADS):
        vs_ref[0, g] = _head_row(zp, (2 * B_KV_HEADS + g) * HEAD_DIM, HEAD_DIM, ones_col).astype(BF16)
        vw_ref[0, g] = _head_row(zp, (3 * B_KV_HEADS + g) * HEAD_DIM, HEAD_DIM, ones_col).astype(BF16)
        ks_ref[0, g] = _head_row(zk, g * HEAD_DIM, HEAD_DIM, slc_onehot).astype(BF16)
        kw_ref[0, g] = _head_row(zk, (B_KV_HEADS + g) * HEAD_DIM, HEAD_DIM, 0.0).astype(BF16)
    gt_ref[...] = jax.nn.sigmoid(_dot(h, wg_ref[...]))


def _even_proj(x, g, tab64, weights, bsz, seq):
    t = x.shape[0]
    n_sblk = seq // TM
    tok = lambda n: pl.BlockSpec((TM, n), lambda i: (i, 0))
    heads = lambda n: pl.BlockSpec((1, n, TM, LANES), lambda i: (i // n_sblk, 0, i % n_sblk, 0))
    hshape = lambda n: jax.ShapeDtypeStruct((bsz, n, seq, LANES), BF16)
    return pl.pallas_call(
        functools.partial(_even_proj_kernel, n_sblk=n_sblk),
        grid=(t // TM,),
        in_specs=[tok(D_MODEL), _full((1, D_MODEL)), pl.BlockSpec((3, TM, LANES), lambda i: (0, i, 0))]
                 + [_resident(w.shape) for w in weights],
        out_specs=[tok(2 * A_WIDTH), heads(B_HEADS), heads(B_HEADS), tok(B_KV_WIDTH), tok(B_KV_WIDTH),
                   heads(B_KV_HEADS), heads(B_KV_HEADS), heads(B_KV_HEADS), heads(B_KV_HEADS), tok(LANES)],
        out_shape=[jax.ShapeDtypeStruct((t, 2 * A_WIDTH), F32), hshape(B_HEADS), hshape(B_HEADS),
                   jax.ShapeDtypeStruct((t, B_KV_WIDTH), F32), jax.ShapeDtypeStruct((t, B_KV_WIDTH), F32),
                   hshape(B_KV_HEADS), hshape(B_KV_HEADS), hshape(B_KV_HEADS), hshape(B_KV_HEADS),
                   jax.ShapeDtypeStruct((t, LANES), F32)],
        compiler_params=_params("parallel"),
        name="even_in_proj",
    )(x, g.reshape(1, D_MODEL), tab64, *[w.astype(BF16) for w in weights])


OD_ROPE = C_WIDTH + 2 * HEAD_DIM + 2 * D_WIDTH
OD_KD = C_WIDTH + 2 * HEAD_DIM + D_WIDTH
OD_ROPEI = 2 * LANES
OD_PLAIN = 2 * HEAD_DIM + D_WIDTH


def _odd_proj_kernel(x_ref, g_ref, t64_ref, t32_ref, wr_ref, wi_ref, wp_ref,
                     qc_ref, kc_ref, vc_ref, qi_ref, ki_ref, wt_ref, qd_ref, kd_ref, vd_ref, km_ref,
                     *, n_sblk):
    sblk = pl.program_id(0) % n_sblk
    h = _rms(x_ref[...], g_ref[...]).astype(BF16)
    lane = _iota((TM, LANES), 1)
    ones_col = jnp.where(lane == ONES_LANE, 1.0, 0.0)
    blk = _seq_pos(sblk, TM) // MOBA_BLOCK

    zr = _rope_tiles(_dot(h, wr_ref[...]), t64_ref, ROT_DIM // 2)
    for hh in range(C_HEADS):
        qc_ref[0, hh] = _head_row(zr, hh * HEAD_DIM, HEAD_DIM, 0.0).astype(BF16)
    kc_ref[0] = _head_row(zr, C_WIDTH, HEAD_DIM, 0.0).astype(BF16)
    for hh in range(D_HEADS):
        qd_ref[0, hh] = _head_row(zr, C_WIDTH + 2 * HEAD_DIM + hh * HEAD_DIM, HEAD_DIM, 0.0).astype(BF16)
        onehot = jnp.where(lane == AUX_LANE + hh * (LANES - AUX_LANE) // D_HEADS + blk, 1.0, 0.0)
        kd_ref[0, hh] = _head_row(zr, OD_KD + hh * HEAD_DIM, HEAD_DIM, onehot).astype(BF16)
    zkd = zr[:, OD_KD:]
    n_mb = TM // MOBA_BLOCK
    km_ref[0] = jnp.sum(zkd.reshape(n_mb, MOBA_BLOCK, D_WIDTH), axis=1) * (1.0 / MOBA_BLOCK)

    zi = _rope_tiles(_dot(h, wi_ref[...]), t32_ref, IDX_ROT // 2)
    for hh in range(IDX_HEADS):
        qi_ref[0, hh] = _head_row(zi, hh * IDX_DIM, IDX_DIM, 0.0).astype(BF16)
    ki_ref[0] = _head_row(zi, IDX_HEADS * IDX_DIM, IDX_DIM, 0.0).astype(BF16)

    zp = _dot(h, wp_ref[...])
    vc_ref[0] = _head_row(zp, 0, HEAD_DIM, ones_col).astype(BF16)
    wt_ref[0] = pltpu.roll(zp[:, :LANES], LANES - HEAD_DIM, axis=1).T[:IDX_HEADS]
    for hh in range(D_HEADS):
        vd_ref[0, hh] = _head_row(zp, 2 * HEAD_DIM + hh * HEAD_DIM, HEAD_DIM, ones_col).astype(BF16)


def _odd_proj(x, g, tab64, tab32, weights, bsz, seq):
    t = x.shape[0]
    n_sblk = seq // TM
    tok = lambda n: pl.BlockSpec((TM, n), lambda i: (i, 0))
    tab = pl.BlockSpec((3, TM, LANES), lambda i: (0, i, 0))
    heads = lambda n: pl.BlockSpec((1, n, TM, LANES), lambda i: (i // n_sblk, 0, i % n_sblk, 0))
    single = pl.BlockSpec((1, TM, LANES), lambda i: (i // n_sblk, i % n_sblk, 0))
    hshape = lambda n: jax.ShapeDtypeStruct((bsz, n, seq, LANES), BF16)
    sshape = jax.ShapeDtypeStruct((bsz, seq, LANES), BF16)
    return pl.pallas_call(
        functools.partial(_odd_proj_kernel, n_sblk=n_sblk),
        grid=(t // TM,),
        in_specs=[tok(D_MODEL), _full((1, D_MODEL)), tab, tab] + [_resident(w.shape) for w in weights],
        out_specs=[heads(C_HEADS), single, single, heads(IDX_HEADS), single,
                   pl.BlockSpec((1, IDX_HEADS, TM), lambda i: (i // n_sblk, 0, i % n_sblk)),
                   heads(D_HEADS), heads(D_HEADS), heads(D_HEADS),
                   pl.BlockSpec((1, TM // MOBA_BLOCK, D_WIDTH), lambda i: (i, 0, 0))],
        out_shape=[hshape(C_HEADS), sshape, sshape, hshape(IDX_HEADS), sshape,
                   jax.ShapeDtypeStruct((bsz, IDX_HEADS, seq), F32),
                   hshape(D_HEADS), hshape(D_HEADS), hshape(D_HEADS),
                   jax.ShapeDtypeStruct((t // TM, TM // MOBA_BLOCK, D_WIDTH), F32)],
        compiler_params=_params("parallel"),
        name="odd_in_proj",
    )(x, g.reshape(1, D_MODEL), tab64, tab32, *[w.astype(BF16) for w in weights])


def _rope_tables(positions, rot_dim, period):
    half = rot_dim // 2
    inv_freq = ROPE_THETA ** (-jnp.arange(0, rot_dim, 2, dtype=F32) / rot_dim)
    ang = positions.astype(F32).reshape(-1, 1) * inv_freq
    cos, sin = jnp.cos(ang), jnp.sin(ang)
    t = cos.shape[0]
    one = jnp.ones((t, period - 2 * half), F32)
    zero = jnp.zeros((t, period - 2 * half), F32)
    zh = jnp.zeros((t, half), F32)
    cs = jnp.concatenate([cos, cos, one], axis=1)
    s1 = jnp.concatenate([-sin, zh, zero], axis=1)
    s2 = jnp.concatenate([zh, sin, zero], axis=1)
    rep = LANES // period
    return jnp.stack([jnp.tile(a, (1, rep)) for a in (cs, s1, s2)])


def _outproj_kernel(x_ref, a_ref, b_ref, wa_ref, wb_ref, o_ref):
    o_ref[...] = x_ref[...] + (_dot(a_ref[...], wa_ref[...]) + _dot(b_ref[...], wb_ref[...]))


def _outproj(x, a, b, w_out):
    t = x.shape[0]
    na, nb = a.shape[1], b.shape[1]
    wa, wb = w_out[:na].astype(BF16), w_out[na:].astype(BF16)
    return pl.pallas_call(
        _outproj_kernel,
        grid=(t // TM,),
        in_specs=[pl.BlockSpec((TM, D_MODEL), lambda i: (i, 0)),
                  pl.BlockSpec((TM, na), lambda i: (i, 0)),
                  pl.BlockSpec((TM, nb), lambda i: (i, 0)),
                  _resident((na, D_MODEL)), _resident((nb, D_MODEL))],
        out_specs=pl.BlockSpec((TM, D_MODEL), lambda i: (i, 0)),
        out_shape=jax.ShapeDtypeStruct((t, D_MODEL), F32),
        compiler_params=_params("parallel"),
        name="mixer_out_proj",
    )(x, a, b, wa, wb)


def _gmlp_kernel(a_ref, n_ref, w_ref, b_ref, o_ref):
    z = _gelu(a_ref[...])
    u = z[:, :A_WIDTH]
    v = _rms(z[:, A_WIDTH:], n_ref[...]).astype(BF16)
    causal = _iota((A_CHUNK, A_CHUNK), 1) <= _iota((A_CHUNK, A_CHUNK), 0)
    lane_group = _iota((A_CHUNK, A_WIDTH), 1) // HEAD_DIM
    mixed = jnp.zeros((A_CHUNK, A_WIDTH), F32)
    for g in range(A_GROUPS):
        w = jnp.where(causal, w_ref[g], 0.0).astype(BF16)
        mixed = jnp.where(lane_group == g, _dot(w, v) + b_ref[g], mixed)
    o_ref[...] = (u * mixed).astype(BF16)


def _gmlp(a_in, sgu_norm, sgu_w, sgu_b):
    t = a_in.shape[0]
    return pl.pallas_call(
        _gmlp_kernel,
        grid=(t // A_CHUNK,),
        in_specs=[pl.BlockSpec((A_CHUNK, 2 * A_WIDTH), lambda i: (i, 0)),
                  _full((1, A_WIDTH)), _full((A_GROUPS, A_CHUNK, A_CHUNK)),
                  _full((A_GROUPS, A_CHUNK, 1))],
        out_specs=pl.BlockSpec((A_CHUNK, A_WIDTH), lambda i: (i, 0)),
        out_shape=jax.ShapeDtypeStruct((t, A_WIDTH), BF16),
        compiler_params=_params("parallel"),
        name="gmlp",
    )(a_in, sgu_norm.reshape(1, A_WIDTH), sgu_w, sgu_b.reshape(A_GROUPS, A_CHUNK, 1))


def _compress_one(t2, pos_ref, w1_ref, w2_ref):
    a = _dot((t2 + pos_ref[0:1, :]).astype(BF16), w1_ref[0])
    b = _dot((t2 + pos_ref[1:2, :]).astype(BF16), w1_ref[1])
    hid = _gelu(a + pltpu.roll(b, t2.shape[0] - 1, axis=0))
    out = _dot(hid.astype(BF16), w2_ref[...])
    row = _iota(out.shape, 0)
    return jnp.where(row < t2.shape[0] - 1, out, 0.0).astype(BF16)


def _compress_kernel(kc_ref, vc_ref, pk_ref, w1k_ref, w2k_ref, pv_ref, w1v_ref, w2v_ref, ko_ref, vo_ref):
    for g in range(B_KV_HEADS):
        ko_ref[0, g] = _compress_one(kc_ref[0, g], pk_ref, w1k_ref, w2k_ref)
        vo_ref[0, g] = _compress_one(vc_ref[0, g], pv_ref, w1v_ref, w2v_ref)


def _compress(kc2, vc2, pos_k, w1_k, w2_k, pos_v, w1_v, w2_v):
    bsz, _, nrow, wide = kc2.shape

    def prep(pos, w1, w2):
        w2p = jnp.pad(w2, ((0, 0), (0, LANES - HEAD_DIM)))
        return (pos.reshape(2, wide), w1.reshape(2, wide, CMP_HIDDEN).astype(BF16), w2p.astype(BF16))

    pk, w1k, w2k = prep(pos_k, w1_k, w2_k)
    pv, w1v, w2v = prep(pos_v, w1_v, w2_v)
    blk_in = pl.BlockSpec((1, B_KV_HEADS, nrow, wide), lambda b: (b, 0, 0, 0))
    blk_out = pl.BlockSpec((1, B_KV_HEADS, nrow, LANES), lambda b: (b, 0, 0, 0))
    wspecs = [_full((2, wide)), _full((2, wide, CMP_HIDDEN)), _full((CMP_HIDDEN, LANES))]
    return pl.pallas_call(
        _compress_kernel,
        grid=(bsz,),
        in_specs=[blk_in, blk_in] + wspecs + wspecs,
        out_specs=[blk_out, blk_out],
        out_shape=[jax.ShapeDtypeStruct((bsz, B_KV_HEADS, nrow, LANES), BF16)] * 2,
        compiler_params=_params("parallel"),
        name="nsa_compress",
    )(kc2, vc2, pk, w1k, w2k, pv, w1v, w2v)


def _rank_lt(vals, axis, n, k):
    j = _iota(vals.shape, axis)
    rank = jnp.zeros(vals.shape, F32)
    for jp in range(n):
        row = lax.slice_in_dim(vals, jp, jp + 1, axis=axis)
        beats = jnp.where(row > vals, 1.0, jnp.where(row == vals, jnp.where(j > jp, 1.0, 0.0), 0.0))
        rank = rank + beats
    return jnp.where(rank < k, 1.0, 0.0)


def _penalty_rows(pen_t):
    n = pen_t.shape[0]
    parts = [jnp.zeros((AUX_LANE, LANES), F32), pen_t]
    if LANES - AUX_LANE - n:
        parts.append(jnp.zeros((LANES - AUX_LANE - n, LANES), F32))
    return jnp.concatenate(parts, axis=0).T


def _normalize(acc):
    den = lax.slice_in_dim(acc, ONES_LANE, ONES_LANE + 1, axis=acc.ndim - 1)
    return acc / den


def _pair_store(o_ref, heads_out):
    lane = _iota((Q_BLOCK, LANES), 1)
    for p in range(len(heads_out) // 2):
        both = jnp.where(lane < HEAD_DIM, heads_out[2 * p], pltpu.roll(heads_out[2 * p + 1], HEAD_DIM, axis=1))
        o_ref[0, :, p * LANES:(p + 1) * LANES] = both.astype(BF16)


def _nsa_kernel(qr_ref, qn_ref, gt_ref, kc_ref, vc_ref, ks_ref, vs_ref, kw_ref, vw_ref, ovt_ref,
                o_ref, s_scr, w_scr, m_scr, acc_scr):
    i = pl.program_id(1)
    q0 = i * Q_BLOCK
    rows = B_GROUP * Q_BLOCK
    n_slc = ovt_ref.shape[0]
    n_g = B_KV_HEADS
    gates = gt_ref[0]
    t_row = q0 + (_iota((rows, LANES), 0) & (Q_BLOCK - 1))
    lane = _iota((rows, LANES), 1)
    t_row2 = q0 + (_iota((rows, KT), 0) & (Q_BLOCK - 1))
    lane2 = _iota((rows, KT), 1)

    q_ext, o_cmp = [], []
    for g in range(n_g):
        hs = slice(g * B_GROUP, (g + 1) * B_GROUP)
        qn = qn_ref[0, hs].reshape(rows, LANES) * ATT_SCALE
        s_c = _dot_nt(qn, kc_ref[0, g])
        m_c = (lane * CMP_STRIDE + (CMP_LEN - 1)) <= t_row
        sm = jnp.where(m_c, s_c, NEG_INF)
        e = jnp.where(m_c, jnp.exp(sm - jnp.max(sm, axis=-1, keepdims=True)), 0.0)
        den = jnp.sum(e, axis=-1, keepdims=True)
        p_cb = (e / jnp.where(den > 0.0, den, 1.0)).astype(BF16)
        o_cmp.append(_dot(p_cb, vc_ref[0, g]))

        imp = jnp.zeros((n_slc, Q_BLOCK), F32)
        for r in range(B_GROUP):
            imp = imp + _dot_nt(ovt_ref[...], p_cb[r * Q_BLOCK:(r + 1) * Q_BLOCK])
        jb = _iota((n_slc, Q_BLOCK), 0)
        tq = q0 + _iota((n_slc, Q_BLOCK), 1)
        forced = (jb == 0) | (jb == (tq >> 6))
        imp = jnp.where(jb * SLC_BLOCK <= tq, jnp.where(forced, FORCE_SCORE, imp), NEG_INF)
        pen = _penalty_rows((_rank_lt(imp, 0, n_slc, SLC_TOPN) - 1.0) * (-NEG_INF))
        qr = qr_ref[0, hs].astype(F32) * ATT_SCALE + pen[None]
        q_ext.append(qr.reshape(rows, LANES).astype(BF16))

    n_tiles = (i + 2) // 2
    last = n_tiles - 1
    for g in range(n_g):
        m_scr[g] = jnp.full((rows, LANES), NEG_INF, F32)

    def sel_scores(c, masked):
        base = pl.multiple_of(c * KT, KT)
        for g in range(n_g):
            s = _dot_nt(q_ext[g], ks_ref[0, g, pl.ds(base, KT), :])
            if masked:
                s = jnp.where(lane2 + base <= t_row2, s, NEG_INF)
            s_scr[c, g] = s
            m_scr[g] = jnp.maximum(m_scr[g], jnp.maximum(s[:, :LANES], s[:, LANES:]))

    def sel_pass1(c, _):
        sel_scores(c, False)
        return 0

    lax.fori_loop(0, last, sel_pass1, 0)
    sel_scores(last, True)

    n_wt = (WINDOW + Q_BLOCK + KT - 1) // KT
    w_mx = [jnp.full((rows, LANES), NEG_INF, F32) for _ in range(n_g)]
    for c in range(n_wt):
        start = q0 + Q_BLOCK - (n_wt - c) * KT
        base = pl.multiple_of(jnp.maximum(start, 0), Q_BLOCK)
        key = lane2 + base
        key_hi = jnp.minimum(t_row2, start + (KT - 1))
        for g in range(n_g):
            s = _dot_nt(q_ext[g], kw_ref[0, g, pl.ds(base, KT), :])
            s = jnp.where(key > t_row2 - WINDOW, jnp.where(key <= key_hi, s, NEG_INF), NEG_INF)
            w_scr[c, g] = s
            w_mx[g] = jnp.maximum(w_mx[g], jnp.maximum(s[:, :LANES], s[:, LANES:]))
    w_max = [jnp.max(w_mx[g], axis=-1, keepdims=True) for g in range(n_g)]

    m_sel = [jnp.max(m_scr[g], axis=-1, keepdims=True) for g in range(n_g)]
    for g in range(n_g):
        acc_scr[g] = jnp.zeros((rows, LANES), F32)

    def sel_pass2(c, _):
        base = pl.multiple_of(c * KT, KT)
        for g in range(n_g):
            p = jnp.exp(s_scr[c, g] - m_sel[g]).astype(BF16)
            acc_scr[g] += _dot(p, vs_ref[0, g, pl.ds(base, KT), :])
        return 0

    lax.fori_loop(0, n_tiles, sel_pass2, 0)

    outs = []
    for g in range(n_g):
        acc_w = jnp.zeros((rows, LANES), F32)
        for c in range(n_wt):
            start = q0 + Q_BLOCK - (n_wt - c) * KT
            base = pl.multiple_of(jnp.maximum(start, 0), Q_BLOCK)
            p = jnp.exp(w_scr[c, g] - w_max[g]).astype(BF16)
            acc_w = acc_w + _dot(p, vw_ref[0, g, pl.ds(base, KT), :])
        o_w = _normalize(acc_w)
        o_s = _normalize(acc_scr[g])
        for r in range(B_GROUP):
            h = g * B_GROUP + r
            rs = slice(r * Q_BLOCK, (r + 1) * Q_BLOCK)
            c0 = h * N_BRANCH
            outs.append(gates[:, c0:c0 + 1] * o_cmp[g][rs] + gates[:, c0 + 1:c0 + 2] * o_s[rs]
                        + gates[:, c0 + 2:c0 + 3] * o_w[rs])
    _pair_store(o_ref, outs)


def _overlap_t(n_rows, n_slc):
    n = np.arange(n_rows)
    c0 = n * CMP_STRIDE
    s0 = np.arange(n_slc) * SLC_BLOCK
    m = (c0[None, :] < s0[:, None] + SLC_BLOCK) & (c0[None, :] + CMP_LEN > s0[:, None])
    m = m & (n[None, :] < n_rows - 1)
    return jnp.asarray(m, dtype=BF16)


def _nsa(q_rot, q_raw, gates, k_cmp, v_cmp, k_slc, v_slc, k_win, v_win):
    bsz, _, seq, _ = q_rot.shape
    n_qb = seq // Q_BLOCK
    n_slc = seq // SLC_BLOCK
    n_cmp_rows = k_cmp.shape[2]
    ovt = _overlap_t(n_cmp_rows, n_slc)
    q_spec = pl.BlockSpec((1, B_HEADS, Q_BLOCK, LANES), lambda b, i: (b, 0, i, 0))
    cmp_spec = pl.BlockSpec((1, B_KV_HEADS, n_cmp_rows, LANES), lambda b, i: (b, 0, 0, 0))
    kv_spec = pl.BlockSpec((1, B_KV_HEADS, seq, LANES), lambda b, i: (b, 0, 0, 0))
    rows = B_GROUP * Q_BLOCK
    n_wt = (WINDOW + Q_BLOCK + KT - 1) // KT
    return pl.pallas_call(
        _nsa_kernel,
        grid=(bsz, n_qb),
        in_specs=[q_spec, q_spec, pl.BlockSpec((1, Q_BLOCK, LANES), lambda b, i: (b, i, 0)),
                  cmp_spec, cmp_spec, kv_spec, kv_spec, kv_spec, kv_spec, _full(ovt.shape)],
        out_specs=pl.BlockSpec((1, Q_BLOCK, B_WIDTH), lambda b, i: (b, i, 0)),
        out_shape=jax.ShapeDtypeStruct((bsz, seq, B_WIDTH), BF16),
        scratch_shapes=[pltpu.VMEM((seq // KT, B_KV_HEADS, rows, KT), F32),
                        pltpu.VMEM((n_wt, B_KV_HEADS, rows, KT), F32),
                        pltpu.VMEM((B_KV_HEADS, rows, LANES), F32),
                        pltpu.VMEM((B_KV_HEADS, rows, LANES), F32)],
        compiler_params=_params("parallel", "arbitrary"),
        name="nsa_attention",
    )(q_rot, q_raw, gates, k_cmp, v_cmp, k_slc, v_slc, k_win, v_win, ovt)


def _dsa_kernel(q_ref, k_ref, v_ref, qi_ref, ki_ref, wi_ref, o_ref,
                sc_scr, bias_scr, s_scr, mx_scr, acc_scr, *, top_k):
    i = pl.program_id(1)
    q0 = i * Q_BLOCK
    n_tiles = (i + 2) // 2
    tile = (KT, Q_BLOCK)
    key_sub = _iota(tile, 0)
    tq = q0 + _iota(tile, 1)
    w_idx = wi_ref[0]
    qi = qi_ref[0].reshape(IDX_HEADS * Q_BLOCK, LANES)
    n_sub = KT // 8

    def fold(x, op):
        return op(op(x.reshape(n_sub // 4, 4, 8, Q_BLOCK), axis=0), axis=0)

    def score_body(c, carry):
        mn, mx = carry
        base = pl.multiple_of(c * KT, KT)
        lg = _dot_nt(ki_ref[0, pl.ds(base, KT), :], qi)
        sc = jnp.zeros(tile, F32)
        for h in range(IDX_HEADS):
            lgh = lg[:, h * Q_BLOCK:(h + 1) * Q_BLOCK] * (IDX_DIM ** -0.5)
            sc = sc + w_idx[h:h + 1, :] * jnp.maximum(lgh, 0.0)
        sc = sc * (IDX_HEADS ** -0.5)
        valid = key_sub + base <= tq
        sc_scr[c] = jnp.where(valid, sc, NEG_INF)
        return (jnp.minimum(mn, fold(jnp.where(valid, sc, -NEG_INF), jnp.min)),
                jnp.maximum(mx, fold(jnp.where(valid, sc, NEG_INF), jnp.max)))

    mn, mx = lax.fori_loop(0, n_tiles, score_body,
                           (jnp.full((8, Q_BLOCK), -NEG_INF, F32), jnp.full((8, Q_BLOCK), NEG_INF, F32)))
    lo = jnp.min(mn, axis=0, keepdims=True)
    mx = jnp.max(mx, axis=0, keepdims=True)
    hi = mx + jnp.abs(mx) * 1e-3 + 1.0
    kf = jnp.float32(top_k)

    def bisect_body(_, carry):
        lo_, hi_ = carry
        mid = 0.5 * (lo_ + hi_)
        cnt = lax.fori_loop(0, n_tiles,
                            lambda c, a: a + fold(jnp.where(sc_scr[c] >= mid, 1.0, 0.0), jnp.sum),
                            jnp.zeros((8, Q_BLOCK), F32))
        ge = jnp.sum(cnt, axis=0, keepdims=True) >= kf
        return jnp.where(ge, mid, lo_), jnp.where(ge, hi_, mid)

    lo, hi = lax.fori_loop(0, BISECT_ITERS, bisect_body, (lo, hi))

    def thr_body(c, carry):
        vm, cg = carry
        sc = sc_scr[c]
        below = sc < hi
        return (jnp.maximum(vm, fold(jnp.where(below, sc, NEG_INF), jnp.max)),
                cg + fold(jnp.where(below, 0.0, 1.0), jnp.sum))

    vm, cg = lax.fori_loop(0, n_tiles, thr_body,
                           (jnp.full((8, Q_BLOCK), NEG_INF, F32), jnp.zeros((8, Q_BLOCK), F32)))
    thr = jnp.max(vm, axis=0, keepdims=True)
    need = kf - jnp.sum(cg, axis=0, keepdims=True)
    take_all = (tq + 1) <= top_k
    tri = jnp.where(_iota((KT, KT), 1) <= _iota((KT, KT), 0), 1.0, 0.0).astype(BF16)

    def mask_body(c, seen):
        base = c * KT
        sc = sc_scr[c]
        eq = sc == thr
        prefix = _dot(tri, jnp.where(eq, 1.0, 0.0).astype(BF16)) + seen
        chosen = jnp.where(sc > thr, 1.0, jnp.where(eq, jnp.where(prefix <= need, 1.0, 0.0), 0.0))
        valid = jnp.where(key_sub + base <= tq, 1.0, 0.0)
        bias_t = (jnp.where(take_all, valid, chosen) - 1.0) * (-NEG_INF)
        for half in range(KT // LANES):
            bias_scr[c, :, half * LANES:(half + 1) * LANES] = bias_t[half * LANES:(half + 1) * LANES].T
        return prefix[KT - 1:KT, :]

    lax.fori_loop(0, n_tiles, mask_body, jnp.zeros((1, Q_BLOCK), F32))

    rows = C_HEADS * Q_BLOCK
    q = q_ref[0].reshape(rows, LANES) * ATT_SCALE
    mx_scr[...] = jnp.full((rows, LANES), NEG_INF, F32)

    def att_pass1(c, _):
        base = pl.multiple_of(c * KT, KT)
        s = _dot_nt(q, k_ref[0, pl.ds(base, KT), :])
        s = (s.reshape(C_HEADS, Q_BLOCK, KT) + bias_scr[c][None]).reshape(rows, KT)
        s_scr[c] = s
        mx_scr[...] = jnp.maximum(mx_scr[...], jnp.maximum(s[:, :LANES], s[:, LANES:]))
        return 0

    lax.fori_loop(0, n_tiles, att_pass1, 0)
    m = jnp.max(mx_scr[...], axis=-1, keepdims=True)
    acc_scr[...] = jnp.zeros((rows, LANES), F32)

    def att_pass2(c, _):
        base = pl.multiple_of(c * KT, KT)
        p = jnp.exp(s_scr[c] - m).astype(BF16)
        acc_scr[...] += _dot(p, v_ref[0, pl.ds(base, KT), :])
        return 0

    lax.fori_loop(0, n_tiles, att_pass2, 0)
    o = _normalize(acc_scr[...])
    _pair_store(o_ref, [o[h * Q_BLOCK:(h + 1) * Q_BLOCK] for h in range(C_HEADS)])


def _dsa(q, k, v, q_idx, k_idx, w_idx_t):
    bsz, _, seq, _ = q.shape
    n_qb = seq // Q_BLOCK
    top_k = min(DSA_TOPK, seq // 4)
    rows = C_HEADS * Q_BLOCK
    single = pl.BlockSpec((1, seq, LANES), lambda b, i: (b, 0, 0))
    return pl.pallas_call(
        functools.partial(_dsa_kernel, top_k=top_k),
        grid=(bsz, n_qb),
        in_specs=[pl.BlockSpec((1, C_HEADS, Q_BLOCK, LANES), lambda b, i: (b, 0, i, 0)),
                  single, single,
                  pl.BlockSpec((1, IDX_HEADS, Q_BLOCK, LANES), lambda b, i: (b, 0, i, 0)),
                  single,
                  pl.BlockSpec((1, IDX_HEADS, Q_BLOCK), lambda b, i: (b, 0, i))],
        out_specs=pl.BlockSpec((1, Q_BLOCK, C_WIDTH), lambda b, i: (b, i, 0)),
        out_shape=jax.ShapeDtypeStruct((bsz, seq, C_WIDTH), BF16),
        scratch_shapes=[pltpu.VMEM((seq // KT, KT, Q_BLOCK), F32),
                        pltpu.VMEM((seq // KT, Q_BLOCK, KT), F32),
                        pltpu.VMEM((seq // KT, rows, KT), F32),
                        pltpu.VMEM((rows, LANES), F32),
                        pltpu.VMEM((rows, LANES), F32)],
        compiler_params=_params("parallel", "arbitrary"),
        name="dsa_attention",
    )(q, k, v, q_idx, k_idx, w_idx_t)


def _moba_kernel(q_ref, k_ref, v_ref, km_ref, o_ref, s_scr, m_scr, acc_scr, *, top_k):
    i = pl.program_id(1)
    q0 = i * Q_BLOCK
    own = q0 // MOBA_BLOCK
    n_blk = km_ref.shape[2]
    q = q_ref[0]
    gate = _bdot_nt(km_ref[0], q)
    jb = _iota(gate.shape, 1)
    past = jb < own
    gate = jnp.where(past, gate, NEG_INF)
    sel_t = jnp.where(past, _rank_lt(gate, 1, n_blk, top_k), 0.0)
    pen_t = jnp.where(jb == own, 0.0, (sel_t - 1.0) * (-NEG_INF))
    pen = _penalty_rows(pen_t.reshape(D_HEADS * n_blk, Q_BLOCK))
    q_ext = (q.astype(F32) * ATT_SCALE + pen[None]).astype(BF16)

    m_scr[...] = jnp.full(m_scr.shape, NEG_INF, F32)
    tq = q0 + _iota((D_HEADS, Q_BLOCK, MOBA_BLOCK), 1)
    key = _iota((D_HEADS, Q_BLOCK, MOBA_BLOCK), 2)

    def scores(j, masked):
        base = pl.multiple_of(j * MOBA_BLOCK, MOBA_BLOCK)
        s = _bdot_nt(q_ext, k_ref[0, :, pl.ds(base, MOBA_BLOCK), :])
        if masked:
            s = jnp.where(key + base <= tq, s, NEG_INF)
        s_scr[j] = s
        m_scr[...] = jnp.maximum(m_scr[...], jnp.maximum(s[..., :LANES], s[..., LANES:]))

    def pass1(j, _):
        scores(j, False)
        return 0

    lax.fori_loop(0, own, pass1, 0)
    scores(own, True)
    m = jnp.max(m_scr[...], axis=-1, keepdims=True)
    acc_scr[...] = jnp.zeros(acc_scr.shape, F32)

    def pass2(j, _):
        base = pl.multiple_of(j * MOBA_BLOCK, MOBA_BLOCK)
        p = jnp.exp(s_scr[j] - m).astype(BF16)
        acc_scr[...] += _bdot(p, v_ref[0, :, pl.ds(base, MOBA_BLOCK), :])
        return 0

    lax.fori_loop(0, own + 1, pass2, 0)
    o = _normalize(acc_scr[...])
    _pair_store(o_ref, [o[h] for h in range(D_HEADS)])


def _moba(q, k, v, k_mean):
    bsz, _, seq, _ = q.shape
    n_qb = seq // Q_BLOCK
    n_blk = seq // MOBA_BLOCK
    top_k = min(MOBA_TOPK, n_blk - 1)
    q_spec = pl.BlockSpec((1, D_HEADS, Q_BLOCK, LANES), lambda b, i: (b, 0, i, 0))
    kv_spec = pl.BlockSpec((1, D_HEADS, seq, LANES), lambda b, i: (b, 0, 0, 0))
    return pl.pallas_call(
        functools.partial(_moba_kernel, top_k=top_k),
        grid=(bsz, n_qb),
        in_specs=[q_spec, kv_spec, kv_spec,
                  pl.BlockSpec((1, D_HEADS, n_blk, LANES), lambda b, i: (b, 0, 0, 0))],
        out_specs=pl.BlockSpec((1, Q_BLOCK, D_WIDTH), lambda b, i: (b, i, 0)),
        out_shape=jax.ShapeDtypeStruct((bsz, seq, D_WIDTH), BF16),
        scratch_shapes=[pltpu.VMEM((n_blk, D_HEADS, Q_BLOCK, MOBA_BLOCK), F32),
                        pltpu.VMEM((D_HEADS, Q_BLOCK, LANES), F32),
                        pltpu.VMEM((D_HEADS, Q_BLOCK, LANES), F32)],
        compiler_params=_params("parallel", "arbitrary"),
        name="moba_attention",
    )(q, k, v, k_mean)


def _cols(w, start, size):
    return w[:, start:start + size]


def _even_mixer(x, bsz, seq, tab64, norm_g, w_in, sgu_norm, sgu_w, sgu_b,
                cmp_pos_k, cmp_w1_k, cmp_w2_k, cmp_pos_v, cmp_w1_v, cmp_w2_v, w_out):
    o_q = 2 * A_WIDTH
    o_kc = o_q + B_WIDTH
    o_vc, o_ksl, o_vsl, o_kw, o_vw = (o_kc + B_KV_WIDTH * n for n in range(1, 6))
    o_gl = o_kc + 6 * B_KV_WIDTH
    n_gate = B_HEADS * N_BRANCH
    w_a = _cols(w_in, 0, 2 * A_WIDTH)
    w_q = _cols(w_in, o_q, B_WIDTH)
    w_plain = jnp.concatenate([_cols(w_in, o, B_KV_WIDTH) for o in (o_kc, o_vc, o_vsl, o_vw)], axis=1)
    w_rope = jnp.concatenate([_cols(w_in, o, B_KV_WIDTH) for o in (o_ksl, o_kw)], axis=1)
    w_gate = jnp.pad(_cols(w_in, o_gl, n_gate), ((0, 0), (0, LANES - n_gate)))
    a_in, q_raw, q_rot, kc, vc, v_slc, v_win, k_slc, k_win, gates = _even_proj(
        x, norm_g, tab64, [w_a, w_q, w_plain, w_rope, w_gate], bsz, seq)

    a_out = _gmlp(a_in, sgu_norm, sgu_w, sgu_b)

    def cmp_rows(t):
        t = t.reshape(bsz, seq // CMP_STRIDE, CMP_STRIDE, B_KV_HEADS, HEAD_DIM)
        return t.transpose(0, 3, 1, 2, 4).reshape(bsz, B_KV_HEADS, seq // CMP_STRIDE, CMP_STRIDE * HEAD_DIM)

    k_cmp, v_cmp = _compress(cmp_rows(kc), cmp_rows(vc), cmp_pos_k, cmp_w1_k, cmp_w2_k,
                             cmp_pos_v, cmp_w1_v, cmp_w2_v)
    b_out = _nsa(q_rot, q_raw, gates.reshape(bsz, seq, LANES), k_cmp, v_cmp, k_slc, v_slc, k_win, v_win)
    return _outproj(x, a_out, b_out.reshape(bsz * seq, B_WIDTH), w_out)


def _odd_mixer(x, bsz, seq, tab64, tab32, norm_g, w_in, w_out):
    sizes = (C_WIDTH, HEAD_DIM, HEAD_DIM, IDX_HEADS * IDX_DIM, IDX_DIM, IDX_HEADS, D_WIDTH, D_WIDTH, D_WIDTH)
    offs = np.concatenate([[0], np.cumsum(sizes)])
    w_qc, w_kc, w_vc, w_qi, w_ki, w_wi, w_qd, w_kd, w_vd = (
        _cols(w_in, int(o), int(s)) for o, s in zip(offs[:-1], sizes))
    zpad = lambda n: jnp.zeros((D_MODEL, n), w_in.dtype)
    w_rope = jnp.concatenate([w_qc, w_kc, zpad(HEAD_DIM), w_qd, w_kd], axis=1)
    w_ropei = jnp.concatenate([w_qi, w_ki, zpad(OD_ROPEI - IDX_HEADS * IDX_DIM - IDX_DIM)], axis=1)
    w_plain = jnp.concatenate([w_vc, w_wi, zpad(HEAD_DIM - IDX_HEADS), w_vd], axis=1)
    qc, kc, vc, qi, ki, wi_t, qd, kd, vd, k_mean = _odd_proj(
        x, norm_g, tab64, tab32, [w_rope, w_ropei, w_plain], bsz, seq)

    c_out = _dsa(qc, kc, vc, qi, ki, wi_t)
    n_blk = seq // MOBA_BLOCK
    km = k_mean.reshape(bsz, n_blk, D_HEADS, HEAD_DIM).transpose(0, 2, 1, 3)
    km = jnp.pad(km, ((0, 0), (0, 0), (0, 0), (0, LANES - HEAD_DIM))).astype(BF16)
    d_out = _moba(qd, kd, vd, km)
    return _outproj(x, c_out.reshape(bsz * seq, C_WIDTH), d_out.reshape(bsz * seq, D_WIDTH), w_out)


def kernel(x, positions, ffn1_norm, ffn1_w_gate, ffn1_w_up, ffn1_w_down, mix_norm, ffn2_norm, ffn2_w_gate, ffn2_w_up, ffn2_w_down, ev_w_in, ev_sgu_norm, ev_sgu_w, ev_sgu_b, ev_cmp_pos_k, ev_cmp_w1_k, ev_cmp_w2_k, ev_cmp_pos_v, ev_cmp_w1_v, ev_cmp_w2_v, ev_w_out, od_w_in, od_w_out, final_norm):
    bsz, seq, _ = x.shape
    depth = ffn1_norm.shape[0]
    tab64 = _rope_tables(positions, ROT_DIM, HEAD_DIM)
    tab32 = _rope_tables(positions, IDX_ROT, IDX_DIM)
    x = x.reshape(bsz * seq, D_MODEL)
    for i in range(depth):
        x = _ffn(x, ffn1_norm[i], ffn1_w_gate[i], ffn1_w_up[i], ffn1_w_down[i])
        if i % 2 == 0:
            e = i // 2
            x = _even_mixer(x, bsz, seq, tab64, mix_norm[i], ev_w_in[e], ev_sgu_norm[e], ev_sgu_w[e],
                            ev_sgu_b[e], ev_cmp_pos_k[e], ev_cmp_w1_k[e], ev_cmp_w2_k[e],
                            ev_cmp_pos_v[e], ev_cmp_w1_v[e], ev_cmp_w2_v[e], ev_w_out[e])
        else:
            o = i // 2
            x = _odd_mixer(x, bsz, seq, tab64, tab32, mix_norm[i], od_w_in[o], od_w_out[o])
        x = _ffn(x, ffn2_norm[i], ffn2_w_gate[i], ffn2_w_up[i], ffn2_w_down[i],
                 final_g=final_norm if i == depth - 1 else None)
    return x.reshape(bsz, seq, D_MODEL)
```

```python
import functools

import numpy as np
import jax
import jax.numpy as jnp
from jax import lax
from jax.experimental import pallas as pl
from jax.experimental.pallas import tpu as pltpu

F32 = jnp.float32
BF16 = jnp.bfloat16

D_MODEL = 1024
HEAD_DIM = 64
ROT_DIM = HEAD_DIM // 4
ROPE_THETA = 500000.0
NORM_EPS = 1e-6
D_FF = 2816
NEG_INF = -1e30

A_GROUPS = 4
A_CHUNK = 128
A_WIDTH = A_GROUPS * HEAD_DIM
B_HEADS = 12
B_KV_HEADS = 3
B_GROUP = B_HEADS // B_KV_HEADS
B_WIDTH = B_HEADS * HEAD_DIM
B_KV_WIDTH = B_KV_HEADS * HEAD_DIM
CMP_LEN = 32
CMP_STRIDE = 16
CMP_HIDDEN = 256
SLC_BLOCK = 64
SLC_TOPN = 8
WINDOW = 512
N_BRANCH = 3
FORCE_SCORE = 1e4
C_HEADS = 8
C_WIDTH = C_HEADS * HEAD_DIM
IDX_HEADS = 4
IDX_DIM = 32
IDX_ROT = IDX_DIM // 4
DSA_TOPK = 256
D_HEADS = 8
D_WIDTH = D_HEADS * HEAD_DIM
MOBA_BLOCK = 256
MOBA_TOPK = 3

LANES = 128
ATT_SCALE = HEAD_DIM ** -0.5
VMEM_LIMIT = 56 * 1024 * 1024
TM = 512
FF_CHUNK = D_FF // 2
KT = 256
QB = 256
BISECT_ROUNDS = 8
BISECT_STEPS = 4
ONES_LANE = HEAD_DIM
AUX_LANE = HEAD_DIM


def _dot(a, b):
    return jnp.dot(a, b, preferred_element_type=F32)


def _dot_nt(a, b):
    return lax.dot_general(a, b, (((1,), (1,)), ((), ())), preferred_element_type=F32)


def _bdot_nt(a, b):
    return lax.dot_general(a, b, (((2,), (2,)), ((0,), (0,))), preferred_element_type=F32)


def _bdot(a, b):
    return lax.dot_general(a, b, (((2,), (1,)), ((0,), (0,))), preferred_element_type=F32)


def _iota(shape, dim):
    return lax.broadcasted_iota(jnp.int32, shape, dim)


def _rms(x, g):
    ms = jnp.mean(x * x, axis=-1, keepdims=True)
    return x * lax.rsqrt(ms + NORM_EPS) * g


def _gelu(x):
    return x * (0.5 * (1.0 + jnp.tanh(0.7978845608028654 * (x + 0.044715 * (x * x * x)))))


def _params(*sem):
    return pltpu.CompilerParams(dimension_semantics=sem, vmem_limit_bytes=VMEM_LIMIT)


def _full(shape):
    n = len(shape)
    return pl.BlockSpec(shape, lambda *_: (0,) * n)


def _resident(shape):
    n = len(shape)
    return pl.BlockSpec(shape, lambda *_: (0,) * n, pipeline_mode=pl.Buffered(1))


def _ffn_kernel(x_ref, g_ref, wg_ref, wu_ref, wd_ref, *rest, final):
    o_ref = rest[-1]
    x = x_ref[...]
    h = _rms(x, g_ref[...]).astype(BF16)
    acc = jnp.zeros_like(x)
    for c in range(D_FF // FF_CHUNK):
        sl = slice(c * FF_CHUNK, (c + 1) * FF_CHUNK)
        gate = _dot(h, wg_ref[:, sl])
        up = _dot(h, wu_ref[:, sl])
        act = (gate * jax.nn.sigmoid(gate) * up).astype(BF16)
        acc = acc + _dot(act, wd_ref[sl, :])
    y = x + 0.5 * acc
    if final:
        y = _rms(y, rest[0][...])
    o_ref[...] = y


def _ffn(x, g, wg, wu, wd, final_g=None):
    t = x.shape[0]
    final = final_g is not None
    ins = [x, g.reshape(1, D_MODEL), wg.astype(BF16), wu.astype(BF16), wd.astype(BF16)]
    specs = [pl.BlockSpec((TM, D_MODEL), lambda i: (i, 0)), _full((1, D_MODEL)),
             _resident((D_MODEL, D_FF)), _resident((D_MODEL, D_FF)), _resident((D_FF, D_MODEL))]
    if final:
        ins.append(final_g.reshape(1, D_MODEL))
        specs.append(_full((1, D_MODEL)))
    return pl.pallas_call(
        functools.partial(_ffn_kernel, final=final),
        grid=(t // TM,),
        in_specs=specs,
        out_specs=pl.BlockSpec((TM, D_MODEL), lambda i: (i, 0)),
        out_shape=jax.ShapeDtypeStruct((t, D_MODEL), F32),
        compiler_params=_params("parallel"),
        name="ffn_final" if final else "ffn",
    )(*ins)


def _rope_tiles(z, tab_ref, shift):
    cs, s1, s2 = tab_ref[0], tab_ref[1], tab_ref[2]
    outs = []
    for c in range(z.shape[1] // LANES):
        zt = z[:, c * LANES:(c + 1) * LANES]
        up = pltpu.roll(zt, LANES - shift, axis=1)
        dn = pltpu.roll(zt, shift, axis=1)
        outs.append(zt * cs + up * s1 + dn * s2)
    return outs[0] if len(outs) == 1 else jnp.concatenate(outs, axis=1)


def _head_row(z, col, width, aux):
    tile = z[:, (col // LANES) * LANES:(col // LANES + 1) * LANES]
    off = col % LANES
    if off:
        tile = pltpu.roll(tile, LANES - off, axis=1)
    lane = _iota(tile.shape, 1)
    return jnp.where(lane < width, tile, aux)


def _seq_pos(sblk, rows):
    return sblk * rows + _iota((rows, LANES), 0)


def _even_proj_kernel(x_ref, g_ref, t64_ref, wa_ref, wq_ref, wp_ref, wr_ref, wg_ref,
                      a_ref, qn_ref, qr_ref, kc_ref, vc_ref, vs_ref, vw_ref, ks_ref, kw_ref, gt_ref,
                      *, n_sblk):
    sblk = pl.program_id(0) % n_sblk
    h = _rms(x_ref[...], g_ref[...]).astype(BF16)
    lane = _iota((TM, LANES), 1)
    ones_col = jnp.where(lane == ONES_LANE, 1.0, 0.0)
    slc_onehot = jnp.where(lane == AUX_LANE + (_seq_pos(sblk, TM) // SLC_BLOCK), 1.0, 0.0)

    a_ref[...] = _dot(h, wa_ref[...])
    zq = _dot(h, wq_ref[...])
    zr = _rope_tiles(zq, t64_ref, ROT_DIM // 2)
    for hh in range(B_HEADS):
        qn_ref[0, hh] = _head_row(zq, hh * HEAD_DIM, HEAD_DIM, 0.0).astype(BF16)
        qr_ref[0, hh] = _head_row(zr, hh * HEAD_DIM, HEAD_DIM, 0.0).astype(BF16)
    zp = _dot(h, wp_ref[...])
    kc_ref[...] = zp[:, :B_KV_WIDTH]
    vc_ref[...] = zp[:, B_KV_WIDTH:2 * B_KV_WIDTH]
    zk = _rope_tiles(_dot(h, wr_ref[...]), t64_ref, ROT_DIM // 2)
    for g in range(B_KV_HEADS):
        vs_ref[0, g] = _head_row(zp, (2 * B_KV_HEADS + g) * HEAD_DIM, HEAD_DIM, ones_col).astype(BF16)
        vw_ref[0, g] = _head_row(zp, (3 * B_KV_HEADS + g) * HEAD_DIM, HEAD_DIM, ones_col).astype(BF16)
        ks_ref[0, g] = _head_row(zk, g * HEAD_DIM, HEAD_DIM, slc_onehot).astype(BF16)
        kw_ref[0, g] = _head_row(zk, (B_KV_HEADS + g) * HEAD_DIM, HEAD_DIM, 0.0).astype(BF16)
    gt_ref[...] = jax.nn.sigmoid(_dot(h, wg_ref[...]))


def _even_proj(x, g, tab64, weights, bsz, seq):
    t = x.shape[0]
    n_sblk = seq // TM
    tok = lambda n: pl.BlockSpec((TM, n), lambda i: (i, 0))
    heads = lambda n: pl.BlockSpec((1, n, TM, LANES), lambda i: (i // n_sblk, 0, i % n_sblk, 0))
    hshape = lambda n: jax.ShapeDtypeStruct((bsz, n, seq, LANES), BF16)
    return pl.pallas_call(
        functools.partial(_even_proj_kernel, n_sblk=n_sblk),
        grid=(t // TM,),
        in_specs=[tok(D_MODEL), _full((1, D_MODEL)), pl.BlockSpec((3, TM, LANES), lambda i: (0, i, 0))]
                 + [_resident(w.shape) for w in weights],
        out_specs=[tok(2 * A_WIDTH), heads(B_HEADS), heads(B_HEADS), tok(B_KV_WIDTH), tok(B_KV_WIDTH),
                   heads(B_KV_HEADS), heads(B_KV_HEADS), heads(B_KV_HEADS), heads(B_KV_HEADS), tok(LANES)],
        out_shape=[jax.ShapeDtypeStruct((t, 2 * A_WIDTH), F32), hshape(B_HEADS), hshape(B_HEADS),
                   jax.ShapeDtypeStruct((t, B_KV_WIDTH), F32), jax.ShapeDtypeStruct((t, B_KV_WIDTH), F32),
                   hshape(B_KV_HEADS), hshape(B_KV_HEADS), hshape(B_KV_HEADS), hshape(B_KV_HEADS),
                   jax.ShapeDtypeStruct((t, LANES), F32)],
        compiler_params=_params("parallel"),
        name="even_in_proj",
    )(x, g.reshape(1, D_MODEL), tab64, *[w.astype(BF16) for w in weights])


OD_ROPE = C_WIDTH + 2 * HEAD_DIM + 2 * D_WIDTH
OD_KD = C_WIDTH + 2 * HEAD_DIM + D_WIDTH
OD_ROPEI = 2 * LANES
OD_PLAIN = 2 * HEAD_DIM + D_WIDTH


def _odd_proj_kernel(x_ref, g_ref, t64_ref, t32_ref, wr_ref, wi_ref, wp_ref,
                     qc_ref, kc_ref, vc_ref, qi_ref, ki_ref, wt_ref, qd_ref, kd_ref, vd_ref, km_ref,
                     *, n_sblk):
    sblk = pl.program_id(0) % n_sblk
    h = _rms(x_ref[...], g_ref[...]).astype(BF16)
    lane = _iota((TM, LANES), 1)
    ones_col = jnp.where(lane == ONES_LANE, 1.0, 0.0)
    blk = _seq_pos(sblk, TM) // MOBA_BLOCK

    zr = _rope_tiles(_dot(h, wr_ref[...]), t64_ref, ROT_DIM // 2)
    for hh in range(C_HEADS):
        qc_ref[0, hh] = _head_row(zr, hh * HEAD_DIM, HEAD_DIM, 0.0).astype(BF16)
    kc_ref[0] = _head_row(zr, C_WIDTH, HEAD_DIM, 0.0).astype(BF16)
    for hh in range(D_HEADS):
        qd_ref[0, hh] = _head_row(zr, C_WIDTH + 2 * HEAD_DIM + hh * HEAD_DIM, HEAD_DIM, 0.0).astype(BF16)
        onehot = jnp.where(lane == AUX_LANE + hh * (LANES - AUX_LANE) // D_HEADS + blk, 1.0, 0.0)
        kd_ref[0, hh] = _head_row(zr, OD_KD + hh * HEAD_DIM, HEAD_DIM, onehot).astype(BF16)
    zkd = zr[:, OD_KD:]
    n_mb = TM // MOBA_BLOCK
    km_ref[0] = jnp.sum(zkd.reshape(n_mb, MOBA_BLOCK, D_WIDTH), axis=1) * (1.0 / MOBA_BLOCK)

    zi = _rope_tiles(_dot(h, wi_ref[...]), t32_ref, IDX_ROT // 2)
    for hh in range(IDX_HEADS):
        qi_ref[0, hh] = _head_row(zi, hh * IDX_DIM, IDX_DIM, 0.0).astype(BF16)
    ki_ref[0] = _head_row(zi, IDX_HEADS * IDX_DIM, IDX_DIM, 0.0).astype(BF16)

    zp = _dot(h, wp_ref[...])
    vc_ref[0] = _head_row(zp, 0, HEAD_DIM, ones_col).astype(BF16)
    wt_ref[0] = pltpu.roll(zp[:, :LANES], LANES - HEAD_DIM, axis=1).T[:IDX_HEADS]
    for hh in range(D_HEADS):
        vd_ref[0, hh] = _head_row(zp, 2 * HEAD_DIM + hh * HEAD_DIM, HEAD_DIM, ones_col).astype(BF16)


def _odd_proj(x, g, tab64, tab32, weights, bsz, seq):
    t = x.shape[0]
    n_sblk = seq // TM
    tok = lambda n: pl.BlockSpec((TM, n), lambda i: (i, 0))
    tab = pl.BlockSpec((3, TM, LANES), lambda i: (0, i, 0))
    heads = lambda n: pl.BlockSpec((1, n, TM, LANES), lambda i: (i // n_sblk, 0, i % n_sblk, 0))
    single = pl.BlockSpec((1, TM, LANES), lambda i: (i // n_sblk, i % n_sblk, 0))
    hshape = lambda n: jax.ShapeDtypeStruct((bsz, n, seq, LANES), BF16)
    sshape = jax.ShapeDtypeStruct((bsz, seq, LANES), BF16)
    return pl.pallas_call(
        functools.partial(_odd_proj_kernel, n_sblk=n_sblk),
        grid=(t // TM,),
        in_specs=[tok(D_MODEL), _full((1, D_MODEL)), tab, tab] + [_resident(w.shape) for w in weights],
        out_specs=[heads(C_HEADS), single, single, heads(IDX_HEADS), single,
                   pl.BlockSpec((1, IDX_HEADS, TM), lambda i: (i // n_sblk, 0, i % n_sblk)),
                   heads(D_HEADS), heads(D_HEADS), heads(D_HEADS),
                   pl.BlockSpec((1, TM // MOBA_BLOCK, D_WIDTH), lambda i: (i, 0, 0))],
        out_shape=[hshape(C_HEADS), sshape, sshape, hshape(IDX_HEADS), sshape,
                   jax.ShapeDtypeStruct((bsz, IDX_HEADS, seq), F32),
                   hshape(D_HEADS), hshape(D_HEADS), hshape(D_HEADS),
                   jax.ShapeDtypeStruct((t // TM, TM // MOBA_BLOCK, D_WIDTH), F32)],
        compiler_params=_params("parallel"),
        name="odd_in_proj",
    )(x, g.reshape(1, D_MODEL), tab64, tab32, *[w.astype(BF16) for w in weights])


def _rope_tables(positions, rot_dim, period):
    half = rot_dim // 2
    inv_freq = ROPE_THETA ** (-jnp.arange(0, rot_dim, 2, dtype=F32) / rot_dim)
    ang = positions.astype(F32).reshape(-1, 1) * inv_freq
    cos, sin = jnp.cos(ang), jnp.sin(ang)
    t = cos.shape[0]
    one = jnp.ones((t, period - 2 * half), F32)
    zero = jnp.zeros((t, period - 2 * half), F32)
    zh = jnp.zeros((t, half), F32)
    cs = jnp.concatenate([cos, cos, one], axis=1)
    s1 = jnp.concatenate([-sin, zh, zero], axis=1)
    s2 = jnp.concatenate([zh, sin, zero], axis=1)
    rep = LANES // period
    return jnp.stack([jnp.tile(a, (1, rep)) for a in (cs, s1, s2)])


def _outproj_kernel(x_ref, a_ref, b_ref, wa_ref, wb_ref, o_ref):
    o_ref[...] = x_ref[...] + (_dot(a_ref[...], wa_ref[...]) + _dot(b_ref[...], wb_ref[...]))


def _outproj(x, a, b, w_out):
    t = x.shape[0]
    na, nb = a.shape[1], b.shape[1]
    wa, wb = w_out[:na].astype(BF16), w_out[na:].astype(BF16)
    return pl.pallas_call(
        _outproj_kernel,
        grid=(t // TM,),
        in_specs=[pl.BlockSpec((TM, D_MODEL), lambda i: (i, 0)),
                  pl.BlockSpec((TM, na), lambda i: (i, 0)),
                  pl.BlockSpec((TM, nb), lambda i: (i, 0)),
                  _resident((na, D_MODEL)), _resident((nb, D_MODEL))],
        out_specs=pl.BlockSpec((TM, D_MODEL), lambda i: (i, 0)),
        out_shape=jax.ShapeDtypeStruct((t, D_MODEL), F32),
        compiler_params=_params("parallel"),
        name="mixer_out_proj",
    )(x, a, b, wa, wb)


def _gmlp_kernel(a_ref, n_ref, w_ref, b_ref, o_ref):
    causal = _iota((A_CHUNK, A_CHUNK), 1) <= _iota((A_CHUNK, A_CHUNK), 0)
    lane_group = _iota((A_CHUNK, A_WIDTH), 1) // HEAD_DIM
    ws = [jnp.where(causal, w_ref[g], 0.0).astype(BF16) for g in range(A_GROUPS)]
    for c in range(TM // A_CHUNK):
        rs = slice(c * A_CHUNK, (c + 1) * A_CHUNK)
        z = _gelu(a_ref[rs, :])
        u = z[:, :A_WIDTH]
        v = _rms(z[:, A_WIDTH:], n_ref[...]).astype(BF16)
        mixed = jnp.zeros((A_CHUNK, A_WIDTH), F32)
        for g in range(A_GROUPS):
            mixed = jnp.where(lane_group == g, _dot(ws[g], v) + b_ref[g], mixed)
        o_ref[rs, :] = (u * mixed).astype(BF16)


def _gmlp(a_in, sgu_norm, sgu_w, sgu_b):
    t = a_in.shape[0]
    return pl.pallas_call(
        _gmlp_kernel,
        grid=(t // TM,),
        in_specs=[pl.BlockSpec((TM, 2 * A_WIDTH), lambda i: (i, 0)),
                  _full((1, A_WIDTH)), _full((A_GROUPS, A_CHUNK, A_CHUNK)),
                  _full((A_GROUPS, A_CHUNK, 1))],
        out_specs=pl.BlockSpec((TM, A_WIDTH), lambda i: (i, 0)),
        out_shape=jax.ShapeDtypeStruct((t, A_WIDTH), BF16),
        compiler_params=_params("parallel"),
        name="gmlp",
    )(a_in, sgu_norm.reshape(1, A_WIDTH), sgu_w, sgu_b.reshape(A_GROUPS, A_CHUNK, 1))


def _compress_one(t2, pos_ref, w1_ref, w2_ref):
    a = _dot((t2 + pos_ref[0:1, :]).astype(BF16), w1_ref[0])
    b = _dot((t2 + pos_ref[1:2, :]).astype(BF16), w1_ref[1])
    hid = _gelu(a + pltpu.roll(b, t2.shape[0] - 1, axis=0))
    out = _dot(hid.astype(BF16), w2_ref[...])
    row = _iota(out.shape, 0)
    return jnp.where(row < t2.shape[0] - 1, out, 0.0).astype(BF16)


def _compress_kernel(kc_ref, vc_ref, pk_ref, w1k_ref, w2k_ref, pv_ref, w1v_ref, w2v_ref, ko_ref, vo_ref):
    for g in range(B_KV_HEADS):
        ko_ref[0, g] = _compress_one(kc_ref[0, g], pk_ref, w1k_ref, w2k_ref)
        vo_ref[0, g] = _compress_one(vc_ref[0, g], pv_ref, w1v_ref, w2v_ref)


def _compress(kc2, vc2, pos_k, w1_k, w2_k, pos_v, w1_v, w2_v):
    bsz, _, nrow, wide = kc2.shape

    def prep(pos, w1, w2):
        w2p = jnp.pad(w2, ((0, 0), (0, LANES - HEAD_DIM)))
        return (pos.reshape(2, wide), w1.reshape(2, wide, CMP_HIDDEN).astype(BF16), w2p.astype(BF16))

    pk, w1k, w2k = prep(pos_k, w1_k, w2_k)
    pv, w1v, w2v = prep(pos_v, w1_v, w2_v)
    blk_in = pl.BlockSpec((1, B_KV_HEADS, nrow, wide), lambda b: (b, 0, 0, 0))
    blk_out = pl.BlockSpec((1, B_KV_HEADS, nrow, LANES), lambda b: (b, 0, 0, 0))
    wspecs = [_full((2, wide)), _full((2, wide, CMP_HIDDEN)), _full((CMP_HIDDEN, LANES))]
    return pl.pallas_call(
        _compress_kernel,
        grid=(bsz,),
        in_specs=[blk_in, blk_in] + wspecs + wspecs,
        out_specs=[blk_out, blk_out],
        out_shape=[jax.ShapeDtypeStruct((bsz, B_KV_HEADS, nrow, LANES), BF16)] * 2,
        compiler_params=_params("parallel"),
        name="nsa_compress",
    )(kc2, vc2, pk, w1k, w2k, pv, w1v, w2v)


def _rank_lt(vals, axis, n, k):
    j = _iota(vals.shape, axis)
    rank = jnp.zeros(vals.shape, F32)
    for jp in range(n):
        row = lax.slice_in_dim(vals, jp, jp + 1, axis=axis)
        beats = jnp.where(row > vals, 1.0, jnp.where(row == vals, jnp.where(j > jp, 1.0, 0.0), 0.0))
        rank = rank + beats
    return jnp.where(rank < k, 1.0, 0.0)


def _penalty_rows(pen_t):
    n, nq = pen_t.shape
    parts = [jnp.zeros((AUX_LANE, nq), F32), pen_t]
    if LANES - AUX_LANE - n:
        parts.append(jnp.zeros((LANES - AUX_LANE - n, nq), F32))
    full = jnp.concatenate(parts, axis=0)
    halves = [full[:, c * LANES:(c + 1) * LANES].T for c in range(nq // LANES)]
    return halves[0] if len(halves) == 1 else jnp.concatenate(halves, axis=0)


def _normalize(acc):
    den = lax.slice_in_dim(acc, ONES_LANE, ONES_LANE + 1, axis=acc.ndim - 1)
    return acc / den


def _pair_store(o_ref, heads_out):
    lane = _iota((QB, LANES), 1)
    for p in range(len(heads_out) // 2):
        both = jnp.where(lane < HEAD_DIM, heads_out[2 * p], pltpu.roll(heads_out[2 * p + 1], HEAD_DIM, axis=1))
        o_ref[0, :, p * LANES:(p + 1) * LANES] = both.astype(BF16)


def _nsa_kernel(qr_ref, qn_ref, gt_ref, kc_ref, vc_ref, ks_ref, vs_ref, kw_ref, vw_ref, ovt_ref,
                o_ref, s_scr, q_scr, cmp_scr, m_scr, acc_scr):
    i = pl.program_id(1)
    q0 = i * QB
    rows = B_GROUP * QB
    n_slc = ovt_ref.shape[0]
    n_g = B_KV_HEADS
    gates = gt_ref[0]
    t_row = q0 + (_iota((rows, LANES), 0) & (QB - 1))
    lane = _iota((rows, LANES), 1)
    t_row2 = q0 + (_iota((rows, KT), 0) & (QB - 1))
    lane2 = _iota((rows, KT), 1)

    for g in range(n_g):
        hs = slice(g * B_GROUP, (g + 1) * B_GROUP)
        qn = qn_ref[0, hs].reshape(rows, LANES) * ATT_SCALE
        s_c = _dot_nt(qn, kc_ref[0, g])
        m_c = (lane * CMP_STRIDE + (CMP_LEN - 1)) <= t_row
        sm = jnp.where(m_c, s_c, NEG_INF)
        e = jnp.where(m_c, jnp.exp(sm - jnp.max(sm, axis=-1, keepdims=True)), 0.0)
        den = jnp.sum(e, axis=-1, keepdims=True)
        p_cb = (e / jnp.where(den > 0.0, den, 1.0)).astype(BF16)
        cmp_scr[g] = _dot(p_cb, vc_ref[0, g])

        imp = jnp.zeros((n_slc, QB), F32)
        for r in range(B_GROUP):
            imp = imp + _dot_nt(ovt_ref[...], p_cb[r * QB:(r + 1) * QB])
        jb = _iota((n_slc, QB), 0)
        tq = q0 + _iota((n_slc, QB), 1)
        forced = (jb == 0) | (jb == (tq >> 6))
        imp = jnp.where(jb * SLC_BLOCK <= tq, jnp.where(forced, FORCE_SCORE, imp), NEG_INF)
        pen = _penalty_rows((_rank_lt(imp, 0, n_slc, SLC_TOPN) - 1.0) * (-NEG_INF))
        qr = qr_ref[0, hs].astype(F32) * ATT_SCALE + pen[None]
        q_scr[g] = qr.reshape(rows, LANES).astype(BF16)

    n_wt = (WINDOW + QB + KT - 1) // KT
    w_mx = [jnp.full((rows, LANES), NEG_INF, F32) for _ in range(n_g)]
    w_base = []
    for c in range(n_wt):
        start = q0 + QB - (n_wt - c) * KT
        base = pl.multiple_of(jnp.maximum(start, 0), min(QB, KT))
        w_base.append(base)
        key = lane2 + base
        key_hi = jnp.minimum(t_row2, start + (KT - 1))
        for g in range(n_g):
            s = _dot_nt(q_scr[g], kw_ref[0, g, pl.ds(base, KT), :])
            s = jnp.where(key > t_row2 - WINDOW, jnp.where(key <= key_hi, s, NEG_INF), NEG_INF)
            s_scr[c, g] = s
            w_mx[g] = jnp.maximum(w_mx[g], jnp.maximum(s[:, :LANES], s[:, LANES:]))
    for g in range(n_g):
        w_max = jnp.max(w_mx[g], axis=-1, keepdims=True)
        acc_w = jnp.zeros((rows, LANES), F32)
        for c in range(n_wt):
            p = jnp.exp(s_scr[c, g] - w_max).astype(BF16)
            acc_w = acc_w + _dot(p, vw_ref[0, g, pl.ds(w_base[c], KT), :])
        acc_scr[g] = _normalize(acc_w)

    n_tiles = (q0 + QB + KT - 1) // KT
    last = n_tiles - 1
    for g in range(n_g):
        m_scr[g] = jnp.full((rows, LANES), NEG_INF, F32)

    def sel_scores(c, masked):
        base = pl.multiple_of(c * KT, KT)
        for g in range(n_g):
            s = _dot_nt(q_scr[g], ks_ref[0, g, pl.ds(base, KT), :])
            if masked:
                s = jnp.where(lane2 + base <= t_row2, s, NEG_INF)
            s_scr[c, g] = s
            m_scr[g] = jnp.maximum(m_scr[g], jnp.maximum(s[:, :LANES], s[:, LANES:]))

    def sel_pass1(c, _):
        sel_scores(c, False)
        return 0

    lax.fori_loop(0, last, sel_pass1, 0)
    sel_scores(last, True)

    m_sel = [jnp.max(m_scr[g], axis=-1, keepdims=True) for g in range(n_g)]
    for g in range(n_g):
        m_scr[g] = jnp.zeros((rows, LANES), F32)

    def sel_pass2(c, _):
        base = pl.multiple_of(c * KT, KT)
        for g in range(n_g):
            p = jnp.exp(s_scr[c, g] - m_sel[g]).astype(BF16)
            m_scr[g] += _dot(p, vs_ref[0, g, pl.ds(base, KT), :])
        return 0

    lax.fori_loop(0, n_tiles, sel_pass2, 0)

    outs = []
    for g in range(n_g):
        for r in range(B_GROUP):
            h = g * B_GROUP + r
            rs = slice(r * QB, (r + 1) * QB)
            c0 = h * N_BRANCH
            outs.append(gates[:, c0:c0 + 1] * cmp_scr[g, rs, :] + gates[:, c0 + 1:c0 + 2] * _normalize(m_scr[g, rs, :])
                        + gates[:, c0 + 2:c0 + 3] * acc_scr[g, rs, :])
    _pair_store(o_ref, outs)


def _overlap_t(n_rows, n_slc):
    n = np.arange(n_rows)
    c0 = n * CMP_STRIDE
    s0 = np.arange(n_slc) * SLC_BLOCK
    m = (c0[None, :] < s0[:, None] + SLC_BLOCK) & (c0[None, :] + CMP_LEN > s0[:, None])
    m = m & (n[None, :] < n_rows - 1)
    return jnp.asarray(m, dtype=BF16)


def _nsa(q_rot, q_raw, gates, k_cmp, v_cmp, k_slc, v_slc, k_win, v_win):
    bsz, _, seq, _ = q_rot.shape
    n_qb = seq // QB
    n_slc = seq // SLC_BLOCK
    n_cmp_rows = k_cmp.shape[2]
    ovt = _overlap_t(n_cmp_rows, n_slc)
    q_spec = pl.BlockSpec((1, B_HEADS, QB, LANES), lambda b, i: (b, 0, i, 0))
    cmp_spec = pl.BlockSpec((1, B_KV_HEADS, n_cmp_rows, LANES), lambda b, i: (b, 0, 0, 0))
    kv_spec = pl.BlockSpec((1, B_KV_HEADS, seq, LANES), lambda b, i: (b, 0, 0, 0), pipeline_mode=pl.Buffered(1))
    rows = B_GROUP * QB
    return pl.pallas_call(
        _nsa_kernel,
        grid=(bsz, n_qb),
        in_specs=[q_spec, q_spec, pl.BlockSpec((1, QB, LANES), lambda b, i: (b, i, 0)),
                  cmp_spec, cmp_spec, kv_spec, kv_spec, kv_spec, kv_spec, _full(ovt.shape)],
        out_specs=pl.BlockSpec((1, QB, B_WIDTH), lambda b, i: (b, i, 0)),
        out_shape=jax.ShapeDtypeStruct((bsz, seq, B_WIDTH), BF16),
        scratch_shapes=[pltpu.VMEM((seq // KT, B_KV_HEADS, rows, KT), F32),
                        pltpu.VMEM((B_KV_HEADS, rows, LANES), BF16),
                        pltpu.VMEM((B_KV_HEADS, rows, LANES), F32),
                        pltpu.VMEM((B_KV_HEADS, rows, LANES), F32),
                        pltpu.VMEM((B_KV_HEADS, rows, LANES), F32)],
        compiler_params=_params("parallel", "arbitrary"),
        name="nsa_attention",
    )(q_rot, q_raw, gates, k_cmp, v_cmp, k_slc, v_slc, k_win, v_win, ovt)


def _dsa_kernel(q_ref, k_ref, v_ref, qi_ref, ki_ref, wi_ref, o_ref,
                sc_scr, bias_scr, s_scr, mx_scr, acc_scr, *, top_k):
    i = pl.program_id(1)
    q0 = i * QB
    n_tiles = (q0 + QB + KT - 1) // KT
    tile = (KT, QB)
    key_sub = _iota(tile, 0)
    tq = q0 + _iota(tile, 1)
    w_idx = wi_ref[0]
    qi = qi_ref[0].reshape(IDX_HEADS * QB, LANES)
    n_sub = KT // 8

    def fold(x, op):
        return op(op(x.reshape(n_sub // 4, 4, 8, QB), axis=0), axis=0)

    def score_body(c, carry):
        mn, mx = carry
        base = pl.multiple_of(c * KT, KT)
        lg = _dot_nt(ki_ref[0, pl.ds(base, KT), :], qi)
        sc = jnp.zeros(tile, F32)
        for h in range(IDX_HEADS):
            lgh = lg[:, h * QB:(h + 1) * QB] * (IDX_DIM ** -0.5)
            sc = sc + w_idx[h:h + 1, :] * jnp.maximum(lgh, 0.0)
        sc = sc * (IDX_HEADS ** -0.5)
        valid = key_sub + base <= tq
        sc_scr[c] = jnp.where(valid, sc, NEG_INF)
        return (jnp.minimum(mn, fold(jnp.where(valid, sc, -NEG_INF), jnp.min)),
                jnp.maximum(mx, fold(jnp.where(valid, sc, NEG_INF), jnp.max)))

    mn, mx = lax.fori_loop(0, n_tiles, score_body,
                           (jnp.full((8, QB), -NEG_INF, F32), jnp.full((8, QB), NEG_INF, F32)))
    lo = jnp.min(mn, axis=0, keepdims=True)
    mx = jnp.max(mx, axis=0, keepdims=True)
    hi = mx + jnp.abs(mx) * 1e-3 + 1.0
    kf = jnp.float32(top_k)
    n_valid = (tq[0:1, :] + 1).astype(F32)
    all_taken = n_valid <= kf

    def bisect_step(carry):
        lo_, hi_, clo = carry
        mid = 0.5 * (lo_ + hi_)
        cnt = lax.fori_loop(0, n_tiles,
                            lambda c, a: a + fold(jnp.where(sc_scr[c] >= mid, 1.0, 0.0), jnp.sum),
                            jnp.zeros((8, QB), F32))
        cm = jnp.sum(cnt, axis=0, keepdims=True)
        ge = cm >= kf
        return jnp.where(ge, mid, lo_), jnp.where(ge, hi_, mid), jnp.where(ge, cm, clo)

    def round_cond(carry):
        r, _, _, clo = carry
        unsettled = jnp.where(all_taken, 0.0, jnp.where(clo == kf, 0.0, 1.0))
        return jnp.logical_and(r < BISECT_ROUNDS, jnp.max(unsettled) > 0.0)

    def round_body(carry):
        st = carry[1:]
        for _ in range(BISECT_STEPS):
            st = bisect_step(st)
        return (carry[0] + 1,) + st

    _, lo, hi, c_lo = lax.while_loop(round_cond, round_body, (jnp.int32(0), lo, hi, n_valid))
    exact_cut = c_lo == kf

    def thr_body(c, carry):
        vm, cg = carry
        sc = sc_scr[c]
        below = sc < hi
        return (jnp.maximum(vm, fold(jnp.where(below, sc, NEG_INF), jnp.max)),
                cg + fold(jnp.where(below, 0.0, 1.0), jnp.sum))

    vm, cg = lax.fori_loop(0, n_tiles, thr_body,
                           (jnp.full((8, QB), NEG_INF, F32), jnp.zeros((8, QB), F32)))
    thr = jnp.max(vm, axis=0, keepdims=True)
    need = kf - jnp.sum(cg, axis=0, keepdims=True)
    tri = jnp.where(_iota((KT, KT), 1) <= _iota((KT, KT), 0), 1.0, 0.0).astype(BF16)

    def mask_body(c, seen):
        base = c * KT
        sc = sc_scr[c]
        eq = sc == thr
        prefix = _dot(tri, jnp.where(eq, 1.0, 0.0).astype(BF16)) + seen
        tied = jnp.where(eq, jnp.where(prefix <= need, 1.0, 0.0), 0.0)
        inside = jnp.where(sc >= lo, jnp.where(exact_cut, 1.0, tied), 0.0)
        chosen = jnp.where(sc >= hi, 1.0, inside)
        valid = jnp.where(key_sub + base <= tq, 1.0, 0.0)
        bias_t = (jnp.where(all_taken, valid, chosen) - 1.0) * (-NEG_INF)
        for qh in range(QB // LANES):
            for kh in range(KT // LANES):
                bias_scr[c, qh * LANES:(qh + 1) * LANES, kh * LANES:(kh + 1) * LANES] = (
                    bias_t[kh * LANES:(kh + 1) * LANES, qh * LANES:(qh + 1) * LANES].T)
        return prefix[KT - 1:KT, :]

    lax.fori_loop(0, n_tiles, mask_body, jnp.zeros((1, QB), F32))

    rows = C_HEADS * QB
    q = q_ref[0].reshape(rows, LANES) * ATT_SCALE
    mx_scr[...] = jnp.full((rows, LANES), NEG_INF, F32)

    def att_pass1(c, _):
        base = pl.multiple_of(c * KT, KT)
        s = _dot_nt(q, k_ref[0, pl.ds(base, KT), :])
        s = (s.reshape(C_HEADS, QB, KT) + bias_scr[c][None]).reshape(rows, KT)
        s_scr[c] = s
        mx_scr[...] = jnp.maximum(mx_scr[...], jnp.maximum(s[:, :LANES], s[:, LANES:]))
        return 0

    lax.fori_loop(0, n_tiles, att_pass1, 0)
    m = jnp.max(mx_scr[...], axis=-1, keepdims=True)
    acc_scr[...] = jnp.zeros((rows, LANES), F32)

    def att_pass2(c, _):
        base = pl.multiple_of(c * KT, KT)
        p = jnp.exp(s_scr[c] - m).astype(BF16)
        acc_scr[...] += _dot(p, v_ref[0, pl.ds(base, KT), :])
        return 0

    lax.fori_loop(0, n_tiles, att_pass2, 0)
    o = _normalize(acc_scr[...])
    _pair_store(o_ref, [o[h * QB:(h + 1) * QB] for h in range(C_HEADS)])


def _dsa(q, k, v, q_idx, k_idx, w_idx_t):
    bsz, _, seq, _ = q.shape
    n_qb = seq // QB
    top_k = min(DSA_TOPK, seq // 4)
    rows = C_HEADS * QB
    single = pl.BlockSpec((1, seq, LANES), lambda b, i: (b, 0, 0))
    return pl.pallas_call(
        functools.partial(_dsa_kernel, top_k=top_k),
        grid=(bsz, n_qb),
        in_specs=[pl.BlockSpec((1, C_HEADS, QB, LANES), lambda b, i: (b, 0, i, 0)),
                  single, single,
                  pl.BlockSpec((1, IDX_HEADS, QB, LANES), lambda b, i: (b, 0, i, 0)),
                  single,
                  pl.BlockSpec((1, IDX_HEADS, QB), lambda b, i: (b, 0, i))],
        out_specs=pl.BlockSpec((1, QB, C_WIDTH), lambda b, i: (b, i, 0)),
        out_shape=jax.ShapeDtypeStruct((bsz, seq, C_WIDTH), BF16),
        scratch_shapes=[pltpu.VMEM((seq // KT, KT, QB), F32),
                        pltpu.VMEM((seq // KT, QB, KT), F32),
                        pltpu.VMEM((seq // KT, rows, KT), F32),
                        pltpu.VMEM((rows, LANES), F32),
                        pltpu.VMEM((rows, LANES), F32)],
        compiler_params=_params("parallel", "arbitrary"),
        name="dsa_attention",
    )(q, k, v, q_idx, k_idx, w_idx_t)


def _moba_kernel(q_ref, k_ref, v_ref, km_ref, o_ref, s_scr, m_scr, acc_scr, *, top_k):
    i = pl.program_id(1)
    q0 = i * QB
    own = q0 // MOBA_BLOCK
    n_blk = km_ref.shape[2]
    q = q_ref[0]
    gate = _bdot_nt(km_ref[0], q)
    jb = _iota(gate.shape, 1)
    past = jb < own
    gate = jnp.where(past, gate, NEG_INF)
    sel_t = jnp.where(past, _rank_lt(gate, 1, n_blk, top_k), 0.0)
    pen_t = jnp.where(jb == own, 0.0, (sel_t - 1.0) * (-NEG_INF))
    pen = _penalty_rows(pen_t.reshape(D_HEADS * n_blk, QB))
    q_ext = (q.astype(F32) * ATT_SCALE + pen[None]).astype(BF16)

    m_scr[...] = jnp.full(m_scr.shape, NEG_INF, F32)
    tq = q0 + _iota((D_HEADS, QB, MOBA_BLOCK), 1)
    key = _iota((D_HEADS, QB, MOBA_BLOCK), 2)

    def scores(j, masked):
        base = pl.multiple_of(j * MOBA_BLOCK, MOBA_BLOCK)
        s = _bdot_nt(q_ext, k_ref[0, :, pl.ds(base, MOBA_BLOCK), :])
        if masked:
            s = jnp.where(key + base <= tq, s, NEG_INF)
        s_scr[j] = s
        m_scr[...] = jnp.maximum(m_scr[...], jnp.maximum(s[..., :LANES], s[..., LANES:]))

    def pass1(j, _):
        scores(j, False)
        return 0

    lax.fori_loop(0, own, pass1, 0)
    scores(own, True)
    m = jnp.max(m_scr[...], axis=-1, keepdims=True)
    acc_scr[...] = jnp.zeros(acc_scr.shape, F32)

    def pass2(j, _):
        base = pl.multiple_of(j * MOBA_BLOCK, MOBA_BLOCK)
        p = jnp.exp(s_scr[j] - m).astype(BF16)
        acc_scr[...] += _bdot(p, v_ref[0, :, pl.ds(base, MOBA_BLOCK), :])
        return 0

    lax.fori_loop(0, own + 1, pass2, 0)
    o = _normalize(acc_scr[...])
    _pair_store(o_ref, [o[h] for h in range(D_HEADS)])


def _moba(q, k, v, k_mean):
    bsz, _, seq, _ = q.shape
    n_qb = seq // QB
    n_blk = seq // MOBA_BLOCK
    top_k = min(MOBA_TOPK, n_blk - 1)
    q_spec = pl.BlockSpec((1, D_HEADS, QB, LANES), lambda b, i: (b, 0, i, 0))
    kv_spec = pl.BlockSpec((1, D_HEADS, seq, LANES), lambda b, i: (b, 0, 0, 0))
    return pl.pallas_call(
        functools.partial(_moba_kernel, top_k=top_k),
        grid=(bsz, n_qb),
        in_specs=[q_spec, kv_spec, kv_spec,
                  pl.BlockSpec((1, D_HEADS, n_blk, LANES), lambda b, i: (b, 0, 0, 0))],
        out_specs=pl.BlockSpec((1, QB, D_WIDTH), lambda b, i: (b, i, 0)),
        out_shape=jax.ShapeDtypeStruct((bsz, seq, D_WIDTH), BF16),
        scratch_shapes=[pltpu.VMEM((n_blk, D_HEADS, QB, MOBA_BLOCK), F32),
                        pltpu.VMEM((D_HEADS, QB, LANES), F32),
                        pltpu.VMEM((D_HEADS, QB, LANES), F32)],
        compiler_params=_params("parallel", "arbitrary"),
        name="moba_attention",
    )(q, k, v, k_mean)


def _cols(w, start, size):
    return w[:, start:start + size]


def _even_mixer(x, bsz, seq, tab64, norm_g, w_in, sgu_norm, sgu_w, sgu_b,
                cmp_pos_k, cmp_w1_k, cmp_w2_k, cmp_pos_v, cmp_w1_v, cmp_w2_v, w_out):
    o_q = 2 * A_WIDTH
    o_kc = o_q + B_WIDTH
    o_vc, o_ksl, o_vsl, o_kw, o_vw = (o_kc + B_KV_WIDTH * n for n in range(1, 6))
    o_gl = o_kc + 6 * B_KV_WIDTH
    n_gate = B_HEADS * N_BRANCH
    w_a = _cols(w_in, 0, 2 * A_WIDTH)
    w_q = _cols(w_in, o_q, B_WIDTH)
    w_plain = jnp.concatenate([_cols(w_in, o, B_KV_WIDTH) for o in (o_kc, o_vc, o_vsl, o_vw)], axis=1)
    w_rope = jnp.concatenate([_cols(w_in, o, B_KV_WIDTH) for o in (o_ksl, o_kw)], axis=1)
    w_gate = jnp.pad(_cols(w_in, o_gl, n_gate), ((0, 0), (0, LANES - n_gate)))
    a_in, q_raw, q_rot, kc, vc, v_slc, v_win, k_slc, k_win, gates = _even_proj(
        x, norm_g, tab64, [w_a, w_q, w_plain, w_rope, w_gate], bsz, seq)

    a_out = _gmlp(a_in, sgu_norm, sgu_w, sgu_b)

    def cmp_rows(t):
        t = t.reshape(bsz, seq // CMP_STRIDE, CMP_STRIDE, B_KV_HEADS, HEAD_DIM)
        return t.transpose(0, 3, 1, 2, 4).reshape(bsz, B_KV_HEADS, seq // CMP_STRIDE, CMP_STRIDE * HEAD_DIM)

    k_cmp, v_cmp = _compress(cmp_rows(kc), cmp_rows(vc), cmp_pos_k, cmp_w1_k, cmp_w2_k,
                             cmp_pos_v, cmp_w1_v, cmp_w2_v)
    b_out = _nsa(q_rot, q_raw, gates.reshape(bsz, seq, LANES), k_cmp, v_cmp, k_slc, v_slc, k_win, v_win)
    return _outproj(x, a_out, b_out.reshape(bsz * seq, B_WIDTH), w_out)


def _odd_mixer(x, bsz, seq, tab64, tab32, norm_g, w_in, w_out):
    sizes = (C_WIDTH, HEAD_DIM, HEAD_DIM, IDX_HEADS * IDX_DIM, IDX_DIM, IDX_HEADS, D_WIDTH, D_WIDTH, D_WIDTH)
    offs = np.concatenate([[0], np.cumsum(sizes)])
    w_qc, w_kc, w_vc, w_qi, w_ki, w_wi, w_qd, w_kd, w_vd = (
        _cols(w_in, int(o), int(s)) for o, s in zip(offs[:-1], sizes))
    zpad = lambda n: jnp.zeros((D_MODEL, n), w_in.dtype)
    w_rope = jnp.concatenate([w_qc, w_kc, zpad(HEAD_DIM), w_qd, w_kd], axis=1)
    w_ropei = jnp.concatenate([w_qi, w_ki, zpad(OD_ROPEI - IDX_HEADS * IDX_DIM - IDX_DIM)], axis=1)
    w_plain = jnp.concatenate([w_vc, w_wi, zpad(HEAD_DIM - IDX_HEADS), w_vd], axis=1)
    qc, kc, vc, qi, ki, wi_t, qd, kd, vd, k_mean = _odd_proj(
        x, norm_g, tab64, tab32, [w_rope, w_ropei, w_plain], bsz, seq)

    c_out = _dsa(qc, kc, vc, qi, ki, wi_t)
    n_blk = seq // MOBA_BLOCK
    km = k_mean.reshape(bsz, n_blk, D_HEADS, HEAD_DIM).transpose(0, 2, 1, 3)
    km = jnp.pad(km, ((0, 0), (0, 0), (0, 0), (0, LANES - HEAD_DIM))).astype(BF16)
    d_out = _moba(qd, kd, vd, km)
    return _outproj(x, c_out.reshape(bsz * seq, C_WIDTH), d_out.reshape(bsz * seq, D_WIDTH), w_out)


def kernel(x, positions, ffn1_norm, ffn1_w_gate, ffn1_w_up, ffn1_w_down, mix_norm, ffn2_norm, ffn2_w_gate, ffn2_w_up, ffn2_w_down, ev_w_in, ev_sgu_norm, ev_sgu_w, ev_sgu_b, ev_cmp_pos_k, ev_cmp_w1_k, ev_cmp_w2_k, ev_cmp_pos_v, ev_cmp_w1_v, ev_cmp_w2_v, ev_w_out, od_w_in, od_w_out, final_norm):
    bsz, seq, _ = x.shape
    depth = ffn1_norm.shape[0]
    tab64 = _rope_tables(positions, ROT_DIM, HEAD_DIM)
    tab32 = _rope_tables(positions, IDX_ROT, IDX_DIM)
    x = x.reshape(bsz * seq, D_MODEL)
    for i in range(depth):
        x = _ffn(x, ffn1_norm[i], ffn1_w_gate[i], ffn1_w_up[i], ffn1_w_down[i])
        if i % 2 == 0:
            e = i // 2
            x = _even_mixer(x, bsz, seq, tab64, mix_norm[i], ev_w_in[e], ev_sgu_norm[e], ev_sgu_w[e],
                            ev_sgu_b[e], ev_cmp_pos_k[e], ev_cmp_w1_k[e], ev_cmp_w2_k[e],
                            ev_cmp_pos_v[e], ev_cmp_w1_v[e], ev_cmp_w2_v[e], ev_w_out[e])
        else:
            o = i // 2
            x = _odd_mixer(x, bsz, seq, tab64, tab32, mix_norm[i], od_w_in[o], od_w_out[o])
        x = _ffn(x, ffn2_norm[i], ffn2_w_gate[i], ffn2_w_up[i], ffn2_w_down[i],
                 final_g=final_norm if i == depth - 1 else None)
    return x.reshape(bsz, seq, D_MODEL)
```

```python
import functools

import numpy as np
import jax
import jax.numpy as jnp
from jax import lax
from jax.experimental import pallas as pl
from jax.experimental.pallas import tpu as pltpu

F32 = jnp.float32
BF16 = jnp.bfloat16

D_MODEL = 1024
HEAD_DIM = 64
ROT_DIM = HEAD_DIM // 4
ROPE_THETA = 500000.0
NORM_EPS = 1e-6
D_FF = 2816
NEG_INF = -1e30

A_GROUPS = 4
A_CHUNK = 128
A_WIDTH = A_GROUPS * HEAD_DIM
B_HEADS = 12
B_KV_HEADS = 3
B_GROUP = B_HEADS // B_KV_HEADS
B_WIDTH = B_HEADS * HEAD_DIM
B_KV_WIDTH = B_KV_HEADS * HEAD_DIM
CMP_LEN = 32
CMP_STRIDE = 16
CMP_HIDDEN = 256
SLC_BLOCK = 64
SLC_TOPN = 8
WINDOW = 512
N_BRANCH = 3
FORCE_SCORE = 1e4
C_HEADS = 8
C_WIDTH = C_HEADS * HEAD_DIM
IDX_HEADS = 4
IDX_DIM = 32
IDX_ROT = IDX_DIM // 4
DSA_TOPK = 256
D_HEADS = 8
D_WIDTH = D_HEADS * HEAD_DIM
MOBA_BLOCK = 256
MOBA_TOPK = 3

LANES = 128
ATT_SCALE = HEAD_DIM ** -0.5
VMEM_LIMIT = 56 * 1024 * 1024
TM = 512
FF_CHUNK = D_FF // 11
KT = 256
QB = 256
QK_SCALE = ATT_SCALE * 1.4426950408889634
SEARCH_ROUNDS = 32
SEARCH_INTERP = 3
ONES_LANE = HEAD_DIM
AUX_LANE = HEAD_DIM


def _dot(a, b):
    return jnp.dot(a, b, preferred_element_type=F32)


def _dot_nt(a, b):
    return lax.dot_general(a, b, (((1,), (1,)), ((), ())), preferred_element_type=F32)


def _bdot_nt(a, b):
    return lax.dot_general(a, b, (((2,), (2,)), ((0,), (0,))), preferred_element_type=F32)


def _bdot(a, b):
    return lax.dot_general(a, b, (((2,), (1,)), ((0,), (0,))), preferred_element_type=F32)


def _iota(shape, dim):
    return lax.broadcasted_iota(jnp.int32, shape, dim)


def _rms(x, g):
    ms = jnp.mean(x * x, axis=-1, keepdims=True)
    return x * lax.rsqrt(ms + NORM_EPS) * g


def _gelu(x):
    return x * (0.5 * (1.0 + jnp.tanh(0.7978845608028654 * (x + 0.044715 * (x * x * x)))))


def _params(*sem):
    return pltpu.CompilerParams(dimension_semantics=sem, vmem_limit_bytes=VMEM_LIMIT)


def _full(shape):
    n = len(shape)
    return pl.BlockSpec(shape, lambda *_: (0,) * n)


def _resident(shape):
    n = len(shape)
    return pl.BlockSpec(shape, lambda *_: (0,) * n, pipeline_mode=pl.Buffered(1))


def _ffn_kernel(x_ref, g_ref, wg_ref, wu_ref, wd_ref, *rest, final):
    o_ref = rest[-1]
    x = x_ref[...]
    h = _rms(x, g_ref[...]).astype(BF16)
    acc = jnp.zeros_like(x)
    for c in range(D_FF // FF_CHUNK):
        sl = slice(c * FF_CHUNK, (c + 1) * FF_CHUNK)
        gate = _dot(h, wg_ref[:, sl])
        up = _dot(h, wu_ref[:, sl])
        act = (gate * jax.nn.sigmoid(gate) * up).astype(BF16)
        acc = acc + _dot(act, wd_ref[sl, :])
    y = x + 0.5 * acc
    if final:
        y = _rms(y, rest[0][...])
    o_ref[...] = y


def _ffn(x, g, wg, wu, wd, final_g=None):
    t = x.shape[0]
    final = final_g is not None
    ins = [x, g.reshape(1, D_MODEL), wg.astype(BF16), wu.astype(BF16), wd.astype(BF16)]
    specs = [pl.BlockSpec((TM, D_MODEL), lambda i: (i, 0)), _full((1, D_MODEL)),
             _resident((D_MODEL, D_FF)), _resident((D_MODEL, D_FF)), _resident((D_FF, D_MODEL))]
    if final:
        ins.append(final_g.reshape(1, D_MODEL))
        specs.append(_full((1, D_MODEL)))
    return pl.pallas_call(
        functools.partial(_ffn_kernel, final=final),
        grid=(t // TM,),
        in_specs=specs,
        out_specs=pl.BlockSpec((TM, D_MODEL), lambda i: (i, 0)),
        out_shape=jax.ShapeDtypeStruct((t, D_MODEL), F32),
        compiler_params=_params("parallel"),
        name="ffn_final" if final else "ffn",
    )(*ins)


def _rope_tiles(z, tab_ref, shift, period):
    cs, sn = tab_ref[0], tab_ref[1]
    is_x1 = (_iota(cs.shape, 1) & (period - 1)) < shift
    outs = []
    for c in range(z.shape[1] // LANES):
        zt = z[:, c * LANES:(c + 1) * LANES]
        up = pltpu.roll(zt, LANES - shift, axis=1)
        dn = pltpu.roll(zt, shift, axis=1)
        outs.append(zt * cs + jnp.where(is_x1, up, dn) * sn)
    return outs[0] if len(outs) == 1 else jnp.concatenate(outs, axis=1)


def _head_row(z, col, width, aux):
    tile = z[:, (col // LANES) * LANES:(col // LANES + 1) * LANES]
    off = col % LANES
    if off:
        tile = pltpu.roll(tile, LANES - off, axis=1)
    lane = _iota(tile.shape, 1)
    return jnp.where(lane < width, tile, aux)


def _seq_pos(sblk, rows):
    return sblk * rows + _iota((rows, LANES), 0)


def _even_proj_kernel(x_ref, g_ref, t64_ref, wa_ref, wq_ref, wp_ref, wr_ref, wg_ref,
                      a_ref, qn_ref, qr_ref, kc_ref, vc_ref, vs_ref, vw_ref, ks_ref, kw_ref, gt_ref,
                      *, n_sblk):
    sblk = pl.program_id(0) % n_sblk
    h = _rms(x_ref[...], g_ref[...]).astype(BF16)
    lane = _iota((TM, LANES), 1)
    ones_col = jnp.where(lane == ONES_LANE, 1.0, 0.0)
    slc_onehot = jnp.where(lane == AUX_LANE + (_seq_pos(sblk, TM) // SLC_BLOCK), 1.0, 0.0)

    a_ref[...] = _dot(h, wa_ref[...])
    zq = _dot(h, wq_ref[...]) * QK_SCALE
    zr = _rope_tiles(zq, t64_ref, ROT_DIM // 2, HEAD_DIM)
    for hh in range(B_HEADS):
        qn_ref[0, hh] = _head_row(zq, hh * HEAD_DIM, HEAD_DIM, 0.0).astype(BF16)
        qr_ref[0, hh] = _head_row(zr, hh * HEAD_DIM, HEAD_DIM, 0.0).astype(BF16)
    zp = _dot(h, wp_ref[...])
    kc_ref[...] = zp[:, :B_KV_WIDTH]
    vc_ref[...] = zp[:, B_KV_WIDTH:2 * B_KV_WIDTH]
    zk = _rope_tiles(_dot(h, wr_ref[...]), t64_ref, ROT_DIM // 2, HEAD_DIM)
    for g in range(B_KV_HEADS):
        vs_ref[0, g] = _head_row(zp, (2 * B_KV_HEADS + g) * HEAD_DIM, HEAD_DIM, ones_col).astype(BF16)
        vw_ref[0, g] = _head_row(zp, (3 * B_KV_HEADS + g) * HEAD_DIM, HEAD_DIM, ones_col).astype(BF16)
        ks_ref[0, g] = _head_row(zk, g * HEAD_DIM, HEAD_DIM, slc_onehot).astype(BF16)
        kw_ref[0, g] = _head_row(zk, (B_KV_HEADS + g) * HEAD_DIM, HEAD_DIM, 0.0).astype(BF16)
    gt_ref[...] = jax.nn.sigmoid(_dot(h, wg_ref[...]))


def _even_proj(x, g, tab64, weights, bsz, seq):
    t = x.shape[0]
    n_sblk = seq // TM
    tok = lambda n: pl.BlockSpec((TM, n), lambda i: (i, 0))
    heads = lambda n: pl.BlockSpec((1, n, TM, LANES), lambda i: (i // n_sblk, 0, i % n_sblk, 0))
    hshape = lambda n: jax.ShapeDtypeStruct((bsz, n, seq, LANES), BF16)
    return pl.pallas_call(
        functools.partial(_even_proj_kernel, n_sblk=n_sblk),
        grid=(t // TM,),
        in_specs=[tok(D_MODEL), _full((1, D_MODEL)), pl.BlockSpec((2, TM, LANES), lambda i: (0, i, 0))]
                 + [_resident(w.shape) for w in weights],
        out_specs=[tok(2 * A_WIDTH), heads(B_HEADS), heads(B_HEADS), tok(B_KV_WIDTH), tok(B_KV_WIDTH),
                   heads(B_KV_HEADS), heads(B_KV_HEADS), heads(B_KV_HEADS), heads(B_KV_HEADS), tok(LANES)],
        out_shape=[jax.ShapeDtypeStruct((t, 2 * A_WIDTH), F32), hshape(B_HEADS), hshape(B_HEADS),
                   jax.ShapeDtypeStruct((t, B_KV_WIDTH), F32), jax.ShapeDtypeStruct((t, B_KV_WIDTH), F32),
                   hshape(B_KV_HEADS), hshape(B_KV_HEADS), hshape(B_KV_HEADS), hshape(B_KV_HEADS),
                   jax.ShapeDtypeStruct((t, LANES), F32)],
        compiler_params=_params("parallel"),
        name="even_in_proj",
    )(x, g.reshape(1, D_MODEL), tab64, *[w.astype(BF16) for w in weights])


OD_ROPE = C_WIDTH + 2 * HEAD_DIM + 2 * D_WIDTH
OD_KD = C_WIDTH + 2 * HEAD_DIM + D_WIDTH
OD_ROPEI = 2 * LANES
OD_PLAIN = 2 * HEAD_DIM + D_WIDTH


def _odd_proj_kernel(x_ref, g_ref, t64_ref, t32_ref, wr_ref, wi_ref, wp_ref,
                     qc_ref, kc_ref, vc_ref, qi_ref, ki_ref, wt_ref, qd_ref, kd_ref, vd_ref, km_ref,
                     *, n_sblk):
    sblk = pl.program_id(0) % n_sblk
    h = _rms(x_ref[...], g_ref[...]).astype(BF16)
    lane = _iota((TM, LANES), 1)
    ones_col = jnp.where(lane == ONES_LANE, 1.0, 0.0)
    blk = _seq_pos(sblk, TM) // MOBA_BLOCK

    zr = _rope_tiles(_dot(h, wr_ref[...]), t64_ref, ROT_DIM // 2, HEAD_DIM)
    for hh in range(C_HEADS):
        qc_ref[0, hh] = (_head_row(zr, hh * HEAD_DIM, HEAD_DIM, 0.0) * QK_SCALE).astype(BF16)
    kc_ref[0] = _head_row(zr, C_WIDTH, HEAD_DIM, 0.0).astype(BF16)
    for hh in range(D_HEADS):
        qd = _head_row(zr, C_WIDTH + 2 * HEAD_DIM + hh * HEAD_DIM, HEAD_DIM, 0.0)
        qd_ref[0, hh] = (qd * QK_SCALE).astype(BF16)
        onehot = jnp.where(lane == AUX_LANE + hh * (LANES - AUX_LANE) // D_HEADS + blk, 1.0, 0.0)
        kd_ref[0, hh] = _head_row(zr, OD_KD + hh * HEAD_DIM, HEAD_DIM, onehot).astype(BF16)
    zkd = zr[:, OD_KD:]
    n_mb = TM // MOBA_BLOCK
    km_ref[0] = jnp.sum(zkd.reshape(n_mb, MOBA_BLOCK, D_WIDTH), axis=1) * (1.0 / MOBA_BLOCK)

    zi = _rope_tiles(_dot(h, wi_ref[...]), t32_ref, IDX_ROT // 2, IDX_DIM)
    for hh in range(IDX_HEADS):
        qi_ref[0, hh] = _head_row(zi, hh * IDX_DIM, IDX_DIM, 0.0).astype(BF16)
    ki_ref[0] = _head_row(zi, IDX_HEADS * IDX_DIM, IDX_DIM, 0.0).astype(BF16)

    zp = _dot(h, wp_ref[...])
    vc_ref[0] = _head_row(zp, 0, HEAD_DIM, ones_col).astype(BF16)
    wt_ref[0] = pltpu.roll(zp[:, :LANES], LANES - HEAD_DIM, axis=1).T[:IDX_HEADS]
    for hh in range(D_HEADS):
        vd_ref[0, hh] = _head_row(zp, 2 * HEAD_DIM + hh * HEAD_DIM, HEAD_DIM, ones_col).astype(BF16)


def _odd_proj(x, g, tab64, tab32, weights, bsz, seq):
    t = x.shape[0]
    n_sblk = seq // TM
    tok = lambda n: pl.BlockSpec((TM, n), lambda i: (i, 0))
    tab = pl.BlockSpec((2, TM, LANES), lambda i: (0, i, 0))
    heads = lambda n: pl.BlockSpec((1, n, TM, LANES), lambda i: (i // n_sblk, 0, i % n_sblk, 0))
    single = pl.BlockSpec((1, TM, LANES), lambda i: (i // n_sblk, i % n_sblk, 0))
    hshape = lambda n: jax.ShapeDtypeStruct((bsz, n, seq, LANES), BF16)
    sshape = jax.ShapeDtypeStruct((bsz, seq, LANES), BF16)
    return pl.pallas_call(
        functools.partial(_odd_proj_kernel, n_sblk=n_sblk),
        grid=(t // TM,),
        in_specs=[tok(D_MODEL), _full((1, D_MODEL)), tab, tab] + [_resident(w.shape) for w in weights],
        out_specs=[heads(C_HEADS), single, single, heads(IDX_HEADS), single,
                   pl.BlockSpec((1, IDX_HEADS, TM), lambda i: (i // n_sblk, 0, i % n_sblk)),
                   heads(D_HEADS), heads(D_HEADS), heads(D_HEADS),
                   pl.BlockSpec((1, TM // MOBA_BLOCK, D_WIDTH), lambda i: (i, 0, 0))],
        out_shape=[hshape(C_HEADS), sshape, sshape, hshape(IDX_HEADS), sshape,
                   jax.ShapeDtypeStruct((bsz, IDX_HEADS, seq), F32),
                   hshape(D_HEADS), hshape(D_HEADS), hshape(D_HEADS),
                   jax.ShapeDtypeStruct((t // TM, TM // MOBA_BLOCK, D_WIDTH), F32)],
        compiler_params=_params("parallel"),
        name="odd_in_proj",
    )(x, g.reshape(1, D_MODEL), tab64, tab32, *[w.astype(BF16) for w in weights])


def _rope_tables(positions, rot_dim, period):
    half = rot_dim // 2
    inv_freq = ROPE_THETA ** (-jnp.arange(0, rot_dim, 2, dtype=F32) / rot_dim)
    ang = positions.astype(F32).reshape(-1, 1) * inv_freq
    cos, sin = jnp.cos(ang), jnp.sin(ang)
    t = cos.shape[0]
    cs = jnp.concatenate([cos, cos, jnp.ones((t, period - 2 * half), F32)], axis=1)
    sn = jnp.concatenate([-sin, sin, jnp.zeros((t, period - 2 * half), F32)], axis=1)
    rep = LANES // period
    return jnp.stack([jnp.tile(a, (1, rep)) for a in (cs, sn)])


def _outproj_kernel(x_ref, a_ref, b_ref, wa_ref, wb_ref, o_ref):
    o_ref[...] = x_ref[...] + (_dot(a_ref[...], wa_ref[...]) + _dot(b_ref[...], wb_ref[...]))


def _outproj(x, a, b, w_out):
    t = x.shape[0]
    na, nb = a.shape[1], b.shape[1]
    wa, wb = w_out[:na].astype(BF16), w_out[na:].astype(BF16)
    return pl.pallas_call(
        _outproj_kernel,
        grid=(t // TM,),
        in_specs=[pl.BlockSpec((TM, D_MODEL), lambda i: (i, 0)),
                  pl.BlockSpec((TM, na), lambda i: (i, 0)),
                  pl.BlockSpec((TM, nb), lambda i: (i, 0)),
                  _resident((na, D_MODEL)), _resident((nb, D_MODEL))],
        out_specs=pl.BlockSpec((TM, D_MODEL), lambda i: (i, 0)),
        out_shape=jax.ShapeDtypeStruct((t, D_MODEL), F32),
        compiler_params=_params("parallel"),
        name="mixer_out_proj",
    )(x, a, b, wa, wb)


def _gmlp_kernel(a_ref, n_ref, w_ref, b_ref, o_ref):
    causal = _iota((A_CHUNK, A_CHUNK), 1) <= _iota((A_CHUNK, A_CHUNK), 0)
    lane_group = _iota((A_CHUNK, A_WIDTH), 1) // HEAD_DIM
    ws = [jnp.where(causal, w_ref[g], 0.0).astype(BF16) for g in range(A_GROUPS)]
    for c in range(TM // A_CHUNK):
        rs = slice(c * A_CHUNK, (c + 1) * A_CHUNK)
        z = _gelu(a_ref[rs, :])
        u = z[:, :A_WIDTH]
        v = _rms(z[:, A_WIDTH:], n_ref[...]).astype(BF16)
        mixed = jnp.zeros((A_CHUNK, A_WIDTH), F32)
        for g in range(A_GROUPS):
            mixed = jnp.where(lane_group == g, _dot(ws[g], v) + b_ref[g], mixed)
        o_ref[rs, :] = (u * mixed).astype(BF16)


def _gmlp(a_in, sgu_norm, sgu_w, sgu_b):
    t = a_in.shape[0]
    return pl.pallas_call(
        _gmlp_kernel,
        grid=(t // TM,),
        in_specs=[pl.BlockSpec((TM, 2 * A_WIDTH), lambda i: (i, 0)),
                  _full((1, A_WIDTH)), _full((A_GROUPS, A_CHUNK, A_CHUNK)),
                  _full((A_GROUPS, A_CHUNK, 1))],
        out_specs=pl.BlockSpec((TM, A_WIDTH), lambda i: (i, 0)),
        out_shape=jax.ShapeDtypeStruct((t, A_WIDTH), BF16),
        compiler_params=_params("parallel"),
        name="gmlp",
    )(a_in, sgu_norm.reshape(1, A_WIDTH), sgu_w, sgu_b.reshape(A_GROUPS, A_CHUNK, 1))


def _compress_one(t2, pos_ref, w1_ref, w2_ref):
    a = _dot((t2 + pos_ref[0:1, :]).astype(BF16), w1_ref[0])
    b = _dot((t2 + pos_ref[1:2, :]).astype(BF16), w1_ref[1])
    hid = _gelu(a + pltpu.roll(b, t2.shape[0] - 1, axis=0))
    out = _dot(hid.astype(BF16), w2_ref[...])
    row = _iota(out.shape, 0)
    return jnp.where(row < t2.shape[0] - 1, out, 0.0).astype(BF16)


def _compress_kernel(kc_ref, vc_ref, pk_ref, w1k_ref, w2k_ref, pv_ref, w1v_ref, w2v_ref, ko_ref, vo_ref):
    for g in range(B_KV_HEADS):
        ko_ref[0, g] = _compress_one(kc_ref[0, g], pk_ref, w1k_ref, w2k_ref)
        vo_ref[0, g] = _compress_one(vc_ref[0, g], pv_ref, w1v_ref, w2v_ref)


def _compress(kc2, vc2, pos_k, w1_k, w2_k, pos_v, w1_v, w2_v):
    bsz, _, nrow, wide = kc2.shape

    def prep(pos, w1, w2):
        w2p = jnp.pad(w2, ((0, 0), (0, LANES - HEAD_DIM)))
        return (pos.reshape(2, wide), w1.reshape(2, wide, CMP_HIDDEN).astype(BF16), w2p.astype(BF16))

    pk, w1k, w2k = prep(pos_k, w1_k, w2_k)
    pv, w1v, w2v = prep(pos_v, w1_v, w2_v)
    blk_in = pl.BlockSpec((1, B_KV_HEADS, nrow, wide), lambda b: (b, 0, 0, 0))
    blk_out = pl.BlockSpec((1, B_KV_HEADS, nrow, LANES), lambda b: (b, 0, 0, 0))
    wspecs = [_full((2, wide)), _full((2, wide, CMP_HIDDEN)), _full((CMP_HIDDEN, LANES))]
    return pl.pallas_call(
        _compress_kernel,
        grid=(bsz,),
        in_specs=[blk_in, blk_in] + wspecs + wspecs,
        out_specs=[blk_out, blk_out],
        out_shape=[jax.ShapeDtypeStruct((bsz, B_KV_HEADS, nrow, LANES), BF16)] * 2,
        compiler_params=_params("parallel"),
        name="nsa_compress",
    )(kc2, vc2, pk, w1k, w2k, pv, w1v, w2v)


def _rank_lt(vals, axis, n, k):
    j = _iota(vals.shape, axis)
    rank = jnp.zeros(vals.shape, F32)
    for jp in range(n):
        row = lax.slice_in_dim(vals, jp, jp + 1, axis=axis)
        beats = jnp.where(row > vals, 1.0, jnp.where(row == vals, jnp.where(j > jp, 1.0, 0.0), 0.0))
        rank = rank + beats
    return jnp.where(rank < k, 1.0, 0.0)


def _penalty_rows(pen_t):
    n, nq = pen_t.shape
    parts = [jnp.zeros((AUX_LANE, nq), F32), pen_t]
    if LANES - AUX_LANE - n:
        parts.append(jnp.zeros((LANES - AUX_LANE - n, nq), F32))
    full = jnp.concatenate(parts, axis=0)
    halves = [full[:, c * LANES:(c + 1) * LANES].T for c in range(nq // LANES)]
    return halves[0] if len(halves) == 1 else jnp.concatenate(halves, axis=0)


def _normalize(acc):
    den = lax.slice_in_dim(acc, ONES_LANE, ONES_LANE + 1, axis=acc.ndim - 1)
    return acc / den


def _pair_store(o_ref, heads_out):
    lane = _iota((QB, LANES), 1)
    for p in range(len(heads_out) // 2):
        both = jnp.where(lane < HEAD_DIM, heads_out[2 * p], pltpu.roll(heads_out[2 * p + 1], HEAD_DIM, axis=1))
        o_ref[0, :, p * LANES:(p + 1) * LANES] = both.astype(BF16)


def _nsa_kernel(qr_ref, qn_ref, gt_ref, kc_ref, vc_ref, ks_ref, vs_ref, kw_ref, vw_ref, ovt_ref,
                o_ref, s_scr, q_scr, cmp_scr, m_scr, acc_scr):
    i = pl.program_id(1)
    q0 = i * QB
    rows = B_GROUP * QB
    n_slc = ovt_ref.shape[0]
    n_g = B_KV_HEADS
    gates = gt_ref[0]
    t_row = q0 + (_iota((rows, LANES), 0) & (QB - 1))
    lane = _iota((rows, LANES), 1)
    t_row2 = q0 + (_iota((rows, KT), 0) & (QB - 1))
    lane2 = _iota((rows, KT), 1)

    for g in range(n_g):
        hs = slice(g * B_GROUP, (g + 1) * B_GROUP)
        qn = qn_ref[0, hs].reshape(rows, LANES)
        s_c = _dot_nt(qn, kc_ref[0, g])
        m_c = (lane * CMP_STRIDE + (CMP_LEN - 1)) <= t_row
        sm = jnp.where(m_c, s_c, NEG_INF)
        e = jnp.where(m_c, jnp.exp2(sm - jnp.max(sm, axis=-1, keepdims=True)), 0.0)
        den = jnp.sum(e, axis=-1, keepdims=True)
        p_cb = (e / jnp.where(den > 0.0, den, 1.0)).astype(BF16)
        cmp_scr[g] = _dot(p_cb, vc_ref[0, g])

        imp = jnp.zeros((n_slc, QB), F32)
        for r in range(B_GROUP):
            imp = imp + _dot_nt(ovt_ref[...], p_cb[r * QB:(r + 1) * QB])
        jb = _iota((n_slc, QB), 0)
        tq = q0 + _iota((n_slc, QB), 1)
        forced = (jb == 0) | (jb == (tq >> 6))
        imp = jnp.where(jb * SLC_BLOCK <= tq, jnp.where(forced, FORCE_SCORE, imp), NEG_INF)
        pen = _penalty_rows((_rank_lt(imp, 0, n_slc, SLC_TOPN) - 1.0) * (-NEG_INF))
        qr = qr_ref[0, hs].astype(F32) + pen[None]
        q_scr[g] = qr.reshape(rows, LANES).astype(BF16)

    n_wt = (WINDOW + QB + KT - 1) // KT
    w_off = [QB - (n_wt - c) * KT for c in range(n_wt)]
    w_base = [pl.multiple_of(jnp.maximum(q0 + off, 0), min(QB, KT)) for off in w_off]

    def window_scores(interior):
        w_mx = [jnp.full((rows, LANES), NEG_INF, F32) for _ in range(n_g)]
        for c in range(n_wt):
            key = lane2 + w_base[c]
            for g in range(n_g):
                s = _dot_nt(q_scr[g], kw_ref[0, g, pl.ds(w_base[c], KT), :])
                if not interior:
                    key_hi = jnp.minimum(t_row2, q0 + (w_off[c] + KT - 1))
                    s = jnp.where(key > t_row2 - WINDOW, jnp.where(key <= key_hi, s, NEG_INF), NEG_INF)
                else:
                    if w_off[c] < QB - WINDOW:
                        s = jnp.where(key > t_row2 - WINDOW, s, NEG_INF)
                    if w_off[c] + KT - 1 > 0:
                        s = jnp.where(key <= t_row2, s, NEG_INF)
                s_scr[c, g] = s
                w_mx[g] = jnp.maximum(w_mx[g], jnp.maximum(s[:, :LANES], s[:, LANES:]))
        for g in range(n_g):
            m_scr[g] = w_mx[g]

    first_interior = (n_wt * KT - QB + QB - 1) // QB
    pl.when(i >= first_interior)(functools.partial(window_scores, True))
    pl.when(i < first_interior)(functools.partial(window_scores, False))
    for g in range(n_g):
        w_max = jnp.max(m_scr[g], axis=-1, keepdims=True)
        acc_w = jnp.zeros((rows, LANES), F32)
        for c in range(n_wt):
            p = jnp.exp2(s_scr[c, g] - w_max).astype(BF16)
            acc_w = acc_w + _dot(p, vw_ref[0, g, pl.ds(w_base[c], KT), :])
        acc_scr[g] = _normalize(acc_w)

    n_tiles = (q0 + QB + KT - 1) // KT
    last = n_tiles - 1
    for g in range(n_g):
        m_scr[g] = jnp.full((rows, LANES), NEG_INF, F32)

    def sel_scores(c, masked):
        base = pl.multiple_of(c * KT, KT)
        for g in range(n_g):
            s = _dot_nt(q_scr[g], ks_ref[0, g, pl.ds(base, KT), :])
            if masked:
                s = jnp.where(lane2 + base <= t_row2, s, NEG_INF)
            s_scr[c, g] = s
            m_scr[g] = jnp.maximum(m_scr[g], jnp.maximum(s[:, :LANES], s[:, LANES:]))

    def sel_pass1(c, _):
        sel_scores(c, False)
        return 0

    lax.fori_loop(0, last, sel_pass1, 0)
    sel_scores(last, True)

    m_sel = [jnp.max(m_scr[g], axis=-1, keepdims=True) for g in range(n_g)]
    for g in range(n_g):
        m_scr[g] = jnp.zeros((rows, LANES), F32)

    def sel_pass2(c, _):
        base = pl.multiple_of(c * KT, KT)
        for g in range(n_g):
            p = jnp.exp2(s_scr[c, g] - m_sel[g]).astype(BF16)
            m_scr[g] += _dot(p, vs_ref[0, g, pl.ds(base, KT), :])
        return 0

    lax.fori_loop(0, n_tiles, sel_pass2, 0)

    outs = []
    for g in range(n_g):
        for r in range(B_GROUP):
            h = g * B_GROUP + r
            rs = slice(r * QB, (r + 1) * QB)
            c0 = h * N_BRANCH
            outs.append(gates[:, c0:c0 + 1] * cmp_scr[g, rs, :] + gates[:, c0 + 1:c0 + 2] * _normalize(m_scr[g, rs, :])
                        + gates[:, c0 + 2:c0 + 3] * acc_scr[g, rs, :])
    _pair_store(o_ref, outs)


def _overlap_t(n_rows, n_slc):
    n = np.arange(n_rows)
    c0 = n * CMP_STRIDE
    s0 = np.arange(n_slc) * SLC_BLOCK
    m = (c0[None, :] < s0[:, None] + SLC_BLOCK) & (c0[None, :] + CMP_LEN > s0[:, None])
    m = m & (n[None, :] < n_rows - 1)
    return jnp.asarray(m, dtype=BF16)


def _nsa(q_rot, q_raw, gates, k_cmp, v_cmp, k_slc, v_slc, k_win, v_win):
    bsz, _, seq, _ = q_rot.shape
    n_qb = seq // QB
    n_slc = seq // SLC_BLOCK
    n_cmp_rows = k_cmp.shape[2]
    ovt = _overlap_t(n_cmp_rows, n_slc)
    q_spec = pl.BlockSpec((1, B_HEADS, QB, LANES), lambda b, i: (b, 0, i, 0))
    cmp_spec = pl.BlockSpec((1, B_KV_HEADS, n_cmp_rows, LANES), lambda b, i: (b, 0, 0, 0))
    kv_spec = pl.BlockSpec((1, B_KV_HEADS, seq, LANES), lambda b, i: (b, 0, 0, 0), pipeline_mode=pl.Buffered(1))
    rows = B_GROUP * QB
    return pl.pallas_call(
        _nsa_kernel,
        grid=(bsz, n_qb),
        in_specs=[q_spec, q_spec, pl.BlockSpec((1, QB, LANES), lambda b, i: (b, i, 0)),
                  cmp_spec, cmp_spec, kv_spec, kv_spec, kv_spec, kv_spec, _full(ovt.shape)],
        out_specs=pl.BlockSpec((1, QB, B_WIDTH), lambda b, i: (b, i, 0)),
        out_shape=jax.ShapeDtypeStruct((bsz, seq, B_WIDTH), BF16),
        scratch_shapes=[pltpu.VMEM((seq // KT, B_KV_HEADS, rows, KT), F32),
                        pltpu.VMEM((B_KV_HEADS, rows, LANES), BF16),
                        pltpu.VMEM((B_KV_HEADS, rows, LANES), F32),
                        pltpu.VMEM((B_KV_HEADS, rows, LANES), F32),
                        pltpu.VMEM((B_KV_HEADS, rows, LANES), F32)],
        compiler_params=_params("parallel", "arbitrary"),
        name="nsa_attention",
    )(q_rot, q_raw, gates, k_cmp, v_cmp, k_slc, v_slc, k_win, v_win, ovt)


def _dsa_kernel(q_ref, k_ref, v_ref, qi_ref, ki_ref, wi_ref, o_ref,
                sc_scr, bias_scr, s_scr, mx_scr, acc_scr, *, top_k):
    i = pl.program_id(1)
    q0 = i * QB
    n_tiles = (q0 + QB + KT - 1) // KT
    tile = (KT, QB)
    key_sub = _iota(tile, 0)
    tq = q0 + _iota(tile, 1)
    w_idx = wi_ref[0]
    qi = qi_ref[0].reshape(IDX_HEADS * QB, LANES)
    n_sub = KT // 8

    def fold(x, op):
        return op(op(x.reshape(n_sub // 4, 4, 8, QB), axis=0), axis=0)

    w_s = w_idx * (IDX_DIM ** -0.5 * IDX_HEADS ** -0.5)
    last = n_tiles - 1

    def score_tile(c, carry, masked):
        mn, mx, zge, zgt = carry
        base = pl.multiple_of(c * KT, KT)
        lg = _dot_nt(ki_ref[0, pl.ds(base, KT), :], qi)
        sc = jnp.zeros(tile, F32)
        for h in range(IDX_HEADS):
            sc = sc + w_s[h:h + 1, :] * jnp.maximum(lg[:, h * QB:(h + 1) * QB], 0.0)
        if masked:
            valid = key_sub + base <= tq
            lowest = jnp.where(valid, sc, -NEG_INF)
            sc = jnp.where(valid, sc, NEG_INF)
        else:
            lowest = sc
        sc_scr[c] = sc
        return (jnp.minimum(mn, fold(lowest, jnp.min)), jnp.maximum(mx, fold(sc, jnp.max)),
                zge + fold(jnp.where(sc >= 0.0, 1.0, 0.0), jnp.sum),
                zgt + fold(jnp.where(sc > 0.0, 1.0, 0.0), jnp.sum))

    stats = lax.fori_loop(0, last, lambda c, carry: score_tile(c, carry, False),
                          (jnp.full((8, QB), -NEG_INF, F32), jnp.full((8, QB), NEG_INF, F32),
                           jnp.zeros((8, QB), F32), jnp.zeros((8, QB), F32)))
    mn, mx, zge, zgt = score_tile(last, stats, True)
    mn = jnp.min(mn, axis=0, keepdims=True)
    mx = jnp.max(mx, axis=0, keepdims=True)
    zge = jnp.sum(zge, axis=0, keepdims=True)
    zgt = jnp.sum(zgt, axis=0, keepdims=True)
    kf = jnp.float32(top_k)
    n_valid = (tq[0:1, :] + 1).astype(F32)
    all_taken = n_valid <= kf
    zero_tied = jnp.logical_and(zgt < kf, zge >= kf)
    positive = zgt >= kf

    def search_step(carry, interpolate):
        lo_, hi_, clo, chi = carry
        frac = (clo - kf + 0.5) / jnp.maximum(clo - chi, 1.0) if interpolate else 0.5
        mid = lo_ + (hi_ - lo_) * frac
        cnt = lax.fori_loop(0, n_tiles,
                            lambda c, a: a + fold(jnp.where(sc_scr[c] >= mid, 1.0, 0.0), jnp.sum),
                            jnp.zeros((8, QB), F32))
        cm = jnp.sum(cnt, axis=0, keepdims=True)
        ge = cm >= kf
        return (jnp.where(ge, mid, lo_), jnp.where(ge, hi_, mid), jnp.where(ge, cm, clo), jnp.where(ge, chi, cm))

    def round_cond(carry):
        r, _, _, clo, chi = carry
        settled = jnp.where(all_taken, 1.0, jnp.where(zero_tied, 1.0,
                            jnp.where(clo == kf, 1.0, jnp.where(chi == kf - 1.0, 1.0, 0.0))))
        return jnp.logical_and(r < SEARCH_ROUNDS, jnp.min(settled) < 1.0)

    def round_body(carry):
        st = carry[1:]
        for _ in range(SEARCH_INTERP):
            st = search_step(st, True)
        st = search_step(st, False)
        return (carry[0] + 1,) + st

    top = mx + jnp.abs(mx) * 1e-3 + 1.0
    init = (jnp.int32(0), jnp.where(positive, 0.0, mn), jnp.where(positive, top, 0.0),
            jnp.where(positive, zge, n_valid), jnp.where(positive, 0.0, zge))
    _, lo, hi, c_lo, c_hi = lax.while_loop(round_cond, round_body, init)
    exact_cut = c_lo == kf

    def thr_body(c, vm):
        sc = sc_scr[c]
        return jnp.maximum(vm, fold(jnp.where(sc < hi, sc, NEG_INF), jnp.max))

    vm = lax.fori_loop(0, n_tiles, thr_body, jnp.full((8, QB), NEG_INF, F32))
    thr = jnp.where(zero_tied, 0.0, jnp.max(vm, axis=0, keepdims=True))
    need = kf - jnp.where(zero_tied, zgt, c_hi)
    tri = jnp.where(_iota((KT, KT), 1) <= _iota((KT, KT), 0), 1.0, 0.0).astype(BF16)

    def mask_body(c, seen):
        base = c * KT
        sc = sc_scr[c]
        eq = sc == thr
        prefix = _dot(tri, jnp.where(eq, 1.0, 0.0).astype(BF16)) + seen
        tied = jnp.where(eq, jnp.where(prefix <= need, 1.0, 0.0), 0.0)
        by_thr = jnp.where(sc > thr, 1.0, tied)
        chosen = jnp.where(exact_cut, jnp.where(sc >= lo, 1.0, 0.0), by_thr)
        valid = jnp.where(key_sub + base <= tq, 1.0, 0.0)
        bias_t = (jnp.where(all_taken, valid, chosen) - 1.0) * (-NEG_INF)
        for qh in range(QB // LANES):
            for kh in range(KT // LANES):
                bias_scr[c, qh * LANES:(qh + 1) * LANES, kh * LANES:(kh + 1) * LANES] = (
                    bias_t[kh * LANES:(kh + 1) * LANES, qh * LANES:(qh + 1) * LANES].T)
        return prefix[KT - 1:KT, :]

    lax.fori_loop(0, n_tiles, mask_body, jnp.zeros((1, QB), F32))

    rows = C_HEADS * QB
    q = q_ref[0].reshape(rows, LANES)
    mx_scr[...] = jnp.full((rows, LANES), NEG_INF, F32)

    def att_pass1(c, _):
        base = pl.multiple_of(c * KT, KT)
        s = _dot_nt(q, k_ref[0, pl.ds(base, KT), :])
        s = (s.reshape(C_HEADS, QB, KT) + bias_scr[c][None]).reshape(rows, KT)
        s_scr[c] = s
        mx_scr[...] = jnp.maximum(mx_scr[...], jnp.maximum(s[:, :LANES], s[:, LANES:]))
        return 0

    lax.fori_loop(0, n_tiles, att_pass1, 0)
    m = jnp.max(mx_scr[...], axis=-1, keepdims=True)
    acc_scr[...] = jnp.zeros((rows, LANES), F32)

    def att_pass2(c, _):
        base = pl.multiple_of(c * KT, KT)
        p = jnp.exp2(s_scr[c] - m).astype(BF16)
        acc_scr[...] += _dot(p, v_ref[0, pl.ds(base, KT), :])
        return 0

    lax.fori_loop(0, n_tiles, att_pass2, 0)
    o = _normalize(acc_scr[...])
    _pair_store(o_ref, [o[h * QB:(h + 1) * QB] for h in range(C_HEADS)])


def _dsa(q, k, v, q_idx, k_idx, w_idx_t):
    bsz, _, seq, _ = q.shape
    n_qb = seq // QB
    top_k = min(DSA_TOPK, seq // 4)
    rows = C_HEADS * QB
    single = pl.BlockSpec((1, seq, LANES), lambda b, i: (b, 0, 0))
    return pl.pallas_call(
        functools.partial(_dsa_kernel, top_k=top_k),
        grid=(bsz, n_qb),
        in_specs=[pl.BlockSpec((1, C_HEADS, QB, LANES), lambda b, i: (b, 0, i, 0)),
                  single, single,
                  pl.BlockSpec((1, IDX_HEADS, QB, LANES), lambda b, i: (b, 0, i, 0)),
                  single,
                  pl.BlockSpec((1, IDX_HEADS, QB), lambda b, i: (b, 0, i))],
        out_specs=pl.BlockSpec((1, QB, C_WIDTH), lambda b, i: (b, i, 0)),
        out_shape=jax.ShapeDtypeStruct((bsz, seq, C_WIDTH), BF16),
        scratch_shapes=[pltpu.VMEM((seq // KT, KT, QB), F32),
                        pltpu.VMEM((seq // KT, QB, KT), F32),
                        pltpu.VMEM((seq // KT, rows, KT), F32),
                        pltpu.VMEM((rows, LANES), F32),
                        pltpu.VMEM((rows, LANES), F32)],
        compiler_params=_params("parallel", "arbitrary"),
        name="dsa_attention",
    )(q, k, v, q_idx, k_idx, w_idx_t)


def _moba_kernel(q_ref, k_ref, v_ref, km_ref, o_ref, s_scr, m_scr, acc_scr, *, top_k):
    i = pl.program_id(1)
    q0 = i * QB
    own = q0 // MOBA_BLOCK
    n_blk = km_ref.shape[2]
    q = q_ref[0]
    gate = _bdot_nt(km_ref[0], q) * (1.0 / QK_SCALE)
    jb = _iota(gate.shape, 1)
    past = jb < own
    gate = jnp.where(past, gate, NEG_INF)
    sel_t = jnp.where(past, _rank_lt(gate, 1, n_blk, top_k), 0.0)
    pen_t = jnp.where(jb == own, 0.0, (sel_t - 1.0) * (-NEG_INF))
    pen = _penalty_rows(pen_t.reshape(D_HEADS * n_blk, QB))
    q_ext = (q.astype(F32) + pen[None]).astype(BF16)

    m_scr[...] = jnp.full(m_scr.shape, NEG_INF, F32)
    tq = q0 + _iota((D_HEADS, QB, MOBA_BLOCK), 1)
    key = _iota((D_HEADS, QB, MOBA_BLOCK), 2)

    def scores(j, masked):
        base = pl.multiple_of(j * MOBA_BLOCK, MOBA_BLOCK)
        s = _bdot_nt(q_ext, k_ref[0, :, pl.ds(base, MOBA_BLOCK), :])
        if masked:
            s = jnp.where(key + base <= tq, s, NEG_INF)
        s_scr[j] = s
        m_scr[...] = jnp.maximum(m_scr[...], jnp.maximum(s[..., :LANES], s[..., LANES:]))

    def pass1(j, _):
        scores(j, False)
        return 0

    lax.fori_loop(0, own, pass1, 0)
    scores(own, True)
    m = jnp.max(m_scr[...], axis=-1, keepdims=True)
    acc_scr[...] = jnp.zeros(acc_scr.shape, F32)

    def pass2(j, _):
        base = pl.multiple_of(j * MOBA_BLOCK, MOBA_BLOCK)
        p = jnp.exp2(s_scr[j] - m).astype(BF16)
        acc_scr[...] += _bdot(p, v_ref[0, :, pl.ds(base, MOBA_BLOCK), :])
        return 0

    lax.fori_loop(0, own + 1, pass2, 0)
    o = _normalize(acc_scr[...])
    _pair_store(o_ref, [o[h] for h in range(D_HEADS)])


def _moba(q, k, v, k_mean):
    bsz, _, seq, _ = q.shape
    n_qb = seq // QB
    n_blk = seq // MOBA_BLOCK
    top_k = min(MOBA_TOPK, n_blk - 1)
    q_spec = pl.BlockSpec((1, D_HEADS, QB, LANES), lambda b, i: (b, 0, i, 0))
    kv_spec = pl.BlockSpec((1, D_HEADS, seq, LANES), lambda b, i: (b, 0, 0, 0))
    return pl.pallas_call(
        functools.partial(_moba_kernel, top_k=top_k),
        grid=(bsz, n_qb),
        in_specs=[q_spec, kv_spec, kv_spec,
                  pl.BlockSpec((1, D_HEADS, n_blk, LANES), lambda b, i: (b, 0, 0, 0))],
        out_specs=pl.BlockSpec((1, QB, D_WIDTH), lambda b, i: (b, i, 0)),
        out_shape=jax.ShapeDtypeStruct((bsz, seq, D_WIDTH), BF16),
        scratch_shapes=[pltpu.VMEM((n_blk, D_HEADS, QB, MOBA_BLOCK), F32),
                        pltpu.VMEM((D_HEADS, QB, LANES), F32),
                        pltpu.VMEM((D_HEADS, QB, LANES), F32)],
        compiler_params=_params("parallel", "arbitrary"),
        name="moba_attention",
    )(q, k, v, k_mean)


def _cols(w, start, size):
    return w[:, start:start + size]


def _even_mixer(x, bsz, seq, tab64, norm_g, w_in, sgu_norm, sgu_w, sgu_b,
                cmp_pos_k, cmp_w1_k, cmp_w2_k, cmp_pos_v, cmp_w1_v, cmp_w2_v, w_out):
    o_q = 2 * A_WIDTH
    o_kc = o_q + B_WIDTH
    o_vc, o_ksl, o_vsl, o_kw, o_vw = (o_kc + B_KV_WIDTH * n for n in range(1, 6))
    o_gl = o_kc + 6 * B_KV_WIDTH
    n_gate = B_HEADS * N_BRANCH
    w_a = _cols(w_in, 0, 2 * A_WIDTH)
    w_q = _cols(w_in, o_q, B_WIDTH)
    w_plain = jnp.concatenate([_cols(w_in, o, B_KV_WIDTH) for o in (o_kc, o_vc, o_vsl, o_vw)], axis=1)
    w_rope = jnp.concatenate([_cols(w_in, o, B_KV_WIDTH) for o in (o_ksl, o_kw)], axis=1)
    w_gate = jnp.pad(_cols(w_in, o_gl, n_gate), ((0, 0), (0, LANES - n_gate)))
    a_in, q_raw, q_rot, kc, vc, v_slc, v_win, k_slc, k_win, gates = _even_proj(
        x, norm_g, tab64, [w_a, w_q, w_plain, w_rope, w_gate], bsz, seq)

    a_out = _gmlp(a_in, sgu_norm, sgu_w, sgu_b)

    def cmp_rows(t):
        t = t.reshape(bsz, seq // CMP_STRIDE, CMP_STRIDE, B_KV_HEADS, HEAD_DIM)
        return t.transpose(0, 3, 1, 2, 4).reshape(bsz, B_KV_HEADS, seq // CMP_STRIDE, CMP_STRIDE * HEAD_DIM)

    k_cmp, v_cmp = _compress(cmp_rows(kc), cmp_rows(vc), cmp_pos_k, cmp_w1_k, cmp_w2_k,
                             cmp_pos_v, cmp_w1_v, cmp_w2_v)
    b_out = _nsa(q_rot, q_raw, gates.reshape(bsz, seq, LANES), k_cmp, v_cmp, k_slc, v_slc, k_win, v_win)
    return _outproj(x, a_out, b_out.reshape(bsz * seq, B_WIDTH), w_out)


def _odd_mixer(x, bsz, seq, tab64, tab32, norm_g, w_in, w_out):
    sizes = (C_WIDTH, HEAD_DIM, HEAD_DIM, IDX_HEADS * IDX_DIM, IDX_DIM, IDX_HEADS, D_WIDTH, D_WIDTH, D_WIDTH)
    offs = np.concatenate([[0], np.cumsum(sizes)])
    w_qc, w_kc, w_vc, w_qi, w_ki, w_wi, w_qd, w_kd, w_vd = (
        _cols(w_in, int(o), int(s)) for o, s in zip(offs[:-1], sizes))
    zpad = lambda n: jnp.zeros((D_MODEL, n), w_in.dtype)
    w_rope = jnp.concatenate([w_qc, w_kc, zpad(HEAD_DIM), w_qd, w_kd], axis=1)
    w_ropei = jnp.concatenate([w_qi, w_ki, zpad(OD_ROPEI - IDX_HEADS * IDX_DIM - IDX_DIM)], axis=1)
    w_plain = jnp.concatenate([w_vc, w_wi, zpad(HEAD_DIM - IDX_HEADS), w_vd], axis=1)
    qc, kc, vc, qi, ki, wi_t, qd, kd, vd, k_mean = _odd_proj(
        x, norm_g, tab64, tab32, [w_rope, w_ropei, w_plain], bsz, seq)

    c_out = _dsa(qc, kc, vc, qi, ki, wi_t)
    n_blk = seq // MOBA_BLOCK
    km = k_mean.reshape(bsz, n_blk, D_HEADS, HEAD_DIM).transpose(0, 2, 1, 3)
    km = jnp.pad(km, ((0, 0), (0, 0), (0, 0), (0, LANES - HEAD_DIM))).astype(BF16)
    d_out = _moba(qd, kd, vd, km)
    return _outproj(x, c_out.reshape(bsz * seq, C_WIDTH), d_out.reshape(bsz * seq, D_WIDTH), w_out)


def kernel(x, positions, ffn1_norm, ffn1_w_gate, ffn1_w_up, ffn1_w_down, mix_norm, ffn2_norm, ffn2_w_gate, ffn2_w_up, ffn2_w_down, ev_w_in, ev_sgu_norm, ev_sgu_w, ev_sgu_b, ev_cmp_pos_k, ev_cmp_w1_k, ev_cmp_w2_k, ev_cmp_pos_v, ev_cmp_w1_v, ev_cmp_w2_v, ev_w_out, od_w_in, od_w_out, final_norm):
    bsz, seq, _ = x.shape
    depth = ffn1_norm.shape[0]
    tab64 = _rope_tables(positions, ROT_DIM, HEAD_DIM)
    tab32 = _rope_tables(positions, IDX_ROT, IDX_DIM)
    x = x.reshape(bsz * seq, D_MODEL)
    for i in range(depth):
        x = _ffn(x, ffn1_norm[i], ffn1_w_gate[i], ffn1_w_up[i], ffn1_w_down[i])
        if i % 2 == 0:
            e = i // 2
            x = _even_mixer(x, bsz, seq, tab64, mix_norm[i], ev_w_in[e], ev_sgu_norm[e], ev_sgu_w[e],
                            ev_sgu_b[e], ev_cmp_pos_k[e], ev_cmp_w1_k[e], ev_cmp_w2_k[e],
                            ev_cmp_pos_v[e], ev_cmp_w1_v[e], ev_cmp_w2_v[e], ev_w_out[e])
        else:
            o = i // 2
            x = _odd_mixer(x, bsz, seq, tab64, tab32, mix_norm[i], od_w_in[o], od_w_out[o])
        x = _ffn(x, ffn2_norm[i], ffn2_w_gate[i], ffn2_w_up[i], ffn2_w_down[i],
                 final_g=final_norm if i == depth - 1 else None)
    return x.reshape(bsz, seq, D_MODEL)
```

```python
import functools

import numpy as np
import jax
import jax.numpy as jnp
from jax import lax
from jax.experimental import pallas as pl
from jax.experimental.pallas import tpu as pltpu

F32 = jnp.float32
BF16 = jnp.bfloat16

D_MODEL = 1024
HEAD_DIM = 64
ROT_DIM = HEAD_DIM // 4
ROPE_THETA = 500000.0
NORM_EPS = 1e-6
D_FF = 2816
NEG_INF = -1e30

A_GROUPS = 4
A_CHUNK = 128
A_WIDTH = A_GROUPS * HEAD_DIM
B_HEADS = 12
B_KV_HEADS = 3
B_GROUP = B_HEADS // B_KV_HEADS
B_WIDTH = B_HEADS * HEAD_DIM
B_KV_WIDTH = B_KV_HEADS * HEAD_DIM
CMP_LEN = 32
CMP_STRIDE = 16
CMP_HIDDEN = 256
SLC_BLOCK = 64
SLC_TOPN = 8
WINDOW = 512
N_BRANCH = 3
FORCE_SCORE = 1e4
C_HEADS = 8
C_WIDTH = C_HEADS * HEAD_DIM
IDX_HEADS = 4
IDX_DIM = 32
IDX_ROT = IDX_DIM // 4
DSA_TOPK = 256
D_HEADS = 8
D_WIDTH = D_HEADS * HEAD_DIM
MOBA_BLOCK = 256
MOBA_TOPK = 3

LANES = 128
ATT_SCALE = HEAD_DIM ** -0.5
VMEM_LIMIT = 56 * 1024 * 1024
TM = 512
FF_CHUNK = D_FF // 11
KT = 256
QB = 256
QK_SCALE = ATT_SCALE * 1.4426950408889634
SEARCH_ROUNDS = 32
SEARCH_PATTERN = ("interp", "mixed", "interp", "halve")
SEARCH_WIDE = 32.0
ONES_LANE = HEAD_DIM
AUX_LANE = HEAD_DIM


def _dot(a, b):
    return jnp.dot(a, b, preferred_element_type=F32)


def _dot_nt(a, b):
    return lax.dot_general(a, b, (((1,), (1,)), ((), ())), preferred_element_type=F32)


def _bdot_nt(a, b):
    return lax.dot_general(a, b, (((2,), (2,)), ((0,), (0,))), preferred_element_type=F32)


def _bdot(a, b):
    return lax.dot_general(a, b, (((2,), (1,)), ((0,), (0,))), preferred_element_type=F32)


def _iota(shape, dim):
    return lax.broadcasted_iota(jnp.int32, shape, dim)


def _rms(x, g):
    ms = jnp.mean(x * x, axis=-1, keepdims=True)
    return x * lax.rsqrt(ms + NORM_EPS) * g


def _gelu(x):
    return x * (0.5 * (1.0 + jnp.tanh(0.7978845608028654 * (x + 0.044715 * (x * x * x)))))


def _params(*sem):
    return pltpu.CompilerParams(dimension_semantics=sem, vmem_limit_bytes=VMEM_LIMIT)


def _full(shape):
    n = len(shape)
    return pl.BlockSpec(shape, lambda *_: (0,) * n)


def _resident(shape):
    n = len(shape)
    return pl.BlockSpec(shape, lambda *_: (0,) * n, pipeline_mode=pl.Buffered(1))


def _ffn_kernel(x_ref, g_ref, wg_ref, wu_ref, wd_ref, *rest, final, mixed):
    o_ref = rest[-1]
    x = x_ref[...]
    if mixed:
        a_ref, b_ref, wa_ref, wb_ref = rest[:4]
        x = x + (_dot(a_ref[...], wa_ref[...]) + _dot(b_ref[...], wb_ref[...]))
        rest = rest[4:]
    h = _rms(x, g_ref[...]).astype(BF16)
    acc = jnp.zeros_like(x)
    for c in range(D_FF // FF_CHUNK):
        sl = slice(c * FF_CHUNK, (c + 1) * FF_CHUNK)
        gate = _dot(h, wg_ref[:, sl])
        up = _dot(h, wu_ref[:, sl])
        act = (gate * jax.nn.sigmoid(gate) * up).astype(BF16)
        acc = acc + _dot(act, wd_ref[sl, :])
    y = x + 0.5 * acc
    if final:
        y = _rms(y, rest[0][...])
    o_ref[...] = y


def _ffn(x, g, wg, wu, wd, final_g=None, mix=None):
    t = x.shape[0]
    final = final_g is not None
    ins = [x, g.reshape(1, D_MODEL), wg.astype(BF16), wu.astype(BF16), wd.astype(BF16)]
    specs = [pl.BlockSpec((TM, D_MODEL), lambda i: (i, 0)), _full((1, D_MODEL)),
             _resident((D_MODEL, D_FF)), _resident((D_MODEL, D_FF)), _resident((D_FF, D_MODEL))]
    if mix is not None:
        a, b, w_out = mix
        na, nb = a.shape[1], b.shape[1]
        ins += [a, b, w_out[:na].astype(BF16), w_out[na:].astype(BF16)]
        specs += [pl.BlockSpec((TM, na), lambda i: (i, 0)), pl.BlockSpec((TM, nb), lambda i: (i, 0)),
                  _resident((na, D_MODEL)), _resident((nb, D_MODEL))]
    if final:
        ins.append(final_g.reshape(1, D_MODEL))
        specs.append(_full((1, D_MODEL)))
    return pl.pallas_call(
        functools.partial(_ffn_kernel, final=final, mixed=mix is not None),
        grid=(t // TM,),
        in_specs=specs,
        out_specs=pl.BlockSpec((TM, D_MODEL), lambda i: (i, 0)),
        out_shape=jax.ShapeDtypeStruct((t, D_MODEL), F32),
        compiler_params=_params("parallel"),
        name="ffn_final" if final else "ffn",
    )(*ins)


def _rope_tiles(z, tab_ref, shift, period):
    cs, sn = tab_ref[0], tab_ref[1]
    is_x1 = (_iota(cs.shape, 1) & (period - 1)) < shift
    outs = []
    for c in range(z.shape[1] // LANES):
        zt = z[:, c * LANES:(c + 1) * LANES]
        up = pltpu.roll(zt, LANES - shift, axis=1)
        dn = pltpu.roll(zt, shift, axis=1)
        outs.append(zt * cs + jnp.where(is_x1, up, dn) * sn)
    return outs[0] if len(outs) == 1 else jnp.concatenate(outs, axis=1)


def _head_row(z, col, width, aux):
    tile = z[:, (col // LANES) * LANES:(col // LANES + 1) * LANES]
    off = col % LANES
    if off:
        tile = pltpu.roll(tile, LANES - off, axis=1)
    lane = _iota(tile.shape, 1)
    return jnp.where(lane < width, tile, aux)


def _seq_pos(sblk, rows):
    return sblk * rows + _iota((rows, LANES), 0)


def _even_proj_kernel(x_ref, g_ref, t64_ref, wa_ref, wq_ref, wp_ref, wr_ref, wg_ref,
                      a_ref, qn_ref, qr_ref, kc_ref, vc_ref, vs_ref, vw_ref, ks_ref, kw_ref, gt_ref,
                      *, n_sblk):
    sblk = pl.program_id(0) % n_sblk
    h = _rms(x_ref[...], g_ref[...]).astype(BF16)
    lane = _iota((TM, LANES), 1)
    ones_col = jnp.where(lane == ONES_LANE, 1.0, 0.0)
    slc_onehot = jnp.where(lane == AUX_LANE + (_seq_pos(sblk, TM) // SLC_BLOCK), 1.0, 0.0)

    a_ref[...] = _dot(h, wa_ref[...])
    zq = _dot(h, wq_ref[...]) * QK_SCALE
    zr = _rope_tiles(zq, t64_ref, ROT_DIM // 2, HEAD_DIM)
    for hh in range(B_HEADS):
        qn_ref[0, hh] = _head_row(zq, hh * HEAD_DIM, HEAD_DIM, 0.0).astype(BF16)
        qr_ref[0, hh] = _head_row(zr, hh * HEAD_DIM, HEAD_DIM, 0.0).astype(BF16)
    zp = _dot(h, wp_ref[...])
    kc_ref[...] = zp[:, :B_KV_WIDTH]
    vc_ref[...] = zp[:, B_KV_WIDTH:2 * B_KV_WIDTH]
    zk = _rope_tiles(_dot(h, wr_ref[...]), t64_ref, ROT_DIM // 2, HEAD_DIM)
    for g in range(B_KV_HEADS):
        vs_ref[0, g] = _head_row(zp, (2 * B_KV_HEADS + g) * HEAD_DIM, HEAD_DIM, ones_col).astype(BF16)
        vw_ref[0, g] = _head_row(zp, (3 * B_KV_HEADS + g) * HEAD_DIM, HEAD_DIM, ones_col).astype(BF16)
        ks_ref[0, g] = _head_row(zk, g * HEAD_DIM, HEAD_DIM, slc_onehot).astype(BF16)
        kw_ref[0, g] = _head_row(zk, (B_KV_HEADS + g) * HEAD_DIM, HEAD_DIM, 0.0).astype(BF16)
    gt_ref[...] = jax.nn.sigmoid(_dot(h, wg_ref[...]))


def _even_proj(x, g, tab64, weights, bsz, seq):
    t = x.shape[0]
    n_sblk = seq // TM
    tok = lambda n: pl.BlockSpec((TM, n), lambda i: (i, 0))
    heads = lambda n: pl.BlockSpec((1, n, TM, LANES), lambda i: (i // n_sblk, 0, i % n_sblk, 0))
    hshape = lambda n: jax.ShapeDtypeStruct((bsz, n, seq, LANES), BF16)
    return pl.pallas_call(
        functools.partial(_even_proj_kernel, n_sblk=n_sblk),
        grid=(t // TM,),
        in_specs=[tok(D_MODEL), _full((1, D_MODEL)), pl.BlockSpec((2, TM, LANES), lambda i: (0, i, 0))]
                 + [_resident(w.shape) for w in weights],
        out_specs=[tok(2 * A_WIDTH), heads(B_HEADS), heads(B_HEADS), tok(B_KV_WIDTH), tok(B_KV_WIDTH),
                   heads(B_KV_HEADS), heads(B_KV_HEADS), heads(B_KV_HEADS), heads(B_KV_HEADS), tok(LANES)],
        out_shape=[jax.ShapeDtypeStruct((t, 2 * A_WIDTH), F32), hshape(B_HEADS), hshape(B_HEADS),
                   jax.ShapeDtypeStruct((t, B_KV_WIDTH), F32), jax.ShapeDtypeStruct((t, B_KV_WIDTH), F32),
                   hshape(B_KV_HEADS), hshape(B_KV_HEADS), hshape(B_KV_HEADS), hshape(B_KV_HEADS),
                   jax.ShapeDtypeStruct((t, LANES), F32)],
        compiler_params=_params("parallel"),
        name="even_in_proj",
    )(x, g.reshape(1, D_MODEL), tab64, *[w.astype(BF16) for w in weights])


OD_ROPE = C_WIDTH + 2 * HEAD_DIM + 2 * D_WIDTH
OD_KD = C_WIDTH + 2 * HEAD_DIM + D_WIDTH
OD_ROPEI = 2 * LANES
OD_PLAIN = 2 * HEAD_DIM + D_WIDTH


def _odd_proj_kernel(x_ref, g_ref, t64_ref, t32_ref, wr_ref, wi_ref, wp_ref,
                     qc_ref, kc_ref, vc_ref, qi_ref, ki_ref, wt_ref, qd_ref, kd_ref, vd_ref, km_ref,
                     *, n_sblk):
    sblk = pl.program_id(0) % n_sblk
    h = _rms(x_ref[...], g_ref[...]).astype(BF16)
    lane = _iota((TM, LANES), 1)
    ones_col = jnp.where(lane == ONES_LANE, 1.0, 0.0)
    blk = _seq_pos(sblk, TM) // MOBA_BLOCK

    zr = _rope_tiles(_dot(h, wr_ref[...]), t64_ref, ROT_DIM // 2, HEAD_DIM)
    for hh in range(C_HEADS):
        qc_ref[0, hh] = (_head_row(zr, hh * HEAD_DIM, HEAD_DIM, 0.0) * QK_SCALE).astype(BF16)
    kc_ref[0] = _head_row(zr, C_WIDTH, HEAD_DIM, 0.0).astype(BF16)
    for hh in range(D_HEADS):
        qd = _head_row(zr, C_WIDTH + 2 * HEAD_DIM + hh * HEAD_DIM, HEAD_DIM, 0.0)
        qd_ref[0, hh] = (qd * QK_SCALE).astype(BF16)
        onehot = jnp.where(lane == AUX_LANE + hh * (LANES - AUX_LANE) // D_HEADS + blk, 1.0, 0.0)
        kd_ref[0, hh] = _head_row(zr, OD_KD + hh * HEAD_DIM, HEAD_DIM, onehot).astype(BF16)
    zkd = zr[:, OD_KD:]
    n_mb = TM // MOBA_BLOCK
    km_ref[0] = jnp.sum(zkd.reshape(n_mb, MOBA_BLOCK, D_WIDTH), axis=1) * (1.0 / MOBA_BLOCK)

    zi = _rope_tiles(_dot(h, wi_ref[...]), t32_ref, IDX_ROT // 2, IDX_DIM)
    for hh in range(IDX_HEADS):
        qi_ref[0, hh] = _head_row(zi, hh * IDX_DIM, IDX_DIM, 0.0).astype(BF16)
    ki_ref[0] = _head_row(zi, IDX_HEADS * IDX_DIM, IDX_DIM, 0.0).astype(BF16)

    zp = _dot(h, wp_ref[...])
    vc_ref[0] = _head_row(zp, 0, HEAD_DIM, ones_col).astype(BF16)
    wt_ref[0] = pltpu.roll(zp[:, :LANES], LANES - HEAD_DIM, axis=1).T[:IDX_HEADS]
    for hh in range(D_HEADS):
        vd_ref[0, hh] = _head_row(zp, 2 * HEAD_DIM + hh * HEAD_DIM, HEAD_DIM, ones_col).astype(BF16)


def _odd_proj(x, g, tab64, tab32, weights, bsz, seq):
    t = x.shape[0]
    n_sblk = seq // TM
    tok = lambda n: pl.BlockSpec((TM, n), lambda i: (i, 0))
    tab = pl.BlockSpec((2, TM, LANES), lambda i: (0, i, 0))
    heads = lambda n: pl.BlockSpec((1, n, TM, LANES), lambda i: (i // n_sblk, 0, i % n_sblk, 0))
    single = pl.BlockSpec((1, TM, LANES), lambda i: (i // n_sblk, i % n_sblk, 0))
    hshape = lambda n: jax.ShapeDtypeStruct((bsz, n, seq, LANES), BF16)
    sshape = jax.ShapeDtypeStruct((bsz, seq, LANES), BF16)
    return pl.pallas_call(
        functools.partial(_odd_proj_kernel, n_sblk=n_sblk),
        grid=(t // TM,),
        in_specs=[tok(D_MODEL), _full((1, D_MODEL)), tab, tab] + [_resident(w.shape) for w in weights],
        out_specs=[heads(C_HEADS), single, single, heads(IDX_HEADS), single,
                   pl.BlockSpec((1, IDX_HEADS, TM), lambda i: (i // n_sblk, 0, i % n_sblk)),
                   heads(D_HEADS), heads(D_HEADS), heads(D_HEADS),
                   pl.BlockSpec((1, TM // MOBA_BLOCK, D_WIDTH), lambda i: (i, 0, 0))],
        out_shape=[hshape(C_HEADS), sshape, sshape, hshape(IDX_HEADS), sshape,
                   jax.ShapeDtypeStruct((bsz, IDX_HEADS, seq), F32),
                   hshape(D_HEADS), hshape(D_HEADS), hshape(D_HEADS),
                   jax.ShapeDtypeStruct((t // TM, TM // MOBA_BLOCK, D_WIDTH), F32)],
        compiler_params=_params("parallel"),
        name="odd_in_proj",
    )(x, g.reshape(1, D_MODEL), tab64, tab32, *[w.astype(BF16) for w in weights])


def _rope_tables(positions, rot_dim, period):
    half = rot_dim // 2
    inv_freq = ROPE_THETA ** (-jnp.arange(0, rot_dim, 2, dtype=F32) / rot_dim)
    ang = positions.astype(F32).reshape(-1, 1) * inv_freq
    cos, sin = jnp.cos(ang), jnp.sin(ang)
    t = cos.shape[0]
    cs = jnp.concatenate([cos, cos, jnp.ones((t, period - 2 * half), F32)], axis=1)
    sn = jnp.concatenate([-sin, sin, jnp.zeros((t, period - 2 * half), F32)], axis=1)
    rep = LANES // period
    return jnp.stack([jnp.tile(a, (1, rep)) for a in (cs, sn)])


def _gmlp_kernel(a_ref, n_ref, w_ref, b_ref, o_ref):
    causal = _iota((A_CHUNK, A_CHUNK), 1) <= _iota((A_CHUNK, A_CHUNK), 0)
    lane_group = _iota((A_CHUNK, A_WIDTH), 1) // HEAD_DIM
    ws = [jnp.where(causal, w_ref[g], 0.0).astype(BF16) for g in range(A_GROUPS)]
    for c in range(TM // A_CHUNK):
        rs = slice(c * A_CHUNK, (c + 1) * A_CHUNK)
        z = _gelu(a_ref[rs, :])
        u = z[:, :A_WIDTH]
        v = _rms(z[:, A_WIDTH:], n_ref[...]).astype(BF16)
        mixed = jnp.zeros((A_CHUNK, A_WIDTH), F32)
        for g in range(A_GROUPS):
            mixed = jnp.where(lane_group == g, _dot(ws[g], v) + b_ref[g], mixed)
        o_ref[rs, :] = (u * mixed).astype(BF16)


def _gmlp(a_in, sgu_norm, sgu_w, sgu_b):
    t = a_in.shape[0]
    return pl.pallas_call(
        _gmlp_kernel,
        grid=(t // TM,),
        in_specs=[pl.BlockSpec((TM, 2 * A_WIDTH), lambda i: (i, 0)),
                  _full((1, A_WIDTH)), _full((A_GROUPS, A_CHUNK, A_CHUNK)),
                  _full((A_GROUPS, A_CHUNK, 1))],
        out_specs=pl.BlockSpec((TM, A_WIDTH), lambda i: (i, 0)),
        out_shape=jax.ShapeDtypeStruct((t, A_WIDTH), BF16),
        compiler_params=_params("parallel"),
        name="gmlp",
    )(a_in, sgu_norm.reshape(1, A_WIDTH), sgu_w, sgu_b.reshape(A_GROUPS, A_CHUNK, 1))


def _compress_one(t2, pos_ref, w1_ref, w2_ref):
    a = _dot((t2 + pos_ref[0:1, :]).astype(BF16), w1_ref[0])
    b = _dot((t2 + pos_ref[1:2, :]).astype(BF16), w1_ref[1])
    hid = _gelu(a + pltpu.roll(b, t2.shape[0] - 1, axis=0))
    out = _dot(hid.astype(BF16), w2_ref[...])
    row = _iota(out.shape, 0)
    return jnp.where(row < t2.shape[0] - 1, out, 0.0).astype(BF16)


def _compress_kernel(kc_ref, vc_ref, pk_ref, w1k_ref, w2k_ref, pv_ref, w1v_ref, w2v_ref, ko_ref, vo_ref):
    for g in range(B_KV_HEADS):
        ko_ref[0, g] = _compress_one(kc_ref[0, g], pk_ref, w1k_ref, w2k_ref)
        vo_ref[0, g] = _compress_one(vc_ref[0, g], pv_ref, w1v_ref, w2v_ref)


def _compress(kc2, vc2, pos_k, w1_k, w2_k, pos_v, w1_v, w2_v):
    bsz, _, nrow, wide = kc2.shape

    def prep(pos, w1, w2):
        w2p = jnp.pad(w2, ((0, 0), (0, LANES - HEAD_DIM)))
        return (pos.reshape(2, wide), w1.reshape(2, wide, CMP_HIDDEN).astype(BF16), w2p.astype(BF16))

    pk, w1k, w2k = prep(pos_k, w1_k, w2_k)
    pv, w1v, w2v = prep(pos_v, w1_v, w2_v)
    blk_in = pl.BlockSpec((1, B_KV_HEADS, nrow, wide), lambda b: (b, 0, 0, 0))
    blk_out = pl.BlockSpec((1, B_KV_HEADS, nrow, LANES), lambda b: (b, 0, 0, 0))
    wspecs = [_full((2, wide)), _full((2, wide, CMP_HIDDEN)), _full((CMP_HIDDEN, LANES))]
    return pl.pallas_call(
        _compress_kernel,
        grid=(bsz,),
        in_specs=[blk_in, blk_in] + wspecs + wspecs,
        out_specs=[blk_out, blk_out],
        out_shape=[jax.ShapeDtypeStruct((bsz, B_KV_HEADS, nrow, LANES), BF16)] * 2,
        compiler_params=_params("parallel"),
        name="nsa_compress",
    )(kc2, vc2, pk, w1k, w2k, pv, w1v, w2v)


def _rank_lt(vals, axis, n, k):
    j = _iota(vals.shape, axis)
    rank = jnp.zeros(vals.shape, F32)
    for jp in range(n):
        row = lax.slice_in_dim(vals, jp, jp + 1, axis=axis)
        beats = jnp.where(row > vals, 1.0, jnp.where(row == vals, jnp.where(j > jp, 1.0, 0.0), 0.0))
        rank = rank + beats
    return jnp.where(rank < k, 1.0, 0.0)


def _penalty_rows(pen_t):
    n, nq = pen_t.shape
    parts = [jnp.zeros((AUX_LANE, nq), F32), pen_t]
    if LANES - AUX_LANE - n:
        parts.append(jnp.zeros((LANES - AUX_LANE - n, nq), F32))
    full = jnp.concatenate(parts, axis=0)
    halves = [full[:, c * LANES:(c + 1) * LANES].T for c in range(nq // LANES)]
    return halves[0] if len(halves) == 1 else jnp.concatenate(halves, axis=0)


def _normalize(acc):
    den = lax.slice_in_dim(acc, ONES_LANE, ONES_LANE + 1, axis=acc.ndim - 1)
    return acc / den


def _pair_store(o_ref, heads_out):
    lane = _iota((QB, LANES), 1)
    for p in range(len(heads_out) // 2):
        both = jnp.where(lane < HEAD_DIM, heads_out[2 * p], pltpu.roll(heads_out[2 * p + 1], HEAD_DIM, axis=1))
        o_ref[0, :, p * LANES:(p + 1) * LANES] = both.astype(BF16)


def _nsa_kernel(qr_ref, qn_ref, gt_ref, kc_ref, vc_ref, ks_ref, vs_ref, kw_ref, vw_ref, ovt_ref,
                o_ref, s_scr, q_scr, cmp_scr, m_scr, acc_scr):
    i = pl.program_id(1)
    q0 = i * QB
    rows = B_GROUP * QB
    n_slc = ovt_ref.shape[0]
    n_g = B_KV_HEADS
    gates = gt_ref[0]
    tq2 = q0 + _iota((QB, KT), 0)
    key2 = _iota((QB, KT), 1)

    def masked(s, ok):
        n = s.shape[1]
        return jnp.where(ok[None], s.reshape(B_GROUP, QB, n), NEG_INF).reshape(rows, n)

    m_c = ((_iota((QB, LANES), 1) * CMP_STRIDE + (CMP_LEN - 1)) <= q0 + _iota((QB, LANES), 0))[None]
    for g in range(n_g):
        hs = slice(g * B_GROUP, (g + 1) * B_GROUP)
        qn = qn_ref[0, hs].reshape(rows, LANES)
        s_c = _dot_nt(qn, kc_ref[0, g]).reshape(B_GROUP, QB, LANES)
        sm = jnp.where(m_c, s_c, NEG_INF)
        e = jnp.where(m_c, jnp.exp2(sm - jnp.max(sm, axis=-1, keepdims=True)), 0.0)
        den = jnp.sum(e, axis=-1, keepdims=True)
        p_cb = (e / jnp.where(den > 0.0, den, 1.0)).reshape(rows, LANES).astype(BF16)
        cmp_scr[g] = _dot(p_cb, vc_ref[0, g])

        imp = jnp.zeros((n_slc, QB), F32)
        for r in range(B_GROUP):
            imp = imp + _dot_nt(ovt_ref[...], p_cb[r * QB:(r + 1) * QB])
        jb = _iota((n_slc, QB), 0)
        tq = q0 + _iota((n_slc, QB), 1)
        forced = (jb == 0) | (jb == (tq >> 6))
        imp = jnp.where(jb * SLC_BLOCK <= tq, jnp.where(forced, FORCE_SCORE, imp), NEG_INF)
        pen = _penalty_rows((_rank_lt(imp, 0, n_slc, SLC_TOPN) - 1.0) * (-NEG_INF))
        qr = qr_ref[0, hs].astype(F32) + pen[None]
        q_scr[g] = qr.reshape(rows, LANES).astype(BF16)

    n_wt = (WINDOW + QB + KT - 1) // KT
    w_off = [QB - (n_wt - c) * KT for c in range(n_wt)]
    w_base = [pl.multiple_of(jnp.maximum(q0 + off, 0), min(QB, KT)) for off in w_off]

    def window_scores(interior):
        w_mx = [jnp.full((rows, LANES), NEG_INF, F32) for _ in range(n_g)]
        for c in range(n_wt):
            key = key2 + w_base[c]
            ok = None
            if not interior:
                ok = (key > tq2 - WINDOW) & (key <= jnp.minimum(tq2, q0 + (w_off[c] + KT - 1)))
            elif w_off[c] < QB - WINDOW and w_off[c] + KT - 1 > 0:
                ok = (key > tq2 - WINDOW) & (key <= tq2)
            elif w_off[c] < QB - WINDOW:
                ok = key > tq2 - WINDOW
            elif w_off[c] + KT - 1 > 0:
                ok = key <= tq2
            for g in range(n_g):
                s = _dot_nt(q_scr[g], kw_ref[0, g, pl.ds(w_base[c], KT), :])
                if ok is not None:
                    s = masked(s, ok)
                s_scr[c, g] = s
                w_mx[g] = jnp.maximum(w_mx[g], jnp.maximum(s[:, :LANES], s[:, LANES:]))
        for g in range(n_g):
            m_scr[g] = w_mx[g]

    first_interior = (n_wt * KT - QB + QB - 1) // QB
    pl.when(i >= first_interior)(functools.partial(window_scores, True))
    pl.when(i < first_interior)(functools.partial(window_scores, False))
    for g in range(n_g):
        w_max = jnp.max(m_scr[g], axis=-1, keepdims=True)
        acc_w = jnp.zeros((rows, LANES), F32)
        for c in range(n_wt):
            p = jnp.exp2(s_scr[c, g] - w_max).astype(BF16)
            acc_w = acc_w + _dot(p, vw_ref[0, g, pl.ds(w_base[c], KT), :])
        acc_scr[g] = _normalize(acc_w)

    n_tiles = (q0 + QB + KT - 1) // KT
    last = n_tiles - 1
    for g in range(n_g):
        m_scr[g] = jnp.full((rows, LANES), NEG_INF, F32)

    def sel_scores(c, causal):
        base = pl.multiple_of(c * KT, KT)
        for g in range(n_g):
            s = _dot_nt(q_scr[g], ks_ref[0, g, pl.ds(base, KT), :])
            if causal:
                s = masked(s, key2 + base <= tq2)
            s_scr[c, g] = s
            m_scr[g] = jnp.maximum(m_scr[g], jnp.maximum(s[:, :LANES], s[:, LANES:]))

    def sel_pass1(c, _):
        sel_scores(c, False)
        return 0

    lax.fori_loop(0, last, sel_pass1, 0)
    sel_scores(last, True)

    m_sel = [jnp.max(m_scr[g], axis=-1, keepdims=True) for g in range(n_g)]
    for g in range(n_g):
        m_scr[g] = jnp.zeros((rows, LANES), F32)

    def sel_pass2(c, _):
        base = pl.multiple_of(c * KT, KT)
        for g in range(n_g):
            p = jnp.exp2(s_scr[c, g] - m_sel[g]).astype(BF16)
            m_scr[g] += _dot(p, vs_ref[0, g, pl.ds(base, KT), :])
        return 0

    lax.fori_loop(0, n_tiles, sel_pass2, 0)

    outs = []
    for g in range(n_g):
        for r in range(B_GROUP):
            h = g * B_GROUP + r
            rs = slice(r * QB, (r + 1) * QB)
            c0 = h * N_BRANCH
            outs.append(gates[:, c0:c0 + 1] * cmp_scr[g, rs, :] + gates[:, c0 + 1:c0 + 2] * _normalize(m_scr[g, rs, :])
                        + gates[:, c0 + 2:c0 + 3] * acc_scr[g, rs, :])
    _pair_store(o_ref, outs)


def _overlap_t(n_rows, n_slc):
    n = np.arange(n_rows)
    c0 = n * CMP_STRIDE
    s0 = np.arange(n_slc) * SLC_BLOCK
    m = (c0[None, :] < s0[:, None] + SLC_BLOCK) & (c0[None, :] + CMP_LEN > s0[:, None])
    m = m & (n[None, :] < n_rows - 1)
    return jnp.asarray(m, dtype=BF16)


def _nsa(q_rot, q_raw, gates, k_cmp, v_cmp, k_slc, v_slc, k_win, v_win):
    bsz, _, seq, _ = q_rot.shape
    n_qb = seq // QB
    n_slc = seq // SLC_BLOCK
    n_cmp_rows = k_cmp.shape[2]
    ovt = _overlap_t(n_cmp_rows, n_slc)
    q_spec = pl.BlockSpec((1, B_HEADS, QB, LANES), lambda b, i: (b, 0, i, 0))
    cmp_spec = pl.BlockSpec((1, B_KV_HEADS, n_cmp_rows, LANES), lambda b, i: (b, 0, 0, 0))
    kv_spec = pl.BlockSpec((1, B_KV_HEADS, seq, LANES), lambda b, i: (b, 0, 0, 0), pipeline_mode=pl.Buffered(1))
    rows = B_GROUP * QB
    return pl.pallas_call(
        _nsa_kernel,
        grid=(bsz, n_qb),
        in_specs=[q_spec, q_spec, pl.BlockSpec((1, QB, LANES), lambda b, i: (b, i, 0)),
                  cmp_spec, cmp_spec, kv_spec, kv_spec, kv_spec, kv_spec, _full(ovt.shape)],
        out_specs=pl.BlockSpec((1, QB, B_WIDTH), lambda b, i: (b, i, 0)),
        out_shape=jax.ShapeDtypeStruct((bsz, seq, B_WIDTH), BF16),
        scratch_shapes=[pltpu.VMEM((seq // KT, B_KV_HEADS, rows, KT), F32),
                        pltpu.VMEM((B_KV_HEADS, rows, LANES), BF16),
                        pltpu.VMEM((B_KV_HEADS, rows, LANES), F32),
                        pltpu.VMEM((B_KV_HEADS, rows, LANES), F32),
                        pltpu.VMEM((B_KV_HEADS, rows, LANES), F32)],
        compiler_params=_params("parallel", "arbitrary"),
        name="nsa_attention",
    )(q_rot, q_raw, gates, k_cmp, v_cmp, k_slc, v_slc, k_win, v_win, ovt)


def _dsa_kernel(q_ref, k_ref, v_ref, qi_ref, ki_ref, wi_ref, o_ref,
                sc_scr, bias_scr, s_scr, mx_scr, acc_scr, *, top_k):
    i = pl.program_id(1)
    q0 = i * QB
    n_tiles = (q0 + QB + KT - 1) // KT
    tile = (KT, QB)
    key_sub = _iota(tile, 0)
    tq = q0 + _iota(tile, 1)
    w_idx = wi_ref[0]
    qi = qi_ref[0].reshape(IDX_HEADS * QB, LANES)
    n_sub = KT // 8

    def fold(x, op):
        return op(op(x.reshape(n_sub // 4, 4, 8, QB), axis=0), axis=0)

    w_s = w_idx * (IDX_DIM ** -0.5 * IDX_HEADS ** -0.5)
    last = n_tiles - 1

    def score_tile(c, carry, masked):
        mn, mx, zge, zgt = carry
        base = pl.multiple_of(c * KT, KT)
        lg = _dot_nt(ki_ref[0, pl.ds(base, KT), :], qi)
        sc = jnp.zeros(tile, F32)
        for h in range(IDX_HEADS):
            sc = sc + w_s[h:h + 1, :] * jnp.maximum(lg[:, h * QB:(h + 1) * QB], 0.0)
        if masked:
            valid = key_sub + base <= tq
            lowest = jnp.where(valid, sc, -NEG_INF)
            sc = jnp.where(valid, sc, NEG_INF)
        else:
            lowest = sc
        sc_scr[c] = sc
        return (jnp.minimum(mn, fold(lowest, jnp.min)), jnp.maximum(mx, fold(sc, jnp.max)),
                zge + fold(jnp.where(sc >= 0.0, 1.0, 0.0), jnp.sum),
                zgt + fold(jnp.where(sc > 0.0, 1.0, 0.0), jnp.sum))

    stats = lax.fori_loop(0, last, lambda c, carry: score_tile(c, carry, False),
                          (jnp.full((8, QB), -NEG_INF, F32), jnp.full((8, QB), NEG_INF, F32),
                           jnp.zeros((8, QB), F32), jnp.zeros((8, QB), F32)))
    mn, mx, zge, zgt = score_tile(last, stats, True)
    mn = jnp.min(mn, axis=0, keepdims=True)
    mx = jnp.max(mx, axis=0, keepdims=True)
    zge = jnp.sum(zge, axis=0, keepdims=True)
    zgt = jnp.sum(zgt, axis=0, keepdims=True)
    kf = jnp.float32(top_k)
    n_valid = (tq[0:1, :] + 1).astype(F32)
    all_taken = n_valid <= kf
    zero_tied = jnp.logical_and(zgt < kf, zge >= kf)
    positive = zgt >= kf

    def search_step(carry, kind):
        lo_, hi_, clo, chi = carry
        if kind == "halve":
            frac = 0.5
        else:
            frac = jnp.clip((clo - kf + 0.5) / jnp.maximum(clo - chi, 1.0), 1.0 / 64, 63.0 / 64)
            if kind == "mixed":
                frac = jnp.where(clo - chi > SEARCH_WIDE, 0.5, frac)
        mid = lo_ + (hi_ - lo_) * frac
        cnt = lax.fori_loop(0, n_tiles,
                            lambda c, a: a + fold(jnp.where(sc_scr[c] >= mid, 1.0, 0.0), jnp.sum),
                            jnp.zeros((8, QB), F32))
        cm = jnp.sum(cnt, axis=0, keepdims=True)
        ge = cm >= kf
        return (jnp.where(ge, mid, lo_), jnp.where(ge, hi_, mid), jnp.where(ge, cm, clo), jnp.where(ge, chi, cm))

    def round_cond(carry):
        r, _, _, clo, chi = carry
        settled = jnp.where(all_taken, 1.0, jnp.where(zero_tied, 1.0,
                            jnp.where(clo == kf, 1.0, jnp.where(chi == kf - 1.0, 1.0, 0.0))))
        return jnp.logical_and(r < SEARCH_ROUNDS, jnp.min(settled) < 1.0)

    def round_body(carry):
        st = carry[1:]
        for kind in SEARCH_PATTERN:
            st = search_step(st, kind)
        return (carry[0] + 1,) + st

    top = mx + jnp.abs(mx) * 1e-3 + 1.0
    init = (jnp.int32(0), jnp.where(positive, 0.0, mn), jnp.where(positive, top, 0.0),
            jnp.where(positive, zge, n_valid), jnp.where(positive, 0.0, zge))
    _, lo, hi, c_lo, c_hi = lax.while_loop(round_cond, round_body, init)
    exact_cut = c_lo == kf

    def thr_body(c, vm):
        sc = sc_scr[c]
        return jnp.maximum(vm, fold(jnp.where(sc < hi, sc, NEG_INF), jnp.max))

    vm = lax.fori_loop(0, n_tiles, thr_body, jnp.full((8, QB), NEG_INF, F32))
    thr = jnp.where(zero_tied, 0.0, jnp.max(vm, axis=0, keepdims=True))
    need = kf - jnp.where(zero_tied, zgt, c_hi)
    tri = jnp.where(_iota((KT, KT), 1) <= _iota((KT, KT), 0), 1.0, 0.0).astype(BF16)

    def mask_body(c, seen):
        base = c * KT
        sc = sc_scr[c]
        eq = sc == thr
        prefix = _dot(tri, jnp.where(eq, 1.0, 0.0).astype(BF16)) + seen
        tied = jnp.where(eq, jnp.where(prefix <= need, 1.0, 0.0), 0.0)
        by_thr = jnp.where(sc > thr, 1.0, tied)
        chosen = jnp.where(exact_cut, jnp.where(sc >= lo, 1.0, 0.0), by_thr)
        valid = jnp.where(key_sub + base <= tq, 1.0, 0.0)
        bias_t = (jnp.where(all_taken, valid, chosen) - 1.0) * (-NEG_INF)
        for qh in range(QB // LANES):
            for kh in range(KT // LANES):
                bias_scr[c, qh * LANES:(qh + 1) * LANES, kh * LANES:(kh + 1) * LANES] = (
                    bias_t[kh * LANES:(kh + 1) * LANES, qh * LANES:(qh + 1) * LANES].T)
        return prefix[KT - 1:KT, :]

    lax.fori_loop(0, n_tiles, mask_body, jnp.zeros((1, QB), F32))

    rows = C_HEADS * QB
    q = q_ref[0].reshape(rows, LANES)
    mx_scr[...] = jnp.full((rows, LANES), NEG_INF, F32)

    def att_pass1(c, _):
        base = pl.multiple_of(c * KT, KT)
        s = _dot_nt(q, k_ref[0, pl.ds(base, KT), :])
        s = (s.reshape(C_HEADS, QB, KT) + bias_scr[c][None]).reshape(rows, KT)
        s_scr[c] = s
        mx_scr[...] = jnp.maximum(mx_scr[...], jnp.maximum(s[:, :LANES], s[:, LANES:]))
        return 0

    lax.fori_loop(0, n_tiles, att_pass1, 0)
    m = jnp.max(mx_scr[...], axis=-1, keepdims=True)
    acc_scr[...] = jnp.zeros((rows, LANES), F32)

    def att_pass2(c, _):
        base = pl.multiple_of(c * KT, KT)
        p = jnp.exp2(s_scr[c] - m).astype(BF16)
        acc_scr[...] += _dot(p, v_ref[0, pl.ds(base, KT), :])
        return 0

    lax.fori_loop(0, n_tiles, att_pass2, 0)
    o = _normalize(acc_scr[...])
    _pair_store(o_ref, [o[h * QB:(h + 1) * QB] for h in range(C_HEADS)])


def _dsa(q, k, v, q_idx, k_idx, w_idx_t):
    bsz, _, seq, _ = q.shape
    n_qb = seq // QB
    top_k = min(DSA_TOPK, seq // 4)
    rows = C_HEADS * QB
    single = pl.BlockSpec((1, seq, LANES), lambda b, i: (b, 0, 0))
    return pl.pallas_call(
        functools.partial(_dsa_kernel, top_k=top_k),
        grid=(bsz, n_qb),
        in_specs=[pl.BlockSpec((1, C_HEADS, QB, LANES), lambda b, i: (b, 0, i, 0)),
                  single, single,
                  pl.BlockSpec((1, IDX_HEADS, QB, LANES), lambda b, i: (b, 0, i, 0)),
                  single,
                  pl.BlockSpec((1, IDX_HEADS, QB), lambda b, i: (b, 0, i))],
        out_specs=pl.BlockSpec((1, QB, C_WIDTH), lambda b, i: (b, i, 0)),
        out_shape=jax.ShapeDtypeStruct((bsz, seq, C_WIDTH), BF16),
        scratch_shapes=[pltpu.VMEM((seq // KT, KT, QB), F32),
                        pltpu.VMEM((seq // KT, QB, KT), F32),
                        pltpu.VMEM((seq // KT, rows, KT), F32),
                        pltpu.VMEM((rows, LANES), F32),
                        pltpu.VMEM((rows, LANES), F32)],
        compiler_params=_params("parallel", "arbitrary"),
        name="dsa_attention",
    )(q, k, v, q_idx, k_idx, w_idx_t)


def _moba_kernel(q_ref, k_ref, v_ref, km_ref, o_ref, s_scr, m_scr, acc_scr, *, top_k):
    i = pl.program_id(1)
    q0 = i * QB
    own = q0 // MOBA_BLOCK
    n_blk = km_ref.shape[2]
    q = q_ref[0]
    gate = _bdot_nt(km_ref[0], q) * (1.0 / QK_SCALE)
    jb = _iota(gate.shape, 1)
    past = jb < own
    gate = jnp.where(past, gate, NEG_INF)
    sel_t = jnp.where(past, _rank_lt(gate, 1, n_blk, top_k), 0.0)
    pen_t = jnp.where(jb == own, 0.0, (sel_t - 1.0) * (-NEG_INF))
    pen = _penalty_rows(pen_t.reshape(D_HEADS * n_blk, QB))
    q_ext = (q.astype(F32) + pen[None]).astype(BF16)

    m_scr[...] = jnp.full(m_scr.shape, NEG_INF, F32)
    tq = q0 + _iota((QB, MOBA_BLOCK), 0)
    key = _iota((QB, MOBA_BLOCK), 1)

    def scores(j, masked):
        base = pl.multiple_of(j * MOBA_BLOCK, MOBA_BLOCK)
        s = _bdot_nt(q_ext, k_ref[0, :, pl.ds(base, MOBA_BLOCK), :])
        if masked:
            s = jnp.where((key + base <= tq)[None], s, NEG_INF)
        s_scr[j] = s
        m_scr[...] = jnp.maximum(m_scr[...], jnp.maximum(s[..., :LANES], s[..., LANES:]))

    def pass1(j, _):
        scores(j, False)
        return 0

    lax.fori_loop(0, own, pass1, 0)
    scores(own, True)
    m = jnp.max(m_scr[...], axis=-1, keepdims=True)
    acc_scr[...] = jnp.zeros(acc_scr.shape, F32)

    def pass2(j, _):
        base = pl.multiple_of(j * MOBA_BLOCK, MOBA_BLOCK)
        p = jnp.exp2(s_scr[j] - m).astype(BF16)
        acc_scr[...] += _bdot(p, v_ref[0, :, pl.ds(base, MOBA_BLOCK), :])
        return 0

    lax.fori_loop(0, own + 1, pass2, 0)
    o = _normalize(acc_scr[...])
    _pair_store(o_ref, [o[h] for h in range(D_HEADS)])


def _moba(q, k, v, k_mean):
    bsz, _, seq, _ = q.shape
    n_qb = seq // QB
    n_blk = seq // MOBA_BLOCK
    top_k = min(MOBA_TOPK, n_blk - 1)
    q_spec = pl.BlockSpec((1, D_HEADS, QB, LANES), lambda b, i: (b, 0, i, 0))
    kv_spec = pl.BlockSpec((1, D_HEADS, seq, LANES), lambda b, i: (b, 0, 0, 0))
    return pl.pallas_call(
        functools.partial(_moba_kernel, top_k=top_k),
        grid=(bsz, n_qb),
        in_specs=[q_spec, kv_spec, kv_spec,
                  pl.BlockSpec((1, D_HEADS, n_blk, LANES), lambda b, i: (b, 0, 0, 0))],
        out_specs=pl.BlockSpec((1, QB, D_WIDTH), lambda b, i: (b, i, 0)),
        out_shape=jax.ShapeDtypeStruct((bsz, seq, D_WIDTH), BF16),
        scratch_shapes=[pltpu.VMEM((n_blk, D_HEADS, QB, MOBA_BLOCK), F32),
                        pltpu.VMEM((D_HEADS, QB, LANES), F32),
                        pltpu.VMEM((D_HEADS, QB, LANES), F32)],
        compiler_params=_params("parallel", "arbitrary"),
        name="moba_attention",
    )(q, k, v, k_mean)


def _cols(w, start, size):
    return w[:, start:start + size]


def _even_mixer(x, bsz, seq, tab64, norm_g, w_in, sgu_norm, sgu_w, sgu_b,
                cmp_pos_k, cmp_w1_k, cmp_w2_k, cmp_pos_v, cmp_w1_v, cmp_w2_v, w_out):
    o_q = 2 * A_WIDTH
    o_kc = o_q + B_WIDTH
    o_vc, o_ksl, o_vsl, o_kw, o_vw = (o_kc + B_KV_WIDTH * n for n in range(1, 6))
    o_gl = o_kc + 6 * B_KV_WIDTH
    n_gate = B_HEADS * N_BRANCH
    w_a = _cols(w_in, 0, 2 * A_WIDTH)
    w_q = _cols(w_in, o_q, B_WIDTH)
    w_plain = jnp.concatenate([_cols(w_in, o, B_KV_WIDTH) for o in (o_kc, o_vc, o_vsl, o_vw)], axis=1)
    w_rope = jnp.concatenate([_cols(w_in, o, B_KV_WIDTH) for o in (o_ksl, o_kw)], axis=1)
    w_gate = jnp.pad(_cols(w_in, o_gl, n_gate), ((0, 0), (0, LANES - n_gate)))
    a_in, q_raw, q_rot, kc, vc, v_slc, v_win, k_slc, k_win, gates = _even_proj(
        x, norm_g, tab64, [w_a, w_q, w_plain, w_rope, w_gate], bsz, seq)

    a_out = _gmlp(a_in, sgu_norm, sgu_w, sgu_b)

    def cmp_rows(t):
        t = t.reshape(bsz, seq // CMP_STRIDE, CMP_STRIDE, B_KV_HEADS, HEAD_DIM)
        return t.transpose(0, 3, 1, 2, 4).reshape(bsz, B_KV_HEADS, seq // CMP_STRIDE, CMP_STRIDE * HEAD_DIM)

    k_cmp, v_cmp = _compress(cmp_rows(kc), cmp_rows(vc), cmp_pos_k, cmp_w1_k, cmp_w2_k,
                             cmp_pos_v, cmp_w1_v, cmp_w2_v)
    b_out = _nsa(q_rot, q_raw, gates.reshape(bsz, seq, LANES), k_cmp, v_cmp, k_slc, v_slc, k_win, v_win)
    return a_out, b_out.reshape(bsz * seq, B_WIDTH), w_out


def _odd_mixer(x, bsz, seq, tab64, tab32, norm_g, w_in, w_out):
    sizes = (C_WIDTH, HEAD_DIM, HEAD_DIM, IDX_HEADS * IDX_DIM, IDX_DIM, IDX_HEADS, D_WIDTH, D_WIDTH, D_WIDTH)
    offs = np.concatenate([[0], np.cumsum(sizes)])
    w_qc, w_kc, w_vc, w_qi, w_ki, w_wi, w_qd, w_kd, w_vd = (
        _cols(w_in, int(o), int(s)) for o, s in zip(offs[:-1], sizes))
    zpad = lambda n: jnp.zeros((D_MODEL, n), w_in.dtype)
    w_rope = jnp.concatenate([w_qc, w_kc, zpad(HEAD_DIM), w_qd, w_kd], axis=1)
    w_ropei = jnp.concatenate([w_qi, w_ki, zpad(OD_ROPEI - IDX_HEADS * IDX_DIM - IDX_DIM)], axis=1)
    w_plain = jnp.concatenate([w_vc, w_wi, zpad(HEAD_DIM - IDX_HEADS), w_vd], axis=1)
    qc, kc, vc, qi, ki, wi_t, qd, kd, vd, k_mean = _odd_proj(
        x, norm_g, tab64, tab32, [w_rope, w_ropei, w_plain], bsz, seq)

    c_out = _dsa(qc, kc, vc, qi, ki, wi_t)
    n_blk = seq // MOBA_BLOCK
    km = k_mean.reshape(bsz, n_blk, D_HEADS, HEAD_DIM).transpose(0, 2, 1, 3)
    km = jnp.pad(km, ((0, 0), (0, 0), (0, 0), (0, LANES - HEAD_DIM))).astype(BF16)
    d_out = _moba(qd, kd, vd, km)
    return c_out.reshape(bsz * seq, C_WIDTH), d_out.reshape(bsz * seq, D_WIDTH), w_out


def kernel(x, positions, ffn1_norm, ffn1_w_gate, ffn1_w_up, ffn1_w_down, mix_norm, ffn2_norm, ffn2_w_gate, ffn2_w_up, ffn2_w_down, ev_w_in, ev_sgu_norm, ev_sgu_w, ev_sgu_b, ev_cmp_pos_k, ev_cmp_w1_k, ev_cmp_w2_k, ev_cmp_pos_v, ev_cmp_w1_v, ev_cmp_w2_v, ev_w_out, od_w_in, od_w_out, final_norm):
    bsz, seq, _ = x.shape
    depth = ffn1_norm.shape[0]
    tab64 = _rope_tables(positions, ROT_DIM, HEAD_DIM)
    tab32 = _rope_tables(positions, IDX_ROT, IDX_DIM)
    x = x.reshape(bsz * seq, D_MODEL)
    for i in range(depth):
        x = _ffn(x, ffn1_norm[i], ffn1_w_gate[i], ffn1_w_up[i], ffn1_w_down[i])
        if i % 2 == 0:
            e = i // 2
            mix = _even_mixer(x, bsz, seq, tab64, mix_norm[i], ev_w_in[e], ev_sgu_norm[e], ev_sgu_w[e],
                              ev_sgu_b[e], ev_cmp_pos_k[e], ev_cmp_w1_k[e], ev_cmp_w2_k[e],
                              ev_cmp_pos_v[e], ev_cmp_w1_v[e], ev_cmp_w2_v[e], ev_w_out[e])
        else:
            o = i // 2
            mix = _odd_mixer(x, bsz, seq, tab64, tab32, mix_norm[i], od_w_in[o], od_w_out[o])
        x = _ffn(x, ffn2_norm[i], ffn2_w_gate[i], ffn2_w_up[i], ffn2_w_down[i],
                 final_g=final_norm if i == depth - 1 else None, mix=mix)
    return x.reshape(bsz, seq, D_MODEL)
```

```python
import functools

import numpy as np
import jax
import jax.numpy as jnp
from jax import lax
from jax.experimental import pallas as pl
from jax.experimental.pallas import tpu as pltpu

F32 = jnp.float32
BF16 = jnp.bfloat16

D_MODEL = 1024
HEAD_DIM = 64
ROT_DIM = HEAD_DIM // 4
ROPE_THETA = 500000.0
NORM_EPS = 1e-6
D_FF = 2816
NEG_INF = -1e30

A_GROUPS = 4
A_CHUNK = 128
A_WIDTH = A_GROUPS * HEAD_DIM
B_HEADS = 12
B_KV_HEADS = 3
B_GROUP = B_HEADS // B_KV_HEADS
B_WIDTH = B_HEADS * HEAD_DIM
B_KV_WIDTH = B_KV_HEADS * HEAD_DIM
CMP_LEN = 32
CMP_STRIDE = 16
CMP_HIDDEN = 256
SLC_BLOCK = 64
SLC_TOPN = 8
WINDOW = 512
N_BRANCH = 3
FORCE_SCORE = 1e4
C_HEADS = 8
C_WIDTH = C_HEADS * HEAD_DIM
IDX_HEADS = 4
IDX_DIM = 32
IDX_ROT = IDX_DIM // 4
DSA_TOPK = 256
D_HEADS = 8
D_WIDTH = D_HEADS * HEAD_DIM
MOBA_BLOCK = 256
MOBA_TOPK = 3

LANES = 128
ATT_SCALE = HEAD_DIM ** -0.5
VMEM_LIMIT = 56 * 1024 * 1024
TM = 512
FF_CHUNK = D_FF // 11
KT = 256
QB = 256
QK_SCALE = ATT_SCALE * 1.4426950408889634
SEARCH_ROUNDS = 32
SEARCH_PATTERN = ("interp", "mixed", "interp", "halve")
SEARCH_WIDE = 32.0
AUX_LANE = HEAD_DIM


def _dot(a, b):
    return jnp.dot(a, b, preferred_element_type=F32)


def _dot_nt(a, b):
    return lax.dot_general(a, b, (((1,), (1,)), ((), ())), preferred_element_type=F32)


def _bdot_nt(a, b):
    return lax.dot_general(a, b, (((2,), (2,)), ((0,), (0,))), preferred_element_type=F32)


def _bdot(a, b):
    return lax.dot_general(a, b, (((2,), (1,)), ((0,), (0,))), preferred_element_type=F32)


def _iota(shape, dim):
    return lax.broadcasted_iota(jnp.int32, shape, dim)


def _rms(x, g):
    ms = jnp.mean(x * x, axis=-1, keepdims=True)
    return x * lax.rsqrt(ms + NORM_EPS) * g


def _gelu(x):
    return x * (0.5 * (1.0 + jnp.tanh(0.7978845608028654 * (x + 0.044715 * (x * x * x)))))


def _params(*sem):
    return pltpu.CompilerParams(dimension_semantics=sem, vmem_limit_bytes=VMEM_LIMIT)


def _full(shape):
    n = len(shape)
    return pl.BlockSpec(shape, lambda *_: (0,) * n)


def _resident(shape):
    n = len(shape)
    return pl.BlockSpec(shape, lambda *_: (0,) * n, pipeline_mode=pl.Buffered(1))


def _ffn_kernel(x_ref, g_ref, wg_ref, wu_ref, wd_ref, *rest, final, mixed):
    o_ref = rest[-1]
    x = x_ref[...]
    if mixed:
        a_ref, b_ref, wa_ref, wb_ref = rest[:4]
        x = x + (_dot(a_ref[...], wa_ref[...]) + _dot(b_ref[...], wb_ref[...]))
        rest = rest[4:]
    h = _rms(x, g_ref[...]).astype(BF16)
    acc = jnp.zeros_like(x)
    for c in range(D_FF // FF_CHUNK):
        sl = slice(c * FF_CHUNK, (c + 1) * FF_CHUNK)
        gate = _dot(h, wg_ref[:, sl])
        up = _dot(h, wu_ref[:, sl])
        act = (gate * jax.nn.sigmoid(gate) * up).astype(BF16)
        acc = acc + _dot(act, wd_ref[sl, :])
    y = x + 0.5 * acc
    if final:
        y = _rms(y, rest[0][...])
    o_ref[...] = y


def _ffn(x, g, wg, wu, wd, final_g=None, mix=None):
    t = x.shape[0]
    final = final_g is not None
    ins = [x, g.reshape(1, D_MODEL), wg.astype(BF16), wu.astype(BF16), wd.astype(BF16)]
    specs = [pl.BlockSpec((TM, D_MODEL), lambda i: (i, 0)), _full((1, D_MODEL)),
             _resident((D_MODEL, D_FF)), _resident((D_MODEL, D_FF)), _resident((D_FF, D_MODEL))]
    if mix is not None:
        a, b, w_out = mix
        na, nb = a.shape[1], b.shape[1]
        ins += [a, b, w_out[:na].astype(BF16), w_out[na:].astype(BF16)]
        specs += [pl.BlockSpec((TM, na), lambda i: (i, 0)), pl.BlockSpec((TM, nb), lambda i: (i, 0)),
                  _resident((na, D_MODEL)), _resident((nb, D_MODEL))]
    if final:
        ins.append(final_g.reshape(1, D_MODEL))
        specs.append(_full((1, D_MODEL)))
    return pl.pallas_call(
        functools.partial(_ffn_kernel, final=final, mixed=mix is not None),
        grid=(t // TM,),
        in_specs=specs,
        out_specs=pl.BlockSpec((TM, D_MODEL), lambda i: (i, 0)),
        out_shape=jax.ShapeDtypeStruct((t, D_MODEL), F32),
        compiler_params=_params("parallel"),
        name="ffn_final" if final else "ffn",
    )(*ins)


def _rope_tiles(z, tab_ref, shift, period):
    cs, sn = tab_ref[0], tab_ref[1]
    is_x1 = (_iota(cs.shape, 1) & (period - 1)) < shift
    outs = []
    for c in range(z.shape[1] // LANES):
        zt = z[:, c * LANES:(c + 1) * LANES]
        up = pltpu.roll(zt, LANES - shift, axis=1)
        dn = pltpu.roll(zt, shift, axis=1)
        outs.append(zt * cs + jnp.where(is_x1, up, dn) * sn)
    return outs[0] if len(outs) == 1 else jnp.concatenate(outs, axis=1)


def _head_row(z, col, width, aux):
    tile = z[:, (col // LANES) * LANES:(col // LANES + 1) * LANES]
    off = col % LANES
    if off:
        tile = pltpu.roll(tile, LANES - off, axis=1)
    lane = _iota(tile.shape, 1)
    return jnp.where(lane < width, tile, aux)


def _seq_pos(sblk, rows):
    return sblk * rows + _iota((rows, LANES), 0)


def _even_proj_kernel(x_ref, g_ref, t64_ref, wa_ref, wq_ref, wp_ref, wr_ref, wg_ref,
                      a_ref, qn_ref, qr_ref, kc_ref, vc_ref, vs_ref, vw_ref, ks_ref, kw_ref, gt_ref,
                      *, n_sblk):
    sblk = pl.program_id(0) % n_sblk
    h = _rms(x_ref[...], g_ref[...]).astype(BF16)
    lane = _iota((TM, LANES), 1)
    ones_col = jnp.where(lane >= HEAD_DIM, 1.0, 0.0)
    slc_onehot = jnp.where(lane == AUX_LANE + (_seq_pos(sblk, TM) // SLC_BLOCK), 1.0, 0.0)

    a_ref[...] = _dot(h, wa_ref[...])
    zq = _dot(h, wq_ref[...]) * QK_SCALE
    zr = _rope_tiles(zq, t64_ref, ROT_DIM // 2, HEAD_DIM)
    for hh in range(B_HEADS):
        qn_ref[0, hh] = _head_row(zq, hh * HEAD_DIM, HEAD_DIM, 0.0).astype(BF16)
        qr_ref[0, hh] = _head_row(zr, hh * HEAD_DIM, HEAD_DIM, 0.0).astype(BF16)
    zp = _dot(h, wp_ref[...])
    kc_ref[...] = zp[:, :B_KV_WIDTH]
    vc_ref[...] = zp[:, B_KV_WIDTH:2 * B_KV_WIDTH]
    zk = _rope_tiles(_dot(h, wr_ref[...]), t64_ref, ROT_DIM // 2, HEAD_DIM)
    for g in range(B_KV_HEADS):
        vs_ref[0, g] = _head_row(zp, (2 * B_KV_HEADS + g) * HEAD_DIM, HEAD_DIM, ones_col).astype(BF16)
        vw_ref[0, g] = _head_row(zp, (3 * B_KV_HEADS + g) * HEAD_DIM, HEAD_DIM, ones_col).astype(BF16)
        ks_ref[0, g] = _head_row(zk, g * HEAD_DIM, HEAD_DIM, slc_onehot).astype(BF16)
        kw_ref[0, g] = _head_row(zk, (B_KV_HEADS + g) * HEAD_DIM, HEAD_DIM, 0.0).astype(BF16)
    gt_ref[...] = jax.nn.sigmoid(_dot(h, wg_ref[...]))


def _even_proj(x, g, tab64, weights, bsz, seq):
    t = x.shape[0]
    n_sblk = seq // TM
    tok = lambda n: pl.BlockSpec((TM, n), lambda i: (i, 0))
    heads = lambda n: pl.BlockSpec((1, n, TM, LANES), lambda i: (i // n_sblk, 0, i % n_sblk, 0))
    hshape = lambda n: jax.ShapeDtypeStruct((bsz, n, seq, LANES), BF16)
    return pl.pallas_call(
        functools.partial(_even_proj_kernel, n_sblk=n_sblk),
        grid=(t // TM,),
        in_specs=[tok(D_MODEL), _full((1, D_MODEL)), pl.BlockSpec((2, TM, LANES), lambda i: (0, i, 0))]
                 + [_resident(w.shape) for w in weights],
        out_specs=[tok(2 * A_WIDTH), heads(B_HEADS), heads(B_HEADS), tok(B_KV_WIDTH), tok(B_KV_WIDTH),
                   heads(B_KV_HEADS), heads(B_KV_HEADS), heads(B_KV_HEADS), heads(B_KV_HEADS), tok(LANES)],
        out_shape=[jax.ShapeDtypeStruct((t, 2 * A_WIDTH), F32), hshape(B_HEADS), hshape(B_HEADS),
                   jax.ShapeDtypeStruct((t, B_KV_WIDTH), F32), jax.ShapeDtypeStruct((t, B_KV_WIDTH), F32),
                   hshape(B_KV_HEADS), hshape(B_KV_HEADS), hshape(B_KV_HEADS), hshape(B_KV_HEADS),
                   jax.ShapeDtypeStruct((t, LANES), F32)],
        compiler_params=_params("parallel"),
        name="even_in_proj",
    )(x, g.reshape(1, D_MODEL), tab64, *[w.astype(BF16) for w in weights])


OD_ROPE = C_WIDTH + 2 * HEAD_DIM + 2 * D_WIDTH
OD_KD = C_WIDTH + 2 * HEAD_DIM + D_WIDTH
OD_ROPEI = 2 * LANES
OD_PLAIN = 2 * HEAD_DIM + D_WIDTH


def _odd_proj_kernel(x_ref, g_ref, t64_ref, t32_ref, wr_ref, wi_ref, wp_ref,
                     qc_ref, kc_ref, vc_ref, qi_ref, ki_ref, wt_ref, qd_ref, kd_ref, vd_ref, km_ref,
                     *, n_sblk):
    sblk = pl.program_id(0) % n_sblk
    h = _rms(x_ref[...], g_ref[...]).astype(BF16)
    lane = _iota((TM, LANES), 1)
    ones_col = jnp.where(lane >= HEAD_DIM, 1.0, 0.0)
    blk = _seq_pos(sblk, TM) // MOBA_BLOCK

    zr = _rope_tiles(_dot(h, wr_ref[...]), t64_ref, ROT_DIM // 2, HEAD_DIM)
    for hh in range(C_HEADS):
        qc_ref[0, hh] = (_head_row(zr, hh * HEAD_DIM, HEAD_DIM, 0.0) * QK_SCALE).astype(BF16)
    kc_ref[0] = _head_row(zr, C_WIDTH, HEAD_DIM, 0.0).astype(BF16)
    for hh in range(D_HEADS):
        qd = _head_row(zr, C_WIDTH + 2 * HEAD_DIM + hh * HEAD_DIM, HEAD_DIM, 0.0)
        qd_ref[0, hh] = (qd * QK_SCALE).astype(BF16)
        onehot = jnp.where(lane == AUX_LANE + hh * (LANES - AUX_LANE) // D_HEADS + blk, 1.0, 0.0)
        kd_ref[0, hh] = _head_row(zr, OD_KD + hh * HEAD_DIM, HEAD_DIM, onehot).astype(BF16)
    zkd = zr[:, OD_KD:]
    n_mb = TM // MOBA_BLOCK
    km_ref[0] = jnp.sum(zkd.reshape(n_mb, MOBA_BLOCK, D_WIDTH), axis=1) * (1.0 / MOBA_BLOCK)

    zi = _rope_tiles(_dot(h, wi_ref[...]), t32_ref, IDX_ROT // 2, IDX_DIM)
    for hh in range(IDX_HEADS):
        qi_ref[0, hh] = _head_row(zi, hh * IDX_DIM, IDX_DIM, 0.0).astype(BF16)
    ki_ref[0] = _head_row(zi, IDX_HEADS * IDX_DIM, IDX_DIM, 0.0).astype(BF16)

    zp = _dot(h, wp_ref[...])
    vc_ref[0] = _head_row(zp, 0, HEAD_DIM, ones_col).astype(BF16)
    wt_ref[0] = pltpu.roll(zp[:, :LANES], LANES - HEAD_DIM, axis=1).T[:IDX_HEADS]
    for hh in range(D_HEADS):
        vd_ref[0, hh] = _head_row(zp, 2 * HEAD_DIM + hh * HEAD_DIM, HEAD_DIM, ones_col).astype(BF16)


def _odd_proj(x, g, tab64, tab32, weights, bsz, seq):
    t = x.shape[0]
    n_sblk = seq // TM
    tok = lambda n: pl.BlockSpec((TM, n), lambda i: (i, 0))
    tab = pl.BlockSpec((2, TM, LANES), lambda i: (0, i, 0))
    heads = lambda n: pl.BlockSpec((1, n, TM, LANES), lambda i: (i // n_sblk, 0, i % n_sblk, 0))
    single = pl.BlockSpec((1, TM, LANES), lambda i: (i // n_sblk, i % n_sblk, 0))
    hshape = lambda n: jax.ShapeDtypeStruct((bsz, n, seq, LANES), BF16)
    sshape = jax.ShapeDtypeStruct((bsz, seq, LANES), BF16)
    return pl.pallas_call(
        functools.partial(_odd_proj_kernel, n_sblk=n_sblk),
        grid=(t // TM,),
        in_specs=[tok(D_MODEL), _full((1, D_MODEL)), tab, tab] + [_resident(w.shape) for w in weights],
        out_specs=[heads(C_HEADS), single, single, heads(IDX_HEADS), single,
                   pl.BlockSpec((1, IDX_HEADS, TM), lambda i: (i // n_sblk, 0, i % n_sblk)),
                   heads(D_HEADS), heads(D_HEADS), heads(D_HEADS),
                   pl.BlockSpec((1, TM // MOBA_BLOCK, D_WIDTH), lambda i: (i, 0, 0))],
        out_shape=[hshape(C_HEADS), sshape, sshape, hshape(IDX_HEADS), sshape,
                   jax.ShapeDtypeStruct((bsz, IDX_HEADS, seq), F32),
                   hshape(D_HEADS), hshape(D_HEADS), hshape(D_HEADS),
                   jax.ShapeDtypeStruct((t // TM, TM // MOBA_BLOCK, D_WIDTH), F32)],
        compiler_params=_params("parallel"),
        name="odd_in_proj",
    )(x, g.reshape(1, D_MODEL), tab64, tab32, *[w.astype(BF16) for w in weights])


def _rope_tables(positions, rot_dim, period):
    half = rot_dim // 2
    inv_freq = ROPE_THETA ** (-jnp.arange(0, rot_dim, 2, dtype=F32) / rot_dim)
    ang = positions.astype(F32).reshape(-1, 1) * inv_freq
    cos, sin = jnp.cos(ang), jnp.sin(ang)
    t = cos.shape[0]
    cs = jnp.concatenate([cos, cos, jnp.ones((t, period - 2 * half), F32)], axis=1)
    sn = jnp.concatenate([-sin, sin, jnp.zeros((t, period - 2 * half), F32)], axis=1)
    rep = LANES // period
    return jnp.stack([jnp.tile(a, (1, rep)) for a in (cs, sn)])


def _gmlp_kernel(a_ref, n_ref, w_ref, b_ref, o_ref):
    causal = _iota((A_CHUNK, A_CHUNK), 1) <= _iota((A_CHUNK, A_CHUNK), 0)
    lane_group = _iota((A_CHUNK, A_WIDTH), 1) // HEAD_DIM
    ws = [jnp.where(causal, w_ref[g], 0.0).astype(BF16) for g in range(A_GROUPS)]
    for c in range(TM // A_CHUNK):
        rs = slice(c * A_CHUNK, (c + 1) * A_CHUNK)
        z = _gelu(a_ref[rs, :])
        u = z[:, :A_WIDTH]
        v = _rms(z[:, A_WIDTH:], n_ref[...]).astype(BF16)
        mixed = jnp.zeros((A_CHUNK, A_WIDTH), F32)
        for g in range(A_GROUPS):
            mixed = jnp.where(lane_group == g, _dot(ws[g], v) + b_ref[g], mixed)
        o_ref[rs, :] = (u * mixed).astype(BF16)


def _gmlp(a_in, sgu_norm, sgu_w, sgu_b):
    t = a_in.shape[0]
    return pl.pallas_call(
        _gmlp_kernel,
        grid=(t // TM,),
        in_specs=[pl.BlockSpec((TM, 2 * A_WIDTH), lambda i: (i, 0)),
                  _full((1, A_WIDTH)), _full((A_GROUPS, A_CHUNK, A_CHUNK)),
                  _full((A_GROUPS, A_CHUNK, 1))],
        out_specs=pl.BlockSpec((TM, A_WIDTH), lambda i: (i, 0)),
        out_shape=jax.ShapeDtypeStruct((t, A_WIDTH), BF16),
        compiler_params=_params("parallel"),
        name="gmlp",
    )(a_in, sgu_norm.reshape(1, A_WIDTH), sgu_w, sgu_b.reshape(A_GROUPS, A_CHUNK, 1))


def _compress_one(t2, pos_ref, w1_ref, w2_ref):
    a = _dot((t2 + pos_ref[0:1, :]).astype(BF16), w1_ref[0])
    b = _dot((t2 + pos_ref[1:2, :]).astype(BF16), w1_ref[1])
    hid = _gelu(a + pltpu.roll(b, t2.shape[0] - 1, axis=0))
    out = _dot(hid.astype(BF16), w2_ref[...])
    row = _iota(out.shape, 0)
    return jnp.where(row < t2.shape[0] - 1, out, 0.0).astype(BF16)


def _compress_kernel(kc_ref, vc_ref, pk_ref, w1k_ref, w2k_ref, pv_ref, w1v_ref, w2v_ref, ko_ref, vo_ref):
    for g in range(B_KV_HEADS):
        ko_ref[0, g] = _compress_one(kc_ref[0, g], pk_ref, w1k_ref, w2k_ref)
        vo_ref[0, g] = _compress_one(vc_ref[0, g], pv_ref, w1v_ref, w2v_ref)


def _compress(kc2, vc2, pos_k, w1_k, w2_k, pos_v, w1_v, w2_v):
    bsz, _, nrow, wide = kc2.shape

    def prep(pos, w1, w2):
        w2p = jnp.pad(w2, ((0, 0), (0, LANES - HEAD_DIM)))
        return (pos.reshape(2, wide), w1.reshape(2, wide, CMP_HIDDEN).astype(BF16), w2p.astype(BF16))

    pk, w1k, w2k = prep(pos_k, w1_k, w2_k)
    pv, w1v, w2v = prep(pos_v, w1_v, w2_v)
    blk_in = pl.BlockSpec((1, B_KV_HEADS, nrow, wide), lambda b: (b, 0, 0, 0))
    blk_out = pl.BlockSpec((1, B_KV_HEADS, nrow, LANES), lambda b: (b, 0, 0, 0))
    wspecs = [_full((2, wide)), _full((2, wide, CMP_HIDDEN)), _full((CMP_HIDDEN, LANES))]
    return pl.pallas_call(
        _compress_kernel,
        grid=(bsz,),
        in_specs=[blk_in, blk_in] + wspecs + wspecs,
        out_specs=[blk_out, blk_out],
        out_shape=[jax.ShapeDtypeStruct((bsz, B_KV_HEADS, nrow, LANES), BF16)] * 2,
        compiler_params=_params("parallel"),
        name="nsa_compress",
    )(kc2, vc2, pk, w1k, w2k, pv, w1v, w2v)


def _rank_lt(vals, axis, n, k):
    j = _iota(vals.shape, axis)
    rank = jnp.zeros(vals.shape, F32)
    for jp in range(n):
        row = lax.slice_in_dim(vals, jp, jp + 1, axis=axis)
        beats = jnp.where(row > vals, 1.0, jnp.where(row == vals, jnp.where(j > jp, 1.0, 0.0), 0.0))
        rank = rank + beats
    return jnp.where(rank < k, 1.0, 0.0)


def _penalty_rows(pen_t):
    n, nq = pen_t.shape
    parts = [jnp.zeros((AUX_LANE, nq), F32), pen_t]
    if LANES - AUX_LANE - n:
        parts.append(jnp.zeros((LANES - AUX_LANE - n, nq), F32))
    full = jnp.concatenate(parts, axis=0)
    halves = [full[:, c * LANES:(c + 1) * LANES].T for c in range(nq // LANES)]
    return halves[0] if len(halves) == 1 else jnp.concatenate(halves, axis=0)


def _swap_halves(acc):
    return pltpu.roll(acc, HEAD_DIM, axis=1)


def _store_normalized(o_ref, accs):
    lane = _iota((QB, LANES), 1)
    for p in range(len(accs) // 2):
        even, odd = accs[2 * p], accs[2 * p + 1]
        both = jnp.where(lane < HEAD_DIM, even / _swap_halves(even), _swap_halves(odd) / odd)
        o_ref[0, :, p * LANES:(p + 1) * LANES] = both.astype(BF16)


def _nsa_kernel(qr_ref, qn_ref, gt_ref, kc_ref, vc_ref, ks_ref, vs_ref, kw_ref, vw_ref, ovt_ref, ex_ref,
                o_ref, s_scr, q_scr, cmp_scr, m_scr, acc_scr):
    i = pl.program_id(1)
    q0 = i * QB
    rows = B_GROUP * QB
    n_slc = ovt_ref.shape[0]
    n_g = B_KV_HEADS
    gates = gt_ref[0]
    tq2 = q0 + _iota((QB, KT), 0)
    key2 = _iota((QB, KT), 1)

    def masked(s, ok):
        n = s.shape[1]
        return jnp.where(ok[None], s.reshape(B_GROUP, QB, n), NEG_INF).reshape(rows, n)

    m_c = ((_iota((QB, LANES), 1) * CMP_STRIDE + (CMP_LEN - 1)) <= q0 + _iota((QB, LANES), 0))[None]
    for g in range(n_g):
        hs = slice(g * B_GROUP, (g + 1) * B_GROUP)
        qn = qn_ref[0, hs].reshape(rows, LANES)
        s_c = _dot_nt(qn, kc_ref[0, g]).reshape(B_GROUP, QB, LANES)
        sm = jnp.where(m_c, s_c, NEG_INF)
        e = jnp.where(m_c, jnp.exp2(sm - jnp.max(sm, axis=-1, keepdims=True)), 0.0)
        den = jnp.sum(e, axis=-1, keepdims=True)
        p_cb = (e / jnp.where(den > 0.0, den, 1.0)).reshape(rows, LANES).astype(BF16)
        cmp_scr[g] = _dot(p_cb, vc_ref[0, g])

        imp = jnp.zeros((n_slc, QB), F32)
        for r in range(B_GROUP):
            imp = imp + _dot_nt(ovt_ref[...], p_cb[r * QB:(r + 1) * QB])
        jb = _iota((n_slc, QB), 0)
        tq = q0 + _iota((n_slc, QB), 1)
        forced = (jb == 0) | (jb == (tq >> 6))
        imp = jnp.where(jb * SLC_BLOCK <= tq, jnp.where(forced, FORCE_SCORE, imp), NEG_INF)
        pen = _penalty_rows((_rank_lt(imp, 0, n_slc, SLC_TOPN) - 1.0) * (-NEG_INF))
        qr = qr_ref[0, hs].astype(F32) + pen[None]
        q_scr[g] = qr.reshape(rows, LANES).astype(BF16)

    n_wt = (WINDOW + QB + KT - 1) // KT
    w_off = [QB - (n_wt - c) * KT for c in range(n_wt)]
    w_base = [pl.multiple_of(jnp.maximum(q0 + off, 0), min(QB, KT)) for off in w_off]

    def window_scores(interior):
        w_mx = [jnp.full((rows, LANES), NEG_INF, F32) for _ in range(n_g)]
        for c in range(n_wt):
            key = key2 + w_base[c]
            ok = None
            if not interior:
                ok = (key > tq2 - WINDOW) & (key <= jnp.minimum(tq2, q0 + (w_off[c] + KT - 1)))
            elif w_off[c] < QB - WINDOW and w_off[c] + KT - 1 > 0:
                ok = (key > tq2 - WINDOW) & (key <= tq2)
            elif w_off[c] < QB - WINDOW:
                ok = key > tq2 - WINDOW
            elif w_off[c] + KT - 1 > 0:
                ok = key <= tq2
            for g in range(n_g):
                s = _dot_nt(q_scr[g], kw_ref[0, g, pl.ds(w_base[c], KT), :])
                if ok is not None:
                    s = masked(s, ok)
                s_scr[c, g] = s
                w_mx[g] = jnp.maximum(w_mx[g], jnp.maximum(s[:, :LANES], s[:, LANES:]))
        for g in range(n_g):
            m_scr[g] = w_mx[g]

    first_interior = (n_wt * KT - QB + QB - 1) // QB
    pl.when(i >= first_interior)(functools.partial(window_scores, True))
    pl.when(i < first_interior)(functools.partial(window_scores, False))
    for g in range(n_g):
        w_max = jnp.max(m_scr[g], axis=-1, keepdims=True)
        acc_w = jnp.zeros((rows, LANES), F32)
        for c in range(n_wt):
            p = jnp.exp2(s_scr[c, g] - w_max).astype(BF16)
            acc_w = acc_w + _dot(p, vw_ref[0, g, pl.ds(w_base[c], KT), :])
        acc_scr[g] = acc_w

    n_tiles = (q0 + QB + KT - 1) // KT
    last = n_tiles - 1
    for g in range(n_g):
        m_scr[g] = jnp.full((rows, LANES), NEG_INF, F32)

    def sel_scores(c, causal):
        base = pl.multiple_of(c * KT, KT)
        for g in range(n_g):
            s = _dot_nt(q_scr[g], ks_ref[0, g, pl.ds(base, KT), :])
            if causal:
                s = masked(s, key2 + base <= tq2)
            s_scr[c, g] = s
            m_scr[g] = jnp.maximum(m_scr[g], jnp.maximum(s[:, :LANES], s[:, LANES:]))

    def sel_pass1(c, _):
        sel_scores(c, False)
        return 0

    lax.fori_loop(0, last, sel_pass1, 0)
    sel_scores(last, True)

    m_sel = [jnp.max(m_scr[g], axis=-1, keepdims=True) for g in range(n_g)]
    for g in range(n_g):
        m_scr[g] = jnp.zeros((rows, LANES), F32)

    def sel_pass2(c, _):
        base = pl.multiple_of(c * KT, KT)
        for g in range(n_g):
            p = jnp.exp2(s_scr[c, g] - m_sel[g]).astype(BF16)
            m_scr[g] += _dot(p, vs_ref[0, g, pl.ds(base, KT), :])
        return 0

    lax.fori_loop(0, n_tiles, sel_pass2, 0)

    g_hi = gates.astype(BF16)
    g_lo = (gates - g_hi.astype(F32)).astype(BF16)
    g_parts = jnp.concatenate([g_hi, g_lo], axis=1)
    lane = _iota((QB, LANES), 1)
    pair_cols = 2 * N_BRANCH * LANES
    for p in range(B_HEADS // 2):
        g, r = (2 * p) // B_GROUP, (2 * p) % B_GROUP
        ev, od = slice(r * QB, (r + 1) * QB), slice((r + 1) * QB, (r + 2) * QB)
        spread = _dot(g_parts, ex_ref[:, p * pair_cols:(p + 1) * pair_cols])
        gate = [spread[:, k * LANES:(k + 1) * LANES] for k in range(2 * N_BRANCH)]
        sel_e, win_e = m_scr[g, ev, :], acc_scr[g, ev, :]
        sel_o, win_o = m_scr[g, od, :], acc_scr[g, od, :]
        lower = (gate[0] * cmp_scr[g, ev, :] + gate[1] * (sel_e / _swap_halves(sel_e))
                 + gate[2] * (win_e / _swap_halves(win_e)))
        upper = (gate[3] * _swap_halves(cmp_scr[g, od, :]) + gate[4] * (_swap_halves(sel_o) / sel_o)
                 + gate[5] * (_swap_halves(win_o) / win_o))
        o_ref[0, :, p * LANES:(p + 1) * LANES] = jnp.where(lane < HEAD_DIM, lower, upper).astype(BF16)


def _overlap_t(n_rows, n_slc):
    n = np.arange(n_rows)
    c0 = n * CMP_STRIDE
    s0 = np.arange(n_slc) * SLC_BLOCK
    m = (c0[None, :] < s0[:, None] + SLC_BLOCK) & (c0[None, :] + CMP_LEN > s0[:, None])
    m = m & (n[None, :] < n_rows - 1)
    return jnp.asarray(m, dtype=BF16)


def _nsa(q_rot, q_raw, gates, k_cmp, v_cmp, k_slc, v_slc, k_win, v_win):
    bsz, _, seq, _ = q_rot.shape
    n_qb = seq // QB
    n_slc = seq // SLC_BLOCK
    n_cmp_rows = k_cmp.shape[2]
    ovt = _overlap_t(n_cmp_rows, n_slc)
    n_gate = B_HEADS * N_BRANCH
    spread = np.zeros((2 * LANES, n_gate * LANES), np.float32)
    for c in range(n_gate):
        spread[[c, LANES + c], c * LANES:(c + 1) * LANES] = 1.0
    spread = jnp.asarray(spread, dtype=BF16)
    q_spec = pl.BlockSpec((1, B_HEADS, QB, LANES), lambda b, i: (b, 0, i, 0))
    cmp_spec = pl.BlockSpec((1, B_KV_HEADS, n_cmp_rows, LANES), lambda b, i: (b, 0, 0, 0))
    kv_spec = pl.BlockSpec((1, B_KV_HEADS, seq, LANES), lambda b, i: (b, 0, 0, 0), pipeline_mode=pl.Buffered(1))
    rows = B_GROUP * QB
    return pl.pallas_call(
        _nsa_kernel,
        grid=(bsz, n_qb),
        in_specs=[q_spec, q_spec, pl.BlockSpec((1, QB, LANES), lambda b, i: (b, i, 0)),
                  cmp_spec, cmp_spec, kv_spec, kv_spec, kv_spec, kv_spec, _full(ovt.shape),
                  _resident(spread.shape)],
        out_specs=pl.BlockSpec((1, QB, B_WIDTH), lambda b, i: (b, i, 0)),
        out_shape=jax.ShapeDtypeStruct((bsz, seq, B_WIDTH), BF16),
        scratch_shapes=[pltpu.VMEM((seq // KT, B_KV_HEADS, rows, KT), F32),
                        pltpu.VMEM((B_KV_HEADS, rows, LANES), BF16),
                        pltpu.VMEM((B_KV_HEADS, rows, LANES), F32),
                        pltpu.VMEM((B_KV_HEADS, rows, LANES), F32),
                        pltpu.VMEM((B_KV_HEADS, rows, LANES), F32)],
        compiler_params=_params("parallel", "arbitrary"),
        name="nsa_attention",
    )(q_rot, q_raw, gates, k_cmp, v_cmp, k_slc, v_slc, k_win, v_win, ovt, spread)


def _dsa_kernel(q_ref, k_ref, v_ref, qi_ref, ki_ref, wi_ref, o_ref,
                sc_scr, bias_scr, s_scr, mx_scr, acc_scr, *, top_k):
    i = pl.program_id(1)
    q0 = i * QB
    n_tiles = (q0 + QB + KT - 1) // KT
    tile = (KT, QB)
    key_sub = _iota(tile, 0)
    tq = q0 + _iota(tile, 1)
    w_idx = wi_ref[0]
    qi = qi_ref[0].reshape(IDX_HEADS * QB, LANES)
    n_sub = KT // 8

    def fold(x, op):
        return op(op(x.reshape(n_sub // 4, 4, 8, QB), axis=0), axis=0)

    w_s = w_idx * (IDX_DIM ** -0.5 * IDX_HEADS ** -0.5)
    last = n_tiles - 1

    def score_tile(c, carry, masked):
        mn, mx, zge, zgt = carry
        base = pl.multiple_of(c * KT, KT)
        lg = _dot_nt(ki_ref[0, pl.ds(base, KT), :], qi)
        sc = jnp.zeros(tile, F32)
        for h in range(IDX_HEADS):
            sc = sc + w_s[h:h + 1, :] * jnp.maximum(lg[:, h * QB:(h + 1) * QB], 0.0)
        if masked:
            valid = key_sub + base <= tq
            lowest = jnp.where(valid, sc, -NEG_INF)
            sc = jnp.where(valid, sc, NEG_INF)
        else:
            lowest = sc
        sc_scr[c] = sc
        return (jnp.minimum(mn, fold(lowest, jnp.min)), jnp.maximum(mx, fold(sc, jnp.max)),
                zge + fold(jnp.where(sc >= 0.0, 1.0, 0.0), jnp.sum),
                zgt + fold(jnp.where(sc > 0.0, 1.0, 0.0), jnp.sum))

    stats = lax.fori_loop(0, last, lambda c, carry: score_tile(c, carry, False),
                          (jnp.full((8, QB), -NEG_INF, F32), jnp.full((8, QB), NEG_INF, F32),
                           jnp.zeros((8, QB), F32), jnp.zeros((8, QB), F32)))
    mn, mx, zge, zgt = score_tile(last, stats, True)
    mn = jnp.min(mn, axis=0, keepdims=True)
    mx = jnp.max(mx, axis=0, keepdims=True)
    zge = jnp.sum(zge, axis=0, keepdims=True)
    zgt = jnp.sum(zgt, axis=0, keepdims=True)
    kf = jnp.float32(top_k)
    n_valid = (tq[0:1, :] + 1).astype(F32)
    all_taken = n_valid <= kf
    zero_tied = jnp.logical_and(zgt < kf, zge >= kf)
    positive = zgt >= kf

    def search_step(carry, kind):
        lo_, hi_, clo, chi = carry
        if kind == "halve":
            frac = 0.5
        else:
            frac = jnp.clip((clo - kf + 0.5) / jnp.maximum(clo - chi, 1.0), 1.0 / 64, 63.0 / 64)
            if kind == "mixed":
                frac = jnp.where(clo - chi > SEARCH_WIDE, 0.5, frac)
        mid = lo_ + (hi_ - lo_) * frac
        cnt = lax.fori_loop(0, n_tiles,
                            lambda c, a: a + fold(jnp.where(sc_scr[c] >= mid, 1.0, 0.0), jnp.sum),
                            jnp.zeros((8, QB), F32))
        cm = jnp.sum(cnt, axis=0, keepdims=True)
        ge = cm >= kf
        return (jnp.where(ge, mid, lo_), jnp.where(ge, hi_, mid), jnp.where(ge, cm, clo), jnp.where(ge, chi, cm))

    def round_cond(carry):
        r, _, _, clo, chi = carry
        settled = jnp.where(all_taken, 1.0, jnp.where(zero_tied, 1.0,
                            jnp.where(clo == kf, 1.0, jnp.where(chi == kf - 1.0, 1.0, 0.0))))
        return jnp.logical_and(r < SEARCH_ROUNDS, jnp.min(settled) < 1.0)

    def round_body(carry):
        st = carry[1:]
        for kind in SEARCH_PATTERN:
            st = search_step(st, kind)
        return (carry[0] + 1,) + st

    top = mx + jnp.abs(mx) * 1e-3 + 1.0
    init = (jnp.int32(0), jnp.where(positive, 0.0, mn), jnp.where(positive, top, 0.0),
            jnp.where(positive, zge, n_valid), jnp.where(positive, 0.0, zge))
    _, lo, hi, c_lo, c_hi = lax.while_loop(round_cond, round_body, init)
    exact_cut = c_lo == kf

    def thr_body(c, vm):
        sc = sc_scr[c]
        return jnp.maximum(vm, fold(jnp.where(sc < hi, sc, NEG_INF), jnp.max))

    vm = lax.fori_loop(0, n_tiles, thr_body, jnp.full((8, QB), NEG_INF, F32))
    thr = jnp.where(zero_tied, 0.0, jnp.max(vm, axis=0, keepdims=True))
    need = kf - jnp.where(zero_tied, zgt, c_hi)
    tri = jnp.where(_iota((KT, KT), 1) <= _iota((KT, KT), 0), 1.0, 0.0).astype(BF16)

    def mask_body(c, seen):
        base = c * KT
        sc = sc_scr[c]
        eq = sc == thr
        prefix = _dot(tri, jnp.where(eq, 1.0, 0.0).astype(BF16)) + seen
        tied = jnp.where(eq, jnp.where(prefix <= need, 1.0, 0.0), 0.0)
        by_thr = jnp.where(sc > thr, 1.0, tied)
        chosen = jnp.where(exact_cut, jnp.where(sc >= lo, 1.0, 0.0), by_thr)
        valid = jnp.where(key_sub + base <= tq, 1.0, 0.0)
        bias_t = (jnp.where(all_taken, valid, chosen) - 1.0) * (-NEG_INF)
        for qh in range(QB // LANES):
            for kh in range(KT // LANES):
                bias_scr[c, qh * LANES:(qh + 1) * LANES, kh * LANES:(kh + 1) * LANES] = (
                    bias_t[kh * LANES:(kh + 1) * LANES, qh * LANES:(qh + 1) * LANES].T)
        return prefix[KT - 1:KT, :]

    lax.fori_loop(0, n_tiles, mask_body, jnp.zeros((1, QB), F32))

    rows = C_HEADS * QB
    q = q_ref[0].reshape(rows, LANES)
    mx_scr[...] = jnp.full((rows, LANES), NEG_INF, F32)

    def att_pass1(c, _):
        base = pl.multiple_of(c * KT, KT)
        s = _dot_nt(q, k_ref[0, pl.ds(base, KT), :])
        s = (s.reshape(C_HEADS, QB, KT) + bias_scr[c][None]).reshape(rows, KT)
        s_scr[c] = s
        mx_scr[...] = jnp.maximum(mx_scr[...], jnp.maximum(s[:, :LANES], s[:, LANES:]))
        return 0

    lax.fori_loop(0, n_tiles, att_pass1, 0)
    m = jnp.max(mx_scr[...], axis=-1, keepdims=True)
    acc_scr[...] = jnp.zeros((rows, LANES), F32)

    def att_pass2(c, _):
        base = pl.multiple_of(c * KT, KT)
        p = jnp.exp2(s_scr[c] - m).astype(BF16)
        acc_scr[...] += _dot(p, v_ref[0, pl.ds(base, KT), :])
        return 0

    lax.fori_loop(0, n_tiles, att_pass2, 0)
    _store_normalized(o_ref, [acc_scr[h * QB:(h + 1) * QB, :] for h in range(C_HEADS)])


def _dsa(q, k, v, q_idx, k_idx, w_idx_t):
    bsz, _, seq, _ = q.shape
    n_qb = seq // QB
    top_k = min(DSA_TOPK, seq // 4)
    rows = C_HEADS * QB
    single = pl.BlockSpec((1, seq, LANES), lambda b, i: (b, 0, 0))
    return pl.pallas_call(
        functools.partial(_dsa_kernel, top_k=top_k),
        grid=(bsz, n_qb),
        in_specs=[pl.BlockSpec((1, C_HEADS, QB, LANES), lambda b, i: (b, 0, i, 0)),
                  single, single,
                  pl.BlockSpec((1, IDX_HEADS, QB, LANES), lambda b, i: (b, 0, i, 0)),
                  single,
                  pl.BlockSpec((1, IDX_HEADS, QB), lambda b, i: (b, 0, i))],
        out_specs=pl.BlockSpec((1, QB, C_WIDTH), lambda b, i: (b, i, 0)),
        out_shape=jax.ShapeDtypeStruct((bsz, seq, C_WIDTH), BF16),
        scratch_shapes=[pltpu.VMEM((seq // KT, KT, QB), F32),
                        pltpu.VMEM((seq // KT, QB, KT), F32),
                        pltpu.VMEM((seq // KT, rows, KT), F32),
                        pltpu.VMEM((rows, LANES), F32),
                        pltpu.VMEM((rows, LANES), F32)],
        compiler_params=_params("parallel", "arbitrary"),
        name="dsa_attention",
    )(q, k, v, q_idx, k_idx, w_idx_t)


def _moba_kernel(q_ref, k_ref, v_ref, km_ref, o_ref, s_scr, m_scr, acc_scr, *, top_k):
    i = pl.program_id(1)
    q0 = i * QB
    own = q0 // MOBA_BLOCK
    n_blk = km_ref.shape[2]
    q = q_ref[0]
    gate = _bdot_nt(km_ref[0], q) * (1.0 / QK_SCALE)
    jb = _iota(gate.shape, 1)
    past = jb < own
    gate = jnp.where(past, gate, NEG_INF)
    sel_t = jnp.where(past, _rank_lt(gate, 1, n_blk, top_k), 0.0)
    pen_t = jnp.where(jb == own, 0.0, (sel_t - 1.0) * (-NEG_INF))
    pen = _penalty_rows(pen_t.reshape(D_HEADS * n_blk, QB))
    q_ext = (q.astype(F32) + pen[None]).astype(BF16)

    m_scr[...] = jnp.full(m_scr.shape, NEG_INF, F32)
    tq = q0 + _iota((QB, MOBA_BLOCK), 0)
    key = _iota((QB, MOBA_BLOCK), 1)

    def scores(j, masked):
        base = pl.multiple_of(j * MOBA_BLOCK, MOBA_BLOCK)
        s = _bdot_nt(q_ext, k_ref[0, :, pl.ds(base, MOBA_BLOCK), :])
        if masked:
            s = jnp.where((key + base <= tq)[None], s, NEG_INF)
        s_scr[j] = s
        m_scr[...] = jnp.maximum(m_scr[...], jnp.maximum(s[..., :LANES], s[..., LANES:]))

    def pass1(j, _):
        scores(j, False)
        return 0

    lax.fori_loop(0, own, pass1, 0)
    scores(own, True)
    m = jnp.max(m_scr[...], axis=-1, keepdims=True)
    acc_scr[...] = jnp.zeros(acc_scr.shape, F32)

    def pass2(j, _):
        base = pl.multiple_of(j * MOBA_BLOCK, MOBA_BLOCK)
        p = jnp.exp2(s_scr[j] - m).astype(BF16)
        acc_scr[...] += _bdot(p, v_ref[0, :, pl.ds(base, MOBA_BLOCK), :])
        return 0

    lax.fori_loop(0, own + 1, pass2, 0)
    _store_normalized(o_ref, [acc_scr[h] for h in range(D_HEADS)])


def _moba(q, k, v, k_mean):
    bsz, _, seq, _ = q.shape
    n_qb = seq // QB
    n_blk = seq // MOBA_BLOCK
    top_k = min(MOBA_TOPK, n_blk - 1)
    q_spec = pl.BlockSpec((1, D_HEADS, QB, LANES), lambda b, i: (b, 0, i, 0))
    kv_spec = pl.BlockSpec((1, D_HEADS, seq, LANES), lambda b, i: (b, 0, 0, 0))
    return pl.pallas_call(
        functools.partial(_moba_kernel, top_k=top_k),
        grid=(bsz, n_qb),
        in_specs=[q_spec, kv_spec, kv_spec,
                  pl.BlockSpec((1, D_HEADS, n_blk, LANES), lambda b, i: (b, 0, 0, 0))],
        out_specs=pl.BlockSpec((1, QB, D_WIDTH), lambda b, i: (b, i, 0)),
        out_shape=jax.ShapeDtypeStruct((bsz, seq, D_WIDTH), BF16),
        scratch_shapes=[pltpu.VMEM((n_blk, D_HEADS, QB, MOBA_BLOCK), F32),
                        pltpu.VMEM((D_HEADS, QB, LANES), F32),
                        pltpu.VMEM((D_HEADS, QB, LANES), F32)],
        compiler_params=_params("parallel", "arbitrary"),
        name="moba_attention",
    )(q, k, v, k_mean)


def _cols(w, start, size):
    return w[:, start:start + size]


def _even_mixer(x, bsz, seq, tab64, norm_g, w_in, sgu_norm, sgu_w, sgu_b,
                cmp_pos_k, cmp_w1_k, cmp_w2_k, cmp_pos_v, cmp_w1_v, cmp_w2_v, w_out):
    o_q = 2 * A_WIDTH
    o_kc = o_q + B_WIDTH
    o_vc, o_ksl, o_vsl, o_kw, o_vw = (o_kc + B_KV_WIDTH * n for n in range(1, 6))
    o_gl = o_kc + 6 * B_KV_WIDTH
    n_gate = B_HEADS * N_BRANCH
    w_a = _cols(w_in, 0, 2 * A_WIDTH)
    w_q = _cols(w_in, o_q, B_WIDTH)
    w_plain = jnp.concatenate([_cols(w_in, o, B_KV_WIDTH) for o in (o_kc, o_vc, o_vsl, o_vw)], axis=1)
    w_rope = jnp.concatenate([_cols(w_in, o, B_KV_WIDTH) for o in (o_ksl, o_kw)], axis=1)
    w_gate = jnp.pad(_cols(w_in, o_gl, n_gate), ((0, 0), (0, LANES - n_gate)))
    a_in, q_raw, q_rot, kc, vc, v_slc, v_win, k_slc, k_win, gates = _even_proj(
        x, norm_g, tab64, [w_a, w_q, w_plain, w_rope, w_gate], bsz, seq)

    a_out = _gmlp(a_in, sgu_norm, sgu_w, sgu_b)

    def cmp_rows(t):
        t = t.reshape(bsz, seq // CMP_STRIDE, CMP_STRIDE, B_KV_HEADS, HEAD_DIM)
        return t.transpose(0, 3, 1, 2, 4).reshape(bsz, B_KV_HEADS, seq // CMP_STRIDE, CMP_STRIDE * HEAD_DIM)

    k_cmp, v_cmp = _compress(cmp_rows(kc), cmp_rows(vc), cmp_pos_k, cmp_w1_k, cmp_w2_k,
                             cmp_pos_v, cmp_w1_v, cmp_w2_v)
    b_out = _nsa(q_rot, q_raw, gates.reshape(bsz, seq, LANES), k_cmp, v_cmp, k_slc, v_slc, k_win, v_win)
    return a_out, b_out.reshape(bsz * seq, B_WIDTH), w_out


def _odd_mixer(x, bsz, seq, tab64, tab32, norm_g, w_in, w_out):
    sizes = (C_WIDTH, HEAD_DIM, HEAD_DIM, IDX_HEADS * IDX_DIM, IDX_DIM, IDX_HEADS, D_WIDTH, D_WIDTH, D_WIDTH)
    offs = np.concatenate([[0], np.cumsum(sizes)])
    w_qc, w_kc, w_vc, w_qi, w_ki, w_wi, w_qd, w_kd, w_vd = (
        _cols(w_in, int(o), int(s)) for o, s in zip(offs[:-1], sizes))
    zpad = lambda n: jnp.zeros((D_MODEL, n), w_in.dtype)
    w_rope = jnp.concatenate([w_qc, w_kc, zpad(HEAD_DIM), w_qd, w_kd], axis=1)
    w_ropei = jnp.concatenate([w_qi, w_ki, zpad(OD_ROPEI - IDX_HEADS * IDX_DIM - IDX_DIM)], axis=1)
    w_plain = jnp.concatenate([w_vc, w_wi, zpad(HEAD_DIM - IDX_HEADS), w_vd], axis=1)
    qc, kc, vc, qi, ki, wi_t, qd, kd, vd, k_mean = _odd_proj(
        x, norm_g, tab64, tab32, [w_rope, w_ropei, w_plain], bsz, seq)

    c_out = _dsa(qc, kc, vc, qi, ki, wi_t)
    n_blk = seq // MOBA_BLOCK
    km = k_mean.reshape(bsz, n_blk, D_HEADS, HEAD_DIM).transpose(0, 2, 1, 3)
    km = jnp.pad(km, ((0, 0), (0, 0), (0, 0), (0, LANES - HEAD_DIM))).astype(BF16)
    d_out = _moba(qd, kd, vd, km)
    return c_out.reshape(bsz * seq, C_WIDTH), d_out.reshape(bsz * seq, D_WIDTH), w_out


def kernel(x, positions, ffn1_norm, ffn1_w_gate, ffn1_w_up, ffn1_w_down, mix_norm, ffn2_norm, ffn2_w_gate, ffn2_w_up, ffn2_w_down, ev_w_in, ev_sgu_norm, ev_sgu_w, ev_sgu_b, ev_cmp_pos_k, ev_cmp_w1_k, ev_cmp_w2_k, ev_cmp_pos_v, ev_cmp_w1_v, ev_cmp_w2_v, ev_w_out, od_w_in, od_w_out, final_norm):
    bsz, seq, _ = x.shape
    depth = ffn1_norm.shape[0]
    tab64 = _rope_tables(positions, ROT_DIM, HEAD_DIM)
    tab32 = _rope_tables(positions, IDX_ROT, IDX_DIM)
    x = x.reshape(bsz * seq, D_MODEL)
    for i in range(depth):
        x = _ffn(x, ffn1_norm[i], ffn1_w_gate[i], ffn1_w_up[i], ffn1_w_down[i])
        if i % 2 == 0:
            e = i // 2
            mix = _even_mixer(x, bsz, seq, tab64, mix_norm[i], ev_w_in[e], ev_sgu_norm[e], ev_sgu_w[e],
                              ev_sgu_b[e], ev_cmp_pos_k[e], ev_cmp_w1_k[e], ev_cmp_w2_k[e],
                              ev_cmp_pos_v[e], ev_cmp_w1_v[e], ev_cmp_w2_v[e], ev_w_out[e])
        else:
            o = i // 2
            mix = _odd_mixer(x, bsz, seq, tab64, tab32, mix_norm[i], od_w_in[o], od_w_out[o])
        x = _ffn(x, ffn2_norm[i], ffn2_w_gate[i], ffn2_w_up[i], ffn2_w_down[i],
                 final_g=final_norm if i == depth - 1 else None, mix=mix)
    return x.reshape(bsz, seq, D_MODEL)
```

```python
import functools

import numpy as np
import jax
import jax.numpy as jnp
from jax import lax
from jax.experimental import pallas as pl
from jax.experimental.pallas import tpu as pltpu

F32 = jnp.float32
BF16 = jnp.bfloat16

D_MODEL = 1024
HEAD_DIM = 64
ROT_DIM = HEAD_DIM // 4
ROPE_THETA = 500000.0
NORM_EPS = 1e-6
D_FF = 2816
NEG_INF = -1e30

A_GROUPS = 4
A_CHUNK = 128
A_WIDTH = A_GROUPS * HEAD_DIM
B_HEADS = 12
B_KV_HEADS = 3
B_GROUP = B_HEADS // B_KV_HEADS
B_WIDTH = B_HEADS * HEAD_DIM
B_KV_WIDTH = B_KV_HEADS * HEAD_DIM
CMP_LEN = 32
CMP_STRIDE = 16
CMP_HIDDEN = 256
SLC_BLOCK = 64
SLC_TOPN = 8
WINDOW = 512
N_BRANCH = 3
FORCE_SCORE = 1e4
C_HEADS = 8
C_WIDTH = C_HEADS * HEAD_DIM
IDX_HEADS = 4
IDX_DIM = 32
IDX_ROT = IDX_DIM // 4
DSA_TOPK = 256
D_HEADS = 8
D_WIDTH = D_HEADS * HEAD_DIM
MOBA_BLOCK = 256
MOBA_TOPK = 3

LANES = 128
ATT_SCALE = HEAD_DIM ** -0.5
VMEM_LIMIT = 56 * 1024 * 1024
TM = 512
FF_CHUNK = D_FF // 11
KT = 256
QB = 256
QK_SCALE = ATT_SCALE * 1.4426950408889634
SEARCH_ROUNDS = 32
SEARCH_PATTERN = ("interp", "mixed", "interp", "halve")
SEARCH_WIDE = 32.0
AUX_LANE = HEAD_DIM


def _dot(a, b):
    return jnp.dot(a, b, preferred_element_type=F32)


def _dot_nt(a, b):
    return lax.dot_general(a, b, (((1,), (1,)), ((), ())), preferred_element_type=F32)


def _bdot_nt(a, b):
    return lax.dot_general(a, b, (((2,), (2,)), ((0,), (0,))), preferred_element_type=F32)


def _bdot(a, b):
    return lax.dot_general(a, b, (((2,), (1,)), ((0,), (0,))), preferred_element_type=F32)


def _iota(shape, dim):
    return lax.broadcasted_iota(jnp.int32, shape, dim)


def _rms(x, g):
    ms = jnp.mean(x * x, axis=-1, keepdims=True)
    return x * lax.rsqrt(ms + NORM_EPS) * g


def _gelu(x):
    return x * (0.5 * (1.0 + jnp.tanh(0.7978845608028654 * (x + 0.044715 * (x * x * x)))))


def _params(*sem):
    return pltpu.CompilerParams(dimension_semantics=sem, vmem_limit_bytes=VMEM_LIMIT)


def _full(shape):
    n = len(shape)
    return pl.BlockSpec(shape, lambda *_: (0,) * n)


def _resident(shape):
    n = len(shape)
    return pl.BlockSpec(shape, lambda *_: (0,) * n, pipeline_mode=pl.Buffered(1))


def _ffn_kernel(x_ref, g_ref, wg_ref, wu_ref, wd_ref, *rest, final, mixed):
    o_ref = rest[-1]
    x = x_ref[...]
    if mixed:
        a_ref, b_ref, wa_ref, wb_ref = rest[:4]
        x = x + (_dot(a_ref[...], wa_ref[...]) + _dot(b_ref[...], wb_ref[...]))
        rest = rest[4:]
    h = _rms(x, g_ref[...]).astype(BF16)
    acc = jnp.zeros_like(x)
    for c in range(D_FF // FF_CHUNK):
        sl = slice(c * FF_CHUNK, (c + 1) * FF_CHUNK)
        gate = _dot(h, wg_ref[:, sl])
        up = _dot(h, wu_ref[:, sl])
        act = (gate * jax.nn.sigmoid(gate) * up).astype(BF16)
        acc = acc + _dot(act, wd_ref[sl, :])
    y = x + 0.5 * acc
    if final:
        y = _rms(y, rest[0][...])
    o_ref[...] = y


def _ffn(x, g, wg, wu, wd, final_g=None, mix=None):
    t = x.shape[0]
    final = final_g is not None
    ins = [x, g.reshape(1, D_MODEL), wg.astype(BF16), wu.astype(BF16), wd.astype(BF16)]
    specs = [pl.BlockSpec((TM, D_MODEL), lambda i: (i, 0)), _full((1, D_MODEL)),
             _resident((D_MODEL, D_FF)), _resident((D_MODEL, D_FF)), _resident((D_FF, D_MODEL))]
    if mix is not None:
        a, b, w_out = mix
        na, nb = a.shape[1], b.shape[1]
        ins += [a, b, w_out[:na].astype(BF16), w_out[na:].astype(BF16)]
        specs += [pl.BlockSpec((TM, na), lambda i: (i, 0)), pl.BlockSpec((TM, nb), lambda i: (i, 0)),
                  _resident((na, D_MODEL)), _resident((nb, D_MODEL))]
    if final:
        ins.append(final_g.reshape(1, D_MODEL))
        specs.append(_full((1, D_MODEL)))
    return pl.pallas_call(
        functools.partial(_ffn_kernel, final=final, mixed=mix is not None),
        grid=(t // TM,),
        in_specs=specs,
        out_specs=pl.BlockSpec((TM, D_MODEL), lambda i: (i, 0)),
        out_shape=jax.ShapeDtypeStruct((t, D_MODEL), F32),
        compiler_params=_params("parallel"),
        name="ffn_final" if final else "ffn",
    )(*ins)


def _rope_tiles(z, parts_ref, spread_ref, shift, period):
    tab = _dot(parts_ref[...], spread_ref[...])
    cs, sn = tab[:, :LANES], tab[:, LANES:]
    is_x1 = (_iota(cs.shape, 1) & (period - 1)) < shift
    outs = []
    for c in range(z.shape[1] // LANES):
        zt = z[:, c * LANES:(c + 1) * LANES]
        up = pltpu.roll(zt, LANES - shift, axis=1)
        dn = pltpu.roll(zt, shift, axis=1)
        outs.append(zt * cs + jnp.where(is_x1, up, dn) * sn)
    return outs[0] if len(outs) == 1 else jnp.concatenate(outs, axis=1)


def _head_row(z, col, width, aux):
    tile = z[:, (col // LANES) * LANES:(col // LANES + 1) * LANES]
    off = col % LANES
    if off:
        tile = pltpu.roll(tile, LANES - off, axis=1)
    lane = _iota(tile.shape, 1)
    return jnp.where(lane < width, tile, aux)


def _seq_pos(sblk, rows):
    return sblk * rows + _iota((rows, LANES), 0)


def _even_proj_kernel(x_ref, g_ref, rp_ref, sp_ref, wa_ref, wq_ref, wp_ref, wr_ref, wg_ref,
                      a_ref, qn_ref, qr_ref, kc_ref, vc_ref, vs_ref, vw_ref, ks_ref, kw_ref, gt_ref,
                      *, n_sblk):
    sblk = pl.program_id(0) % n_sblk
    h = _rms(x_ref[...], g_ref[...]).astype(BF16)
    lane = _iota((TM, LANES), 1)
    ones_col = jnp.where(lane >= HEAD_DIM, 1.0, 0.0)
    slc_onehot = jnp.where(lane == AUX_LANE + (_seq_pos(sblk, TM) // SLC_BLOCK), 1.0, 0.0)

    a_ref[...] = _dot(h, wa_ref[...])
    zq = _dot(h, wq_ref[...]) * QK_SCALE
    zr = _rope_tiles(zq, rp_ref, sp_ref, ROT_DIM // 2, HEAD_DIM)
    for hh in range(B_HEADS):
        qn_ref[0, hh] = _head_row(zq, hh * HEAD_DIM, HEAD_DIM, 0.0).astype(BF16)
        qr_ref[0, hh] = _head_row(zr, hh * HEAD_DIM, HEAD_DIM, 0.0).astype(BF16)
    zp = _dot(h, wp_ref[...])
    for g in range(B_KV_HEADS):
        kc_ref[0, g] = _head_row(zp, g * HEAD_DIM, HEAD_DIM, 0.0)
        vc_ref[0, g] = _head_row(zp, (B_KV_HEADS + g) * HEAD_DIM, HEAD_DIM, 0.0)
    zk = _rope_tiles(_dot(h, wr_ref[...]), rp_ref, sp_ref, ROT_DIM // 2, HEAD_DIM)
    for g in range(B_KV_HEADS):
        vs_ref[0, g] = _head_row(zp, (2 * B_KV_HEADS + g) * HEAD_DIM, HEAD_DIM, ones_col).astype(BF16)
        vw_ref[0, g] = _head_row(zp, (3 * B_KV_HEADS + g) * HEAD_DIM, HEAD_DIM, ones_col).astype(BF16)
        ks_ref[0, g] = _head_row(zk, g * HEAD_DIM, HEAD_DIM, slc_onehot).astype(BF16)
        kw_ref[0, g] = _head_row(zk, (B_KV_HEADS + g) * HEAD_DIM, HEAD_DIM, 0.0).astype(BF16)
    gt_ref[...] = jax.nn.sigmoid(_dot(h, wg_ref[...]))


def _even_proj(x, g, tab64, weights, bsz, seq):
    t = x.shape[0]
    n_sblk = seq // TM
    tok = lambda n: pl.BlockSpec((TM, n), lambda i: (i, 0))
    heads = lambda n: pl.BlockSpec((1, n, TM, LANES), lambda i: (i // n_sblk, 0, i % n_sblk, 0))
    hshape = lambda n: jax.ShapeDtypeStruct((bsz, n, seq, LANES), BF16)
    return pl.pallas_call(
        functools.partial(_even_proj_kernel, n_sblk=n_sblk),
        grid=(t // TM,),
        in_specs=[tok(D_MODEL), _full((1, D_MODEL)), tok(LANES), _full(tab64[1].shape)]
                 + [_resident(w.shape) for w in weights],
        out_specs=[tok(2 * A_WIDTH), heads(B_HEADS), heads(B_HEADS), heads(B_KV_HEADS), heads(B_KV_HEADS),
                   heads(B_KV_HEADS), heads(B_KV_HEADS), heads(B_KV_HEADS), heads(B_KV_HEADS), tok(LANES)],
        out_shape=[jax.ShapeDtypeStruct((t, 2 * A_WIDTH), F32), hshape(B_HEADS), hshape(B_HEADS),
                   jax.ShapeDtypeStruct((bsz, B_KV_HEADS, seq, LANES), F32),
                   jax.ShapeDtypeStruct((bsz, B_KV_HEADS, seq, LANES), F32),
                   hshape(B_KV_HEADS), hshape(B_KV_HEADS), hshape(B_KV_HEADS), hshape(B_KV_HEADS),
                   jax.ShapeDtypeStruct((t, LANES), F32)],
        compiler_params=_params("parallel"),
        name="even_in_proj",
    )(x, g.reshape(1, D_MODEL), *tab64, *[w.astype(BF16) for w in weights])


OD_ROPE = C_WIDTH + 2 * HEAD_DIM + 2 * D_WIDTH
OD_KD = C_WIDTH + 2 * HEAD_DIM + D_WIDTH
OD_ROPEI = 2 * LANES
OD_PLAIN = 2 * HEAD_DIM + D_WIDTH


def _odd_proj_kernel(x_ref, g_ref, rp64_ref, sp64_ref, rp32_ref, sp32_ref, wr_ref, wi_ref, wp_ref,
                     qc_ref, kc_ref, vc_ref, qi_ref, ki_ref, wt_ref, qd_ref, kd_ref, vd_ref, km_ref,
                     *, n_sblk):
    sblk = pl.program_id(0) % n_sblk
    h = _rms(x_ref[...], g_ref[...]).astype(BF16)
    lane = _iota((TM, LANES), 1)
    ones_col = jnp.where(lane >= HEAD_DIM, 1.0, 0.0)
    blk = _seq_pos(sblk, TM) // MOBA_BLOCK

    zr = _rope_tiles(_dot(h, wr_ref[...]), rp64_ref, sp64_ref, ROT_DIM // 2, HEAD_DIM)
    for hh in range(C_HEADS):
        qc_ref[0, hh] = (_head_row(zr, hh * HEAD_DIM, HEAD_DIM, 0.0) * QK_SCALE).astype(BF16)
    kc_ref[0] = _head_row(zr, C_WIDTH, HEAD_DIM, 0.0).astype(BF16)
    for hh in range(D_HEADS):
        qd = _head_row(zr, C_WIDTH + 2 * HEAD_DIM + hh * HEAD_DIM, HEAD_DIM, 0.0)
        qd_ref[0, hh] = (qd * QK_SCALE).astype(BF16)
        onehot = jnp.where(lane == AUX_LANE + hh * (LANES - AUX_LANE) // D_HEADS + blk, 1.0, 0.0)
        kd_ref[0, hh] = _head_row(zr, OD_KD + hh * HEAD_DIM, HEAD_DIM, onehot).astype(BF16)
    zkd = zr[:, OD_KD:]
    n_mb = TM // MOBA_BLOCK
    km_ref[0] = jnp.sum(zkd.reshape(n_mb, MOBA_BLOCK, D_WIDTH), axis=1) * (1.0 / MOBA_BLOCK)

    zi = _rope_tiles(_dot(h, wi_ref[...]), rp32_ref, sp32_ref, IDX_ROT // 2, IDX_DIM)
    for hh in range(IDX_HEADS):
        qi_ref[0, hh] = _head_row(zi, hh * IDX_DIM, IDX_DIM, 0.0).astype(BF16)
    ki_ref[0] = _head_row(zi, IDX_HEADS * IDX_DIM, IDX_DIM, 0.0).astype(BF16)

    zp = _dot(h, wp_ref[...])
    vc_ref[0] = _head_row(zp, 0, HEAD_DIM, ones_col).astype(BF16)
    wt_ref[0] = pltpu.roll(zp[:, :LANES], LANES - HEAD_DIM, axis=1).T[:IDX_HEADS]
    for hh in range(D_HEADS):
        vd_ref[0, hh] = _head_row(zp, 2 * HEAD_DIM + hh * HEAD_DIM, HEAD_DIM, ones_col).astype(BF16)


def _odd_proj(x, g, tab64, tab32, weights, bsz, seq):
    t = x.shape[0]
    n_sblk = seq // TM
    tok = lambda n: pl.BlockSpec((TM, n), lambda i: (i, 0))
    tabs = [tok(LANES), _full(tab64[1].shape), tok(LANES), _full(tab32[1].shape)]
    heads = lambda n: pl.BlockSpec((1, n, TM, LANES), lambda i: (i // n_sblk, 0, i % n_sblk, 0))
    single = pl.BlockSpec((1, TM, LANES), lambda i: (i // n_sblk, i % n_sblk, 0))
    hshape = lambda n: jax.ShapeDtypeStruct((bsz, n, seq, LANES), BF16)
    sshape = jax.ShapeDtypeStruct((bsz, seq, LANES), BF16)
    return pl.pallas_call(
        functools.partial(_odd_proj_kernel, n_sblk=n_sblk),
        grid=(t // TM,),
        in_specs=[tok(D_MODEL), _full((1, D_MODEL))] + tabs + [_resident(w.shape) for w in weights],
        out_specs=[heads(C_HEADS), single, single, heads(IDX_HEADS), single,
                   pl.BlockSpec((1, IDX_HEADS, TM), lambda i: (i // n_sblk, 0, i % n_sblk)),
                   heads(D_HEADS), heads(D_HEADS), heads(D_HEADS),
                   pl.BlockSpec((1, TM // MOBA_BLOCK, D_WIDTH), lambda i: (i, 0, 0))],
        out_shape=[hshape(C_HEADS), sshape, sshape, hshape(IDX_HEADS), sshape,
                   jax.ShapeDtypeStruct((bsz, IDX_HEADS, seq), F32),
                   hshape(D_HEADS), hshape(D_HEADS), hshape(D_HEADS),
                   jax.ShapeDtypeStruct((t // TM, TM // MOBA_BLOCK, D_WIDTH), F32)],
        compiler_params=_params("parallel"),
        name="odd_in_proj",
    )(x, g.reshape(1, D_MODEL), *tab64, *tab32, *[w.astype(BF16) for w in weights])


def _rope_tables(positions, rot_dim, period):
    half = rot_dim // 2
    inv_freq = ROPE_THETA ** (-jnp.arange(0, rot_dim, 2, dtype=F32) / rot_dim)
    ang = positions.astype(F32).reshape(-1, 1) * inv_freq

    def parts(x):
        hi = x.astype(BF16)
        mid = (x - hi.astype(F32)).astype(BF16)
        lo = (x - hi.astype(F32) - mid.astype(F32)).astype(BF16)
        return [hi, mid, lo]

    t = ang.shape[0]
    cols = parts(jnp.cos(ang)) + parts(jnp.sin(ang)) + [jnp.ones((t, 1), BF16)]
    rows = jnp.concatenate(cols, axis=1)
    rows = jnp.pad(rows, ((0, 0), (0, LANES - rows.shape[1])))
    spread = np.zeros((LANES, 2 * LANES), np.float32)
    for lane in range(LANES):
        j = lane % period
        for k in range(3):
            if j < 2 * half:
                spread[k * half + j % half, lane] = 1.0
                spread[3 * half + k * half + j % half, LANES + lane] = -1.0 if j < half else 1.0
        if j >= 2 * half:
            spread[6 * half, lane] = 1.0
    return rows, jnp.asarray(spread, dtype=BF16)


def _gmlp_kernel(a_ref, n_ref, w_ref, b_ref, o_ref):
    causal = _iota((A_CHUNK, A_CHUNK), 1) <= _iota((A_CHUNK, A_CHUNK), 0)
    lane_group = _iota((A_CHUNK, A_WIDTH), 1) // HEAD_DIM
    ws = [jnp.where(causal, w_ref[g], 0.0).astype(BF16) for g in range(A_GROUPS)]
    for c in range(TM // A_CHUNK):
        rs = slice(c * A_CHUNK, (c + 1) * A_CHUNK)
        z = _gelu(a_ref[rs, :])
        u = z[:, :A_WIDTH]
        v = _rms(z[:, A_WIDTH:], n_ref[...]).astype(BF16)
        mixed = jnp.zeros((A_CHUNK, A_WIDTH), F32)
        for g in range(A_GROUPS):
            mixed = jnp.where(lane_group == g, _dot(ws[g], v) + b_ref[g], mixed)
        o_ref[rs, :] = (u * mixed).astype(BF16)


def _gmlp(a_in, sgu_norm, sgu_w, sgu_b):
    t = a_in.shape[0]
    return pl.pallas_call(
        _gmlp_kernel,
        grid=(t // TM,),
        in_specs=[pl.BlockSpec((TM, 2 * A_WIDTH), lambda i: (i, 0)),
                  _full((1, A_WIDTH)), _full((A_GROUPS, A_CHUNK, A_CHUNK)),
                  _full((A_GROUPS, A_CHUNK, 1))],
        out_specs=pl.BlockSpec((TM, A_WIDTH), lambda i: (i, 0)),
        out_shape=jax.ShapeDtypeStruct((t, A_WIDTH), BF16),
        compiler_params=_params("parallel"),
        name="gmlp",
    )(a_in, sgu_norm.reshape(1, A_WIDTH), sgu_w, sgu_b.reshape(A_GROUPS, A_CHUNK, 1))


def _compress_one(x_ref, pos_ref, w1_ref, w2_ref, o_ref):
    n_rows = x_ref.shape[2] // CMP_STRIDE
    first = [jnp.zeros((n_rows, CMP_HIDDEN), F32) for _ in range(B_KV_HEADS)]
    second = [jnp.zeros((n_rows, CMP_HIDDEN), F32) for _ in range(B_KV_HEADS)]
    for r in range(CMP_STRIDE):
        for g in range(B_KV_HEADS):
            xg = x_ref[0, g, pl.ds(r, n_rows, stride=CMP_STRIDE), :]
            first[g] = first[g] + _dot((xg + pos_ref[r:r + 1, :]).astype(BF16), w1_ref[r])
            second[g] = second[g] + _dot((xg + pos_ref[CMP_STRIDE + r:CMP_STRIDE + r + 1, :]).astype(BF16),
                                         w1_ref[CMP_STRIDE + r])
    for g in range(B_KV_HEADS):
        hid = _gelu(first[g] + pltpu.roll(second[g], n_rows - 1, axis=0))
        out = _dot(hid.astype(BF16), w2_ref[...])
        row = _iota(out.shape, 0)
        o_ref[0, g] = jnp.where(row < n_rows - 1, out, 0.0).astype(BF16)


def _compress_kernel(kc_ref, vc_ref, pk_ref, w1k_ref, w2k_ref, pv_ref, w1v_ref, w2v_ref, ko_ref, vo_ref):
    _compress_one(kc_ref, pk_ref, w1k_ref, w2k_ref, ko_ref)
    _compress_one(vc_ref, pv_ref, w1v_ref, w2v_ref, vo_ref)


def _compress(kc, vc, pos_k, w1_k, w2_k, pos_v, w1_v, w2_v):
    bsz, _, seq, _ = kc.shape
    nrow = seq // CMP_STRIDE
    lane_pad = LANES - HEAD_DIM

    def prep(pos, w1, w2):
        w1p = jnp.pad(w1.reshape(CMP_LEN, HEAD_DIM, CMP_HIDDEN), ((0, 0), (0, lane_pad), (0, 0)))
        return (jnp.pad(pos, ((0, 0), (0, lane_pad))), w1p.astype(BF16),
                jnp.pad(w2, ((0, 0), (0, lane_pad))).astype(BF16))

    pk, w1k, w2k = prep(pos_k, w1_k, w2_k)
    pv, w1v, w2v = prep(pos_v, w1_v, w2_v)
    blk_in = pl.BlockSpec((1, B_KV_HEADS, seq, LANES), lambda b: (b, 0, 0, 0))
    blk_out = pl.BlockSpec((1, B_KV_HEADS, nrow, LANES), lambda b: (b, 0, 0, 0))
    wspecs = [_full((CMP_LEN, LANES)), _full((CMP_LEN, LANES, CMP_HIDDEN)), _full((CMP_HIDDEN, LANES))]
    return pl.pallas_call(
        _compress_kernel,
        grid=(bsz,),
        in_specs=[blk_in, blk_in] + wspecs + wspecs,
        out_specs=[blk_out, blk_out],
        out_shape=[jax.ShapeDtypeStruct((bsz, B_KV_HEADS, nrow, LANES), BF16)] * 2,
        compiler_params=_params("parallel"),
        name="nsa_compress",
    )(kc, vc, pk, w1k, w2k, pv, w1v, w2v)


def _rank_lt(vals, axis, n, k):
    j = _iota(vals.shape, axis)
    rank = jnp.zeros(vals.shape, F32)
    for jp in range(n):
        row = lax.slice_in_dim(vals, jp, jp + 1, axis=axis)
        beats = jnp.where(row > vals, 1.0, jnp.where(row == vals, jnp.where(j > jp, 1.0, 0.0), 0.0))
        rank = rank + beats
    return jnp.where(rank < k, 1.0, 0.0)


def _penalty_rows(pen_t):
    n, nq = pen_t.shape
    parts = [jnp.zeros((AUX_LANE, nq), F32), pen_t]
    if LANES - AUX_LANE - n:
        parts.append(jnp.zeros((LANES - AUX_LANE - n, nq), F32))
    full = jnp.concatenate(parts, axis=0)
    halves = [full[:, c * LANES:(c + 1) * LANES].T for c in range(nq // LANES)]
    return halves[0] if len(halves) == 1 else jnp.concatenate(halves, axis=0)


def _swap_halves(acc):
    return pltpu.roll(acc, HEAD_DIM, axis=1)


def _store_normalized(o_ref, accs):
    lane = _iota((QB, LANES), 1)
    for p in range(len(accs) // 2):
        even, odd = accs[2 * p], accs[2 * p + 1]
        both = jnp.where(lane < HEAD_DIM, even / _swap_halves(even), _swap_halves(odd) / odd)
        o_ref[0, :, p * LANES:(p + 1) * LANES] = both.astype(BF16)


def _nsa_kernel(qr_ref, qn_ref, gt_ref, kc_ref, vc_ref, ks_ref, vs_ref, kw_ref, vw_ref, ovt_ref, ex_ref,
                o_ref, s_scr, q_scr, cmp_scr, m_scr, acc_scr):
    i = pl.program_id(1)
    q0 = i * QB
    rows = B_GROUP * QB
    n_slc = ovt_ref.shape[0]
    n_g = B_KV_HEADS
    gates = gt_ref[0]
    tq2 = q0 + _iota((QB, KT), 0)
    key2 = _iota((QB, KT), 1)

    def masked(s, ok):
        n = s.shape[1]
        return jnp.where(ok[None], s.reshape(B_GROUP, QB, n), NEG_INF).reshape(rows, n)

    m_c = ((_iota((QB, LANES), 1) * CMP_STRIDE + (CMP_LEN - 1)) <= q0 + _iota((QB, LANES), 0))[None]
    for g in range(n_g):
        hs = slice(g * B_GROUP, (g + 1) * B_GROUP)
        qn = qn_ref[0, hs].reshape(rows, LANES)
        s_c = _dot_nt(qn, kc_ref[0, g]).reshape(B_GROUP, QB, LANES)
        sm = jnp.where(m_c, s_c, NEG_INF)
        e = jnp.where(m_c, jnp.exp2(sm - jnp.max(sm, axis=-1, keepdims=True)), 0.0)
        den = jnp.sum(e, axis=-1, keepdims=True)
        p_cb = (e / jnp.where(den > 0.0, den, 1.0)).reshape(rows, LANES).astype(BF16)
        cmp_scr[g] = _dot(p_cb, vc_ref[0, g])

        imp = jnp.zeros((n_slc, QB), F32)
        for r in range(B_GROUP):
            imp = imp + _dot_nt(ovt_ref[...], p_cb[r * QB:(r + 1) * QB])
        jb = _iota((n_slc, QB), 0)
        tq = q0 + _iota((n_slc, QB), 1)
        forced = (jb == 0) | (jb == (tq >> 6))
        imp = jnp.where(jb * SLC_BLOCK <= tq, jnp.where(forced, FORCE_SCORE, imp), NEG_INF)
        pen = _penalty_rows((_rank_lt(imp, 0, n_slc, SLC_TOPN) - 1.0) * (-NEG_INF))
        qr = qr_ref[0, hs].astype(F32) + pen[None]
        q_scr[g] = qr.reshape(rows, LANES).astype(BF16)

    n_wt = (WINDOW + QB + KT - 1) // KT
    w_off = [QB - (n_wt - c) * KT for c in range(n_wt)]
    w_base = [pl.multiple_of(jnp.maximum(q0 + off, 0), min(QB, KT)) for off in w_off]

    def window_scores(interior):
        w_mx = [jnp.full((rows, LANES), NEG_INF, F32) for _ in range(n_g)]
        for c in range(n_wt):
            key = key2 + w_base[c]
            ok = None
            if not interior:
                ok = (key > tq2 - WINDOW) & (key <= jnp.minimum(tq2, q0 + (w_off[c] + KT - 1)))
            elif w_off[c] < QB - WINDOW and w_off[c] + KT - 1 > 0:
                ok = (key > tq2 - WINDOW) & (key <= tq2)
            elif w_off[c] < QB - WINDOW:
                ok = key > tq2 - WINDOW
            elif w_off[c] + KT - 1 > 0:
                ok = key <= tq2
            for g in range(n_g):
                s = _dot_nt(q_scr[g], kw_ref[0, g, pl.ds(w_base[c], KT), :])
                if ok is not None:
                    s = masked(s, ok)
                s_scr[c, g] = s
                w_mx[g] = jnp.maximum(w_mx[g], jnp.maximum(s[:, :LANES], s[:, LANES:]))
        for g in range(n_g):
            m_scr[g] = w_mx[g]

    first_interior = (n_wt * KT - QB + QB - 1) // QB
    pl.when(i >= first_interior)(functools.partial(window_scores, True))
    pl.when(i < first_interior)(functools.partial(window_scores, False))
    for g in range(n_g):
        w_max = jnp.max(m_scr[g], axis=-1, keepdims=True)
        acc_w = jnp.zeros((rows, LANES), F32)
        for c in range(n_wt):
            p = jnp.exp2(s_scr[c, g] - w_max).astype(BF16)
            acc_w = acc_w + _dot(p, vw_ref[0, g, pl.ds(w_base[c], KT), :])
        acc_scr[g] = acc_w

    n_tiles = (q0 + QB + KT - 1) // KT
    last = n_tiles - 1
    for g in range(n_g):
        m_scr[g] = jnp.full((rows, LANES), NEG_INF, F32)

    def sel_scores(c, causal):
        base = pl.multiple_of(c * KT, KT)
        for g in range(n_g):
            s = _dot_nt(q_scr[g], ks_ref[0, g, pl.ds(base, KT), :])
            if causal:
                s = masked(s, key2 + base <= tq2)
            s_scr[c, g] = s
            m_scr[g] = jnp.maximum(m_scr[g], jnp.maximum(s[:, :LANES], s[:, LANES:]))

    def sel_pass1(c, _):
        sel_scores(c, False)
        return 0

    lax.fori_loop(0, last, sel_pass1, 0)
    sel_scores(last, True)

    m_sel = [jnp.max(m_scr[g], axis=-1, keepdims=True) for g in range(n_g)]
    for g in range(n_g):
        m_scr[g] = jnp.zeros((rows, LANES), F32)

    def sel_pass2(c, _):
        base = pl.multiple_of(c * KT, KT)
        for g in range(n_g):
            p = jnp.exp2(s_scr[c, g] - m_sel[g]).astype(BF16)
            m_scr[g] += _dot(p, vs_ref[0, g, pl.ds(base, KT), :])
        return 0

    lax.fori_loop(0, n_tiles, sel_pass2, 0)

    g_hi = gates.astype(BF16)
    g_lo = (gates - g_hi.astype(F32)).astype(BF16)
    g_parts = jnp.concatenate([g_hi, g_lo], axis=1)
    lane = _iota((QB, LANES), 1)
    pair_cols = 2 * N_BRANCH * LANES
    for p in range(B_HEADS // 2):
        g, r = (2 * p) // B_GROUP, (2 * p) % B_GROUP
        ev, od = slice(r * QB, (r + 1) * QB), slice((r + 1) * QB, (r + 2) * QB)
        spread = _dot(g_parts, ex_ref[:, p * pair_cols:(p + 1) * pair_cols])
        gate = [spread[:, k * LANES:(k + 1) * LANES] for k in range(2 * N_BRANCH)]
        sel_e, win_e = m_scr[g, ev, :], acc_scr[g, ev, :]
        sel_o, win_o = m_scr[g, od, :], acc_scr[g, od, :]
        lower = (gate[0] * cmp_scr[g, ev, :] + gate[1] * (sel_e / _swap_halves(sel_e))
                 + gate[2] * (win_e / _swap_halves(win_e)))
        upper = (gate[3] * _swap_halves(cmp_scr[g, od, :]) + gate[4] * (_swap_halves(sel_o) / sel_o)
                 + gate[5] * (_swap_halves(win_o) / win_o))
        o_ref[0, :, p * LANES:(p + 1) * LANES] = jnp.where(lane < HEAD_DIM, lower, upper).astype(BF16)


def _overlap_t(n_rows, n_slc):
    n = np.arange(n_rows)
    c0 = n * CMP_STRIDE
    s0 = np.arange(n_slc) * SLC_BLOCK
    m = (c0[None, :] < s0[:, None] + SLC_BLOCK) & (c0[None, :] + CMP_LEN > s0[:, None])
    m = m & (n[None, :] < n_rows - 1)
    return jnp.asarray(m, dtype=BF16)


def _nsa(q_rot, q_raw, gates, k_cmp, v_cmp, k_slc, v_slc, k_win, v_win):
    bsz, _, seq, _ = q_rot.shape
    n_qb = seq // QB
    n_slc = seq // SLC_BLOCK
    n_cmp_rows = k_cmp.shape[2]
    ovt = _overlap_t(n_cmp_rows, n_slc)
    n_gate = B_HEADS * N_BRANCH
    spread = np.zeros((2 * LANES, n_gate * LANES), np.float32)
    for c in range(n_gate):
        spread[[c, LANES + c], c * LANES:(c + 1) * LANES] = 1.0
    spread = jnp.asarray(spread, dtype=BF16)
    q_spec = pl.BlockSpec((1, B_HEADS, QB, LANES), lambda b, i: (b, 0, i, 0))
    cmp_spec = pl.BlockSpec((1, B_KV_HEADS, n_cmp_rows, LANES), lambda b, i: (b, 0, 0, 0))
    kv_spec = pl.BlockSpec((1, B_KV_HEADS, seq, LANES), lambda b, i: (b, 0, 0, 0), pipeline_mode=pl.Buffered(1))
    rows = B_GROUP * QB
    return pl.pallas_call(
        _nsa_kernel,
        grid=(bsz, n_qb),
        in_specs=[q_spec, q_spec, pl.BlockSpec((1, QB, LANES), lambda b, i: (b, i, 0)),
                  cmp_spec, cmp_spec, kv_spec, kv_spec, kv_spec, kv_spec, _full(ovt.shape),
                  _resident(spread.shape)],
        out_specs=pl.BlockSpec((1, QB, B_WIDTH), lambda b, i: (b, i, 0)),
        out_shape=jax.ShapeDtypeStruct((bsz, seq, B_WIDTH), BF16),
        scratch_shapes=[pltpu.VMEM((seq // KT, B_KV_HEADS, rows, KT), F32),
                        pltpu.VMEM((B_KV_HEADS, rows, LANES), BF16),
                        pltpu.VMEM((B_KV_HEADS, rows, LANES), F32),
                        pltpu.VMEM((B_KV_HEADS, rows, LANES), F32),
                        pltpu.VMEM((B_KV_HEADS, rows, LANES), F32)],
        compiler_params=_params("parallel", "arbitrary"),
        name="nsa_attention",
    )(q_rot, q_raw, gates, k_cmp, v_cmp, k_slc, v_slc, k_win, v_win, ovt, spread)


def _dsa_kernel(q_ref, k_ref, v_ref, qi_ref, ki_ref, wi_ref, o_ref,
                sc_scr, bias_scr, s_scr, mx_scr, acc_scr, *, top_k):
    i = pl.program_id(1)
    q0 = i * QB
    n_tiles = (q0 + QB + KT - 1) // KT
    tile = (KT, QB)
    key_sub = _iota(tile, 0)
    tq = q0 + _iota(tile, 1)
    w_idx = wi_ref[0]
    qi = qi_ref[0].reshape(IDX_HEADS * QB, LANES)
    n_sub = KT // 8

    def fold(x, op):
        return op(op(x.reshape(n_sub // 4, 4, 8, QB), axis=0), axis=0)

    w_s = w_idx * (IDX_DIM ** -0.5 * IDX_HEADS ** -0.5)
    last = n_tiles - 1

    def score_tile(c, carry, masked):
        mn, mx, zge, zgt = carry
        base = pl.multiple_of(c * KT, KT)
        lg = _dot_nt(ki_ref[0, pl.ds(base, KT), :], qi)
        sc = jnp.zeros(tile, F32)
        for h in range(IDX_HEADS):
            sc = sc + w_s[h:h + 1, :] * jnp.maximum(lg[:, h * QB:(h + 1) * QB], 0.0)
        if masked:
            valid = key_sub + base <= tq
            lowest = jnp.where(valid, sc, -NEG_INF)
            sc = jnp.where(valid, sc, NEG_INF)
        else:
            lowest = sc
        sc_scr[c] = sc
        return (jnp.minimum(mn, fold(lowest, jnp.min)), jnp.maximum(mx, fold(sc, jnp.max)),
                zge + fold(jnp.where(sc >= 0.0, 1.0, 0.0), jnp.sum),
                zgt + fold(jnp.where(sc > 0.0, 1.0, 0.0), jnp.sum))

    stats = lax.fori_loop(0, last, lambda c, carry: score_tile(c, carry, False),
                          (jnp.full((8, QB), -NEG_INF, F32), jnp.full((8, QB), NEG_INF, F32),
                           jnp.zeros((8, QB), F32), jnp.zeros((8, QB), F32)))
    mn, mx, zge, zgt = score_tile(last, stats, True)
    mn = jnp.min(mn, axis=0, keepdims=True)
    mx = jnp.max(mx, axis=0, keepdims=True)
    zge = jnp.sum(zge, axis=0, keepdims=True)
    zgt = jnp.sum(zgt, axis=0, keepdims=True)
    kf = jnp.float32(top_k)
    n_valid = (tq[0:1, :] + 1).astype(F32)
    all_taken = n_valid <= kf
    zero_tied = jnp.logical_and(zgt < kf, zge >= kf)
    positive = zgt >= kf

    def search_step(carry, kind):
        lo_, hi_, clo, chi = carry
        if kind == "halve":
            frac = 0.5
        else:
            frac = jnp.clip((clo - kf + 0.5) / jnp.maximum(clo - chi, 1.0), 1.0 / 64, 63.0 / 64)
            if kind == "mixed":
                frac = jnp.where(clo - chi > SEARCH_WIDE, 0.5, frac)
        mid = lo_ + (hi_ - lo_) * frac
        cnt = lax.fori_loop(0, n_tiles,
                            lambda c, a: a + fold(jnp.where(sc_scr[c] >= mid, 1.0, 0.0), jnp.sum),
                            jnp.zeros((8, QB), F32))
        cm = jnp.sum(cnt, axis=0, keepdims=True)
        ge = cm >= kf
        return (jnp.where(ge, mid, lo_), jnp.where(ge, hi_, mid), jnp.where(ge, cm, clo), jnp.where(ge, chi, cm))

    def round_cond(carry):
        r, _, _, clo, chi = carry
        settled = jnp.where(all_taken, 1.0, jnp.where(zero_tied, 1.0,
                            jnp.where(clo == kf, 1.0, jnp.where(chi == kf - 1.0, 1.0, 0.0))))
        return jnp.logical_and(r < SEARCH_ROUNDS, jnp.min(settled) < 1.0)

    def round_body(carry):
        st = carry[1:]
        for kind in SEARCH_PATTERN:
            st = search_step(st, kind)
        return (carry[0] + 1,) + st

    top = mx + jnp.abs(mx) * 1e-3 + 1.0
    init = (jnp.int32(0), jnp.where(positive, 0.0, mn), jnp.where(positive, top, 0.0),
            jnp.where(positive, zge, n_valid), jnp.where(positive, 0.0, zge))
    _, lo, hi, c_lo, c_hi = lax.while_loop(round_cond, round_body, init)
    exact_cut = c_lo == kf

    def thr_body(c, vm):
        sc = sc_scr[c]
        return jnp.maximum(vm, fold(jnp.where(sc < hi, sc, NEG_INF), jnp.max))

    vm = lax.fori_loop(0, n_tiles, thr_body, jnp.full((8, QB), NEG_INF, F32))
    thr = jnp.where(zero_tied, 0.0, jnp.max(vm, axis=0, keepdims=True))
    need = kf - jnp.where(zero_tied, zgt, c_hi)
    tri = jnp.where(_iota((KT, KT), 1) <= _iota((KT, KT), 0), 1.0, 0.0).astype(BF16)

    def mask_body(c, seen):
        base = c * KT
        sc = sc_scr[c]
        eq = sc == thr
        prefix = _dot(tri, jnp.where(eq, 1.0, 0.0).astype(BF16)) + seen
        tied = jnp.where(eq, jnp.where(prefix <= need, 1.0, 0.0), 0.0)
        by_thr = jnp.where(sc > thr, 1.0, tied)
        chosen = jnp.where(exact_cut, jnp.where(sc >= lo, 1.0, 0.0), by_thr)
        valid = jnp.where(key_sub + base <= tq, 1.0, 0.0)
        bias_t = (jnp.where(all_taken, valid, chosen) - 1.0) * (-NEG_INF)
        for qh in range(QB // LANES):
            for kh in range(KT // LANES):
                bias_scr[c, qh * LANES:(qh + 1) * LANES, kh * LANES:(kh + 1) * LANES] = (
                    bias_t[kh * LANES:(kh + 1) * LANES, qh * LANES:(qh + 1) * LANES].T)
        return prefix[KT - 1:KT, :]

    lax.fori_loop(0, n_tiles, mask_body, jnp.zeros((1, QB), F32))

    rows = C_HEADS * QB
    q = q_ref[0].reshape(rows, LANES)
    mx_scr[...] = jnp.full((rows, LANES), NEG_INF, F32)

    def att_pass1(c, _):
        base = pl.multiple_of(c * KT, KT)
        s = _dot_nt(q, k_ref[0, pl.ds(base, KT), :])
        s = (s.reshape(C_HEADS, QB, KT) + bias_scr[c][None]).reshape(rows, KT)
        s_scr[c] = s
        mx_scr[...] = jnp.maximum(mx_scr[...], jnp.maximum(s[:, :LANES], s[:, LANES:]))
        return 0

    lax.fori_loop(0, n_tiles, att_pass1, 0)
    m = jnp.max(mx_scr[...], axis=-1, keepdims=True)
    acc_scr[...] = jnp.zeros((rows, LANES), F32)

    def att_pass2(c, _):
        base = pl.multiple_of(c * KT, KT)
        p = jnp.exp2(s_scr[c] - m).astype(BF16)
        acc_scr[...] += _dot(p, v_ref[0, pl.ds(base, KT), :])
        return 0

    lax.fori_loop(0, n_tiles, att_pass2, 0)
    _store_normalized(o_ref, [acc_scr[h * QB:(h + 1) * QB, :] for h in range(C_HEADS)])


def _dsa(q, k, v, q_idx, k_idx, w_idx_t):
    bsz, _, seq, _ = q.shape
    n_qb = seq // QB
    top_k = min(DSA_TOPK, seq // 4)
    rows = C_HEADS * QB
    single = pl.BlockSpec((1, seq, LANES), lambda b, i: (b, 0, 0))
    return pl.pallas_call(
        functools.partial(_dsa_kernel, top_k=top_k),
        grid=(bsz, n_qb),
        in_specs=[pl.BlockSpec((1, C_HEADS, QB, LANES), lambda b, i: (b, 0, i, 0)),
                  single, single,
                  pl.BlockSpec((1, IDX_HEADS, QB, LANES), lambda b, i: (b, 0, i, 0)),
                  single,
                  pl.BlockSpec((1, IDX_HEADS, QB), lambda b, i: (b, 0, i))],
        out_specs=pl.BlockSpec((1, QB, C_WIDTH), lambda b, i: (b, i, 0)),
        out_shape=jax.ShapeDtypeStruct((bsz, seq, C_WIDTH), BF16),
        scratch_shapes=[pltpu.VMEM((seq // KT, KT, QB), F32),
                        pltpu.VMEM((seq // KT, QB, KT), F32),
                        pltpu.VMEM((seq // KT, rows, KT), F32),
                        pltpu.VMEM((rows, LANES), F32),
                        pltpu.VMEM((rows, LANES), F32)],
        compiler_params=_params("parallel", "arbitrary"),
        name="dsa_attention",
    )(q, k, v, q_idx, k_idx, w_idx_t)


def _moba_kernel(q_ref, k_ref, v_ref, km_ref, o_ref, s_scr, m_scr, acc_scr, *, top_k):
    i = pl.program_id(1)
    q0 = i * QB
    own = q0 // MOBA_BLOCK
    n_blk = km_ref.shape[2]
    q = q_ref[0]
    gate = _bdot_nt(km_ref[0], q) * (1.0 / QK_SCALE)
    jb = _iota(gate.shape, 1)
    past = jb < own
    gate = jnp.where(past, gate, NEG_INF)
    sel_t = jnp.where(past, _rank_lt(gate, 1, n_blk, top_k), 0.0)
    pen_t = jnp.where(jb == own, 0.0, (sel_t - 1.0) * (-NEG_INF))
    pen = _penalty_rows(pen_t.reshape(D_HEADS * n_blk, QB))
    q_ext = (q.astype(F32) + pen[None]).astype(BF16)

    m_scr[...] = jnp.full(m_scr.shape, NEG_INF, F32)
    tq = q0 + _iota((QB, MOBA_BLOCK), 0)
    key = _iota((QB, MOBA_BLOCK), 1)

    def scores(j, masked):
        base = pl.multiple_of(j * MOBA_BLOCK, MOBA_BLOCK)
        s = _bdot_nt(q_ext, k_ref[0, :, pl.ds(base, MOBA_BLOCK), :])
        if masked:
            s = jnp.where((key + base <= tq)[None], s, NEG_INF)
        s_scr[j] = s
        m_scr[...] = jnp.maximum(m_scr[...], jnp.maximum(s[..., :LANES], s[..., LANES:]))

    def pass1(j, _):
        scores(j, False)
        return 0

    lax.fori_loop(0, own, pass1, 0)
    scores(own, True)
    m = jnp.max(m_scr[...], axis=-1, keepdims=True)
    acc_scr[...] = jnp.zeros(acc_scr.shape, F32)

    def pass2(j, _):
        base = pl.multiple_of(j * MOBA_BLOCK, MOBA_BLOCK)
        p = jnp.exp2(s_scr[j] - m).astype(BF16)
        acc_scr[...] += _bdot(p, v_ref[0, :, pl.ds(base, MOBA_BLOCK), :])
        return 0

    lax.fori_loop(0, own + 1, pass2, 0)
    _store_normalized(o_ref, [acc_scr[h] for h in range(D_HEADS)])


def _moba(q, k, v, k_mean):
    bsz, _, seq, _ = q.shape
    n_qb = seq // QB
    n_blk = seq // MOBA_BLOCK
    top_k = min(MOBA_TOPK, n_blk - 1)
    q_spec = pl.BlockSpec((1, D_HEADS, QB, LANES), lambda b, i: (b, 0, i, 0))
    kv_spec = pl.BlockSpec((1, D_HEADS, seq, LANES), lambda b, i: (b, 0, 0, 0))
    return pl.pallas_call(
        functools.partial(_moba_kernel, top_k=top_k),
        grid=(bsz, n_qb),
        in_specs=[q_spec, kv_spec, kv_spec,
                  pl.BlockSpec((1, D_HEADS, n_blk, LANES), lambda b, i: (b, 0, 0, 0))],
        out_specs=pl.BlockSpec((1, QB, D_WIDTH), lambda b, i: (b, i, 0)),
        out_shape=jax.ShapeDtypeStruct((bsz, seq, D_WIDTH), BF16),
        scratch_shapes=[pltpu.VMEM((n_blk, D_HEADS, QB, MOBA_BLOCK), F32),
                        pltpu.VMEM((D_HEADS, QB, LANES), F32),
                        pltpu.VMEM((D_HEADS, QB, LANES), F32)],
        compiler_params=_params("parallel", "arbitrary"),
        name="moba_attention",
    )(q, k, v, k_mean)


def _cols(w, start, size):
    return w[:, start:start + size]


def _even_mixer(x, bsz, seq, tab64, norm_g, w_in, sgu_norm, sgu_w, sgu_b,
                cmp_pos_k, cmp_w1_k, cmp_w2_k, cmp_pos_v, cmp_w1_v, cmp_w2_v, w_out):
    o_q = 2 * A_WIDTH
    o_kc = o_q + B_WIDTH
    o_vc, o_ksl, o_vsl, o_kw, o_vw = (o_kc + B_KV_WIDTH * n for n in range(1, 6))
    o_gl = o_kc + 6 * B_KV_WIDTH
    n_gate = B_HEADS * N_BRANCH
    w_a = _cols(w_in, 0, 2 * A_WIDTH)
    w_q = _cols(w_in, o_q, B_WIDTH)
    w_plain = jnp.concatenate([_cols(w_in, o, B_KV_WIDTH) for o in (o_kc, o_vc, o_vsl, o_vw)], axis=1)
    w_rope = jnp.concatenate([_cols(w_in, o, B_KV_WIDTH) for o in (o_ksl, o_kw)], axis=1)
    w_gate = jnp.pad(_cols(w_in, o_gl, n_gate), ((0, 0), (0, LANES - n_gate)))
    a_in, q_raw, q_rot, kc, vc, v_slc, v_win, k_slc, k_win, gates = _even_proj(
        x, norm_g, tab64, [w_a, w_q, w_plain, w_rope, w_gate], bsz, seq)

    a_out = _gmlp(a_in, sgu_norm, sgu_w, sgu_b)

    k_cmp, v_cmp = _compress(kc, vc, cmp_pos_k, cmp_w1_k, cmp_w2_k, cmp_pos_v, cmp_w1_v, cmp_w2_v)
    b_out = _nsa(q_rot, q_raw, gates.reshape(bsz, seq, LANES), k_cmp, v_cmp, k_slc, v_slc, k_win, v_win)
    return a_out, b_out.reshape(bsz * seq, B_WIDTH), w_out


def _odd_mixer(x, bsz, seq, tab64, tab32, norm_g, w_in, w_out):
    sizes = (C_WIDTH, HEAD_DIM, HEAD_DIM, IDX_HEADS * IDX_DIM, IDX_DIM, IDX_HEADS, D_WIDTH, D_WIDTH, D_WIDTH)
    offs = np.concatenate([[0], np.cumsum(sizes)])
    w_qc, w_kc, w_vc, w_qi, w_ki, w_wi, w_qd, w_kd, w_vd = (
        _cols(w_in, int(o), int(s)) for o, s in zip(offs[:-1], sizes))
    zpad = lambda n: jnp.zeros((D_MODEL, n), w_in.dtype)
    w_rope = jnp.concatenate([w_qc, w_kc, zpad(HEAD_DIM), w_qd, w_kd], axis=1)
    w_ropei = jnp.concatenate([w_qi, w_ki, zpad(OD_ROPEI - IDX_HEADS * IDX_DIM - IDX_DIM)], axis=1)
    w_plain = jnp.concatenate([w_vc, w_wi, zpad(HEAD_DIM - IDX_HEADS), w_vd], axis=1)
    qc, kc, vc, qi, ki, wi_t, qd, kd, vd, k_mean = _odd_proj(
        x, norm_g, tab64, tab32, [w_rope, w_ropei, w_plain], bsz, seq)

    c_out = _dsa(qc, kc, vc, qi, ki, wi_t)
    n_blk = seq // MOBA_BLOCK
    km = k_mean.reshape(bsz, n_blk, D_HEADS, HEAD_DIM).transpose(0, 2, 1, 3)
    km = jnp.pad(km, ((0, 0), (0, 0), (0, 0), (0, LANES - HEAD_DIM))).astype(BF16)
    d_out = _moba(qd, kd, vd, km)
    return c_out.reshape(bsz * seq, C_WIDTH), d_out.reshape(bsz * seq, D_WIDTH), w_out


def kernel(x, positions, ffn1_norm, ffn1_w_gate, ffn1_w_up, ffn1_w_down, mix_norm, ffn2_norm, ffn2_w_gate, ffn2_w_up, ffn2_w_down, ev_w_in, ev_sgu_norm, ev_sgu_w, ev_sgu_b, ev_cmp_pos_k, ev_cmp_w1_k, ev_cmp_w2_k, ev_cmp_pos_v, ev_cmp_w1_v, ev_cmp_w2_v, ev_w_out, od_w_in, od_w_out, final_norm):
    bsz, seq, _ = x.shape
    depth = ffn1_norm.shape[0]
    tab64 = _rope_tables(positions, ROT_DIM, HEAD_DIM)
    tab32 = _rope_tables(positions, IDX_ROT, IDX_DIM)
    x = x.reshape(bsz * seq, D_MODEL)
    for i in range(depth):
        x = _ffn(x, ffn1_norm[i], ffn1_w_gate[i], ffn1_w_up[i], ffn1_w_down[i])
        if i % 2 == 0:
            e = i // 2
            mix = _even_mixer(x, bsz, seq, tab64, mix_norm[i], ev_w_in[e], ev_sgu_norm[e], ev_sgu_w[e],
                              ev_sgu_b[e], ev_cmp_pos_k[e], ev_cmp_w1_k[e], ev_cmp_w2_k[e],
                              ev_cmp_pos_v[e], ev_cmp_w1_v[e], ev_cmp_w2_v[e], ev_w_out[e])
        else:
            o = i // 2
            mix = _odd_mixer(x, bsz, seq, tab64, tab32, mix_norm[i], od_w_in[o], od_w_out[o])
        x = _ffn(x, ffn2_norm[i], ffn2_w_gate[i], ffn2_w_up[i], ffn2_w_down[i],
                 final_g=final_norm if i == depth - 1 else None, mix=mix)
    return x.reshape(bsz, seq, D_MODEL)
```

```python
import functools

import numpy as np
import jax
import jax.numpy as jnp
from jax import lax
from jax.experimental import pallas as pl
from jax.experimental.pallas import tpu as pltpu

F32 = jnp.float32
BF16 = jnp.bfloat16

D_MODEL = 1024
HEAD_DIM = 64
ROT_DIM = HEAD_DIM // 4
ROPE_THETA = 500000.0
NORM_EPS = 1e-6
D_FF = 2816
NEG_INF = -1e30

A_GROUPS = 4
A_CHUNK = 128
A_WIDTH = A_GROUPS * HEAD_DIM
B_HEADS = 12
B_KV_HEADS = 3
B_GROUP = B_HEADS // B_KV_HEADS
B_WIDTH = B_HEADS * HEAD_DIM
B_KV_WIDTH = B_KV_HEADS * HEAD_DIM
CMP_LEN = 32
CMP_STRIDE = 16
CMP_HIDDEN = 256
SLC_BLOCK = 64
SLC_TOPN = 8
WINDOW = 512
N_BRANCH = 3
FORCE_SCORE = 1e4
C_HEADS = 8
C_WIDTH = C_HEADS * HEAD_DIM
IDX_HEADS = 4
IDX_DIM = 32
IDX_ROT = IDX_DIM // 4
DSA_TOPK = 256
D_HEADS = 8
D_WIDTH = D_HEADS * HEAD_DIM
MOBA_BLOCK = 256
MOBA_TOPK = 3

LANES = 128
ATT_SCALE = HEAD_DIM ** -0.5
VMEM_LIMIT = 56 * 1024 * 1024
TM = 512
FF_CHUNK = D_FF // 11
KT = 256
QB = 256
QK_SCALE = ATT_SCALE * 1.4426950408889634
SEARCH_ROUNDS = 32
SEARCH_PATTERN = ("interp", "mixed", "interp", "halve")
SEARCH_WIDE = 32.0
AUX_LANE = HEAD_DIM


def _dot(a, b):
    return jnp.dot(a, b, preferred_element_type=F32)


def _dot_nt(a, b):
    return lax.dot_general(a, b, (((1,), (1,)), ((), ())), preferred_element_type=F32)


def _bdot_nt(a, b):
    return lax.dot_general(a, b, (((2,), (2,)), ((0,), (0,))), preferred_element_type=F32)


def _bdot(a, b):
    return lax.dot_general(a, b, (((2,), (1,)), ((0,), (0,))), preferred_element_type=F32)


def _iota(shape, dim):
    return lax.broadcasted_iota(jnp.int32, shape, dim)


def _rms(x, g):
    ms = jnp.mean(x * x, axis=-1, keepdims=True)
    return x * lax.rsqrt(ms + NORM_EPS) * g


def _gelu(x):
    return x * (0.5 * (1.0 + jnp.tanh(0.7978845608028654 * (x + 0.044715 * (x * x * x)))))


def _params(*sem):
    return pltpu.CompilerParams(dimension_semantics=sem, vmem_limit_bytes=VMEM_LIMIT)


def _full(shape):
    n = len(shape)
    return pl.BlockSpec(shape, lambda *_: (0,) * n)


def _resident(shape):
    n = len(shape)
    return pl.BlockSpec(shape, lambda *_: (0,) * n, pipeline_mode=pl.Buffered(1))


def _ffn_kernel(x_ref, g_ref, wg_ref, wu_ref, wd_ref, *rest, final, mixed):
    o_ref = rest[-1]
    x = x_ref[...]
    if mixed:
        a_ref, b_ref, wa_ref, wb_ref = rest[:4]
        x = x + (_dot(a_ref[...], wa_ref[...]) + _dot(b_ref[...], wb_ref[...]))
        rest = rest[4:]
    h = _rms(x, g_ref[...]).astype(BF16)
    acc = jnp.zeros_like(x)
    for c in range(D_FF // FF_CHUNK):
        sl = slice(c * FF_CHUNK, (c + 1) * FF_CHUNK)
        gate = _dot(h, wg_ref[:, sl])
        up = _dot(h, wu_ref[:, sl])
        act = (gate * jax.nn.sigmoid(gate) * up).astype(BF16)
        acc = acc + _dot(act, wd_ref[sl, :])
    y = x + 0.5 * acc
    if final:
        y = _rms(y, rest[0][...])
    o_ref[...] = y


def _ffn(x, g, wg, wu, wd, layer, final_g=None, mix=None):
    t = x.shape[0]
    final = final_g is not None
    ins = [x, g.reshape(1, D_MODEL), wg, wu, wd]

    def layer_weight(rows, cols):
        return pl.BlockSpec((None, rows, cols), lambda i: (layer, 0, 0), pipeline_mode=pl.Buffered(1))

    specs = [pl.BlockSpec((TM, D_MODEL), lambda i: (i, 0)), _full((1, D_MODEL)),
             layer_weight(D_MODEL, D_FF), layer_weight(D_MODEL, D_FF), layer_weight(D_FF, D_MODEL)]
    if mix is not None:
        a, b, w_out = mix
        na, nb = a.shape[1], b.shape[1]
        ins += [a, b, w_out[:na].astype(BF16), w_out[na:].astype(BF16)]
        specs += [pl.BlockSpec((TM, na), lambda i: (i, 0)), pl.BlockSpec((TM, nb), lambda i: (i, 0)),
                  _resident((na, D_MODEL)), _resident((nb, D_MODEL))]
    if final:
        ins.append(final_g.reshape(1, D_MODEL))
        specs.append(_full((1, D_MODEL)))
    return pl.pallas_call(
        functools.partial(_ffn_kernel, final=final, mixed=mix is not None),
        grid=(t // TM,),
        in_specs=specs,
        out_specs=pl.BlockSpec((TM, D_MODEL), lambda i: (i, 0)),
        out_shape=jax.ShapeDtypeStruct((t, D_MODEL), F32),
        compiler_params=_params("parallel"),
        name="ffn_final" if final else "ffn",
    )(*ins)


def _rope_tiles(z, parts_ref, spread_ref, shift, period):
    tab = _dot(parts_ref[...], spread_ref[...])
    cs, sn = tab[:, :LANES], tab[:, LANES:]
    is_x1 = (_iota(cs.shape, 1) & (period - 1)) < shift
    outs = []
    for c in range(z.shape[1] // LANES):
        zt = z[:, c * LANES:(c + 1) * LANES]
        up = pltpu.roll(zt, LANES - shift, axis=1)
        dn = pltpu.roll(zt, shift, axis=1)
        outs.append(zt * cs + jnp.where(is_x1, up, dn) * sn)
    return outs[0] if len(outs) == 1 else jnp.concatenate(outs, axis=1)


def _head_row(z, col, width, aux):
    tile = z[:, (col // LANES) * LANES:(col // LANES + 1) * LANES]
    off = col % LANES
    if off:
        tile = pltpu.roll(tile, LANES - off, axis=1)
    lane = _iota(tile.shape, 1)
    return jnp.where(lane < width, tile, aux)


def _seq_pos(sblk, rows):
    return sblk * rows + _iota((rows, LANES), 0)


def _even_proj_kernel(x_ref, g_ref, rp_ref, sp_ref, wa_ref, wq_ref, wp_ref, wr_ref, wg_ref,
                      a_ref, qn_ref, qr_ref, kc_ref, vc_ref, vs_ref, vw_ref, ks_ref, kw_ref, gt_ref,
                      *, n_sblk):
    sblk = pl.program_id(0) % n_sblk
    h = _rms(x_ref[...], g_ref[...]).astype(BF16)
    lane = _iota((TM, LANES), 1)
    ones_col = jnp.where(lane >= HEAD_DIM, 1.0, 0.0)
    slc_onehot = jnp.where(lane == AUX_LANE + (_seq_pos(sblk, TM) // SLC_BLOCK), 1.0, 0.0)

    a_ref[...] = _dot(h, wa_ref[...])
    zq = _dot(h, wq_ref[...]) * QK_SCALE
    zr = _rope_tiles(zq, rp_ref, sp_ref, ROT_DIM // 2, HEAD_DIM)
    for hh in range(B_HEADS):
        qn_ref[0, hh] = _head_row(zq, hh * HEAD_DIM, HEAD_DIM, 0.0).astype(BF16)
        qr_ref[0, hh] = _head_row(zr, hh * HEAD_DIM, HEAD_DIM, 0.0).astype(BF16)
    zp = _dot(h, wp_ref[...])
    for g in range(B_KV_HEADS):
        kc_ref[0, g] = _head_row(zp, g * HEAD_DIM, HEAD_DIM, 0.0)
        vc_ref[0, g] = _head_row(zp, (B_KV_HEADS + g) * HEAD_DIM, HEAD_DIM, 0.0)
    zk = _rope_tiles(_dot(h, wr_ref[...]), rp_ref, sp_ref, ROT_DIM // 2, HEAD_DIM)
    for g in range(B_KV_HEADS):
        vs_ref[0, g] = _head_row(zp, (2 * B_KV_HEADS + g) * HEAD_DIM, HEAD_DIM, ones_col).astype(BF16)
        vw_ref[0, g] = _head_row(zp, (3 * B_KV_HEADS + g) * HEAD_DIM, HEAD_DIM, ones_col).astype(BF16)
        ks_ref[0, g] = _head_row(zk, g * HEAD_DIM, HEAD_DIM, slc_onehot).astype(BF16)
        kw_ref[0, g] = _head_row(zk, (B_KV_HEADS + g) * HEAD_DIM, HEAD_DIM, 0.0).astype(BF16)
    gt_ref[...] = jax.nn.sigmoid(_dot(h, wg_ref[...]))


def _even_proj(x, g, tab64, weights, bsz, seq):
    t = x.shape[0]
    n_sblk = seq // TM
    tok = lambda n: pl.BlockSpec((TM, n), lambda i: (i, 0))
    heads = lambda n: pl.BlockSpec((1, n, TM, LANES), lambda i: (i // n_sblk, 0, i % n_sblk, 0))
    hshape = lambda n: jax.ShapeDtypeStruct((bsz, n, seq, LANES), BF16)
    return pl.pallas_call(
        functools.partial(_even_proj_kernel, n_sblk=n_sblk),
        grid=(t // TM,),
        in_specs=[tok(D_MODEL), _full((1, D_MODEL)), tok(LANES), _full(tab64[1].shape)]
                 + [_resident(w.shape) for w in weights],
        out_specs=[tok(2 * A_WIDTH), heads(B_HEADS), heads(B_HEADS), heads(B_KV_HEADS), heads(B_KV_HEADS),
                   heads(B_KV_HEADS), heads(B_KV_HEADS), heads(B_KV_HEADS), heads(B_KV_HEADS), tok(LANES)],
        out_shape=[jax.ShapeDtypeStruct((t, 2 * A_WIDTH), F32), hshape(B_HEADS), hshape(B_HEADS),
                   jax.ShapeDtypeStruct((bsz, B_KV_HEADS, seq, LANES), F32),
                   jax.ShapeDtypeStruct((bsz, B_KV_HEADS, seq, LANES), F32),
                   hshape(B_KV_HEADS), hshape(B_KV_HEADS), hshape(B_KV_HEADS), hshape(B_KV_HEADS),
                   jax.ShapeDtypeStruct((t, LANES), F32)],
        compiler_params=_params("parallel"),
        name="even_in_proj",
    )(x, g.reshape(1, D_MODEL), *tab64, *[w.astype(BF16) for w in weights])


OD_ROPE = C_WIDTH + 2 * HEAD_DIM + 2 * D_WIDTH
OD_KD = C_WIDTH + 2 * HEAD_DIM + D_WIDTH
OD_ROPEI = 2 * LANES
OD_PLAIN = 2 * HEAD_DIM + D_WIDTH


def _odd_proj_kernel(x_ref, g_ref, rp64_ref, sp64_ref, rp32_ref, sp32_ref, wr_ref, wi_ref, wp_ref,
                     qc_ref, kc_ref, vc_ref, qi_ref, ki_ref, wt_ref, qd_ref, kd_ref, vd_ref, km_ref,
                     *, n_sblk):
    sblk = pl.program_id(0) % n_sblk
    h = _rms(x_ref[...], g_ref[...]).astype(BF16)
    lane = _iota((TM, LANES), 1)
    ones_col = jnp.where(lane >= HEAD_DIM, 1.0, 0.0)
    blk = _seq_pos(sblk, TM) // MOBA_BLOCK

    zr = _rope_tiles(_dot(h, wr_ref[...]), rp64_ref, sp64_ref, ROT_DIM // 2, HEAD_DIM)
    for hh in range(C_HEADS):
        qc_ref[0, hh] = (_head_row(zr, hh * HEAD_DIM, HEAD_DIM, 0.0) * QK_SCALE).astype(BF16)
    kc_ref[0] = _head_row(zr, C_WIDTH, HEAD_DIM, 0.0).astype(BF16)
    for hh in range(D_HEADS):
        qd = _head_row(zr, C_WIDTH + 2 * HEAD_DIM + hh * HEAD_DIM, HEAD_DIM, 0.0)
        qd_ref[0, hh] = (qd * QK_SCALE).astype(BF16)
        onehot = jnp.where(lane == AUX_LANE + hh * (LANES - AUX_LANE) // D_HEADS + blk, 1.0, 0.0)
        kd_ref[0, hh] = _head_row(zr, OD_KD + hh * HEAD_DIM, HEAD_DIM, onehot).astype(BF16)
    zkd = zr[:, OD_KD:]
    n_mb = TM // MOBA_BLOCK
    km_ref[0] = jnp.sum(zkd.reshape(n_mb, MOBA_BLOCK, D_WIDTH), axis=1) * (1.0 / MOBA_BLOCK)

    zi = _rope_tiles(_dot(h, wi_ref[...]), rp32_ref, sp32_ref, IDX_ROT // 2, IDX_DIM)
    for hh in range(IDX_HEADS):
        qi_ref[0, hh] = _head_row(zi, hh * IDX_DIM, IDX_DIM, 0.0).astype(BF16)
    ki_ref[0] = _head_row(zi, IDX_HEADS * IDX_DIM, IDX_DIM, 0.0).astype(BF16)

    zp = _dot(h, wp_ref[...])
    vc_ref[0] = _head_row(zp, 0, HEAD_DIM, ones_col).astype(BF16)
    wt_ref[0] = pltpu.roll(zp[:, :LANES], LANES - HEAD_DIM, axis=1).T[:IDX_HEADS]
    for hh in range(D_HEADS):
        vd_ref[0, hh] = _head_row(zp, 2 * HEAD_DIM + hh * HEAD_DIM, HEAD_DIM, ones_col).astype(BF16)


def _odd_proj(x, g, tab64, tab32, weights, bsz, seq):
    t = x.shape[0]
    n_sblk = seq // TM
    tok = lambda n: pl.BlockSpec((TM, n), lambda i: (i, 0))
    tabs = [tok(LANES), _full(tab64[1].shape), tok(LANES), _full(tab32[1].shape)]
    heads = lambda n: pl.BlockSpec((1, n, TM, LANES), lambda i: (i // n_sblk, 0, i % n_sblk, 0))
    single = pl.BlockSpec((1, TM, LANES), lambda i: (i // n_sblk, i % n_sblk, 0))
    hshape = lambda n: jax.ShapeDtypeStruct((bsz, n, seq, LANES), BF16)
    sshape = jax.ShapeDtypeStruct((bsz, seq, LANES), BF16)
    return pl.pallas_call(
        functools.partial(_odd_proj_kernel, n_sblk=n_sblk),
        grid=(t // TM,),
        in_specs=[tok(D_MODEL), _full((1, D_MODEL))] + tabs + [_resident(w.shape) for w in weights],
        out_specs=[heads(C_HEADS), single, single, heads(IDX_HEADS), single,
                   pl.BlockSpec((1, IDX_HEADS, TM), lambda i: (i // n_sblk, 0, i % n_sblk)),
                   heads(D_HEADS), heads(D_HEADS), heads(D_HEADS),
                   pl.BlockSpec((1, TM // MOBA_BLOCK, D_WIDTH), lambda i: (i, 0, 0))],
        out_shape=[hshape(C_HEADS), sshape, sshape, hshape(IDX_HEADS), sshape,
                   jax.ShapeDtypeStruct((bsz, IDX_HEADS, seq), F32),
                   hshape(D_HEADS), hshape(D_HEADS), hshape(D_HEADS),
                   jax.ShapeDtypeStruct((t // TM, TM // MOBA_BLOCK, D_WIDTH), F32)],
        compiler_params=_params("parallel"),
        name="odd_in_proj",
    )(x, g.reshape(1, D_MODEL), *tab64, *tab32, *[w.astype(BF16) for w in weights])


def _rope_tables(positions, rot_dim, period):
    half = rot_dim // 2
    inv_freq = ROPE_THETA ** (-jnp.arange(0, rot_dim, 2, dtype=F32) / rot_dim)
    ang = positions.astype(F32).reshape(-1, 1) * inv_freq

    def parts(x):
        hi = x.astype(BF16)
        mid = (x - hi.astype(F32)).astype(BF16)
        lo = (x - hi.astype(F32) - mid.astype(F32)).astype(BF16)
        return [hi, mid, lo]

    t = ang.shape[0]
    cols = parts(jnp.cos(ang)) + parts(jnp.sin(ang)) + [jnp.ones((t, 1), BF16)]
    rows = jnp.concatenate(cols, axis=1)
    rows = jnp.pad(rows, ((0, 0), (0, LANES - rows.shape[1])))
    spread = np.zeros((LANES, 2 * LANES), np.float32)
    for lane in range(LANES):
        j = lane % period
        for k in range(3):
            if j < 2 * half:
                spread[k * half + j % half, lane] = 1.0
                spread[3 * half + k * half + j % half, LANES + lane] = -1.0 if j < half else 1.0
        if j >= 2 * half:
            spread[6 * half, lane] = 1.0
    return rows, jnp.asarray(spread, dtype=BF16)


def _gmlp_kernel(a_ref, n_ref, w_ref, b_ref, o_ref):
    causal = _iota((A_CHUNK, A_CHUNK), 1) <= _iota((A_CHUNK, A_CHUNK), 0)
    lane_group = _iota((A_CHUNK, A_WIDTH), 1) // HEAD_DIM
    ws = [jnp.where(causal, w_ref[g], 0.0).astype(BF16) for g in range(A_GROUPS)]
    for c in range(TM // A_CHUNK):
        rs = slice(c * A_CHUNK, (c + 1) * A_CHUNK)
        z = _gelu(a_ref[rs, :])
        u = z[:, :A_WIDTH]
        v = _rms(z[:, A_WIDTH:], n_ref[...]).astype(BF16)
        mixed = jnp.zeros((A_CHUNK, A_WIDTH), F32)
        for g in range(A_GROUPS):
            mixed = jnp.where(lane_group == g, _dot(ws[g], v) + b_ref[g], mixed)
        o_ref[rs, :] = (u * mixed).astype(BF16)


def _gmlp(a_in, sgu_norm, sgu_w, sgu_b):
    t = a_in.shape[0]
    return pl.pallas_call(
        _gmlp_kernel,
        grid=(t // TM,),
        in_specs=[pl.BlockSpec((TM, 2 * A_WIDTH), lambda i: (i, 0)),
                  _full((1, A_WIDTH)), _full((A_GROUPS, A_CHUNK, A_CHUNK)),
                  _full((A_GROUPS, A_CHUNK, 1))],
        out_specs=pl.BlockSpec((TM, A_WIDTH), lambda i: (i, 0)),
        out_shape=jax.ShapeDtypeStruct((t, A_WIDTH), BF16),
        compiler_params=_params("parallel"),
        name="gmlp",
    )(a_in, sgu_norm.reshape(1, A_WIDTH), sgu_w, sgu_b.reshape(A_GROUPS, A_CHUNK, 1))


def _compress_one(x_ref, pos_ref, w1_ref, w2_ref, o_ref):
    n_rows = x_ref.shape[2] // CMP_STRIDE
    first = [jnp.zeros((n_rows, CMP_HIDDEN), F32) for _ in range(B_KV_HEADS)]
    second = [jnp.zeros((n_rows, CMP_HIDDEN), F32) for _ in range(B_KV_HEADS)]
    for r in range(CMP_STRIDE):
        for g in range(B_KV_HEADS):
            xg = x_ref[0, g, pl.ds(r, n_rows, stride=CMP_STRIDE), :]
            first[g] = first[g] + _dot((xg + pos_ref[r:r + 1, :]).astype(BF16), w1_ref[r])
            second[g] = second[g] + _dot((xg + pos_ref[CMP_STRIDE + r:CMP_STRIDE + r + 1, :]).astype(BF16),
                                         w1_ref[CMP_STRIDE + r])
    for g in range(B_KV_HEADS):
        hid = _gelu(first[g] + pltpu.roll(second[g], n_rows - 1, axis=0))
        out = _dot(hid.astype(BF16), w2_ref[...])
        row = _iota(out.shape, 0)
        o_ref[0, g] = jnp.where(row < n_rows - 1, out, 0.0).astype(BF16)


def _compress_kernel(kc_ref, vc_ref, pk_ref, w1k_ref, w2k_ref, pv_ref, w1v_ref, w2v_ref, ko_ref, vo_ref):
    _compress_one(kc_ref, pk_ref, w1k_ref, w2k_ref, ko_ref)
    _compress_one(vc_ref, pv_ref, w1v_ref, w2v_ref, vo_ref)


def _compress(kc, vc, pos_k, w1_k, w2_k, pos_v, w1_v, w2_v):
    bsz, _, seq, _ = kc.shape
    nrow = seq // CMP_STRIDE
    lane_pad = LANES - HEAD_DIM

    def prep(pos, w1, w2):
        w1p = jnp.pad(w1.reshape(CMP_LEN, HEAD_DIM, CMP_HIDDEN), ((0, 0), (0, lane_pad), (0, 0)))
        return (jnp.pad(pos, ((0, 0), (0, lane_pad))), w1p.astype(BF16),
                jnp.pad(w2, ((0, 0), (0, lane_pad))).astype(BF16))

    pk, w1k, w2k = prep(pos_k, w1_k, w2_k)
    pv, w1v, w2v = prep(pos_v, w1_v, w2_v)
    blk_in = pl.BlockSpec((1, B_KV_HEADS, seq, LANES), lambda b: (b, 0, 0, 0))
    blk_out = pl.BlockSpec((1, B_KV_HEADS, nrow, LANES), lambda b: (b, 0, 0, 0))
    wspecs = [_full((CMP_LEN, LANES)), _full((CMP_LEN, LANES, CMP_HIDDEN)), _full((CMP_HIDDEN, LANES))]
    return pl.pallas_call(
        _compress_kernel,
        grid=(bsz,),
        in_specs=[blk_in, blk_in] + wspecs + wspecs,
        out_specs=[blk_out, blk_out],
        out_shape=[jax.ShapeDtypeStruct((bsz, B_KV_HEADS, nrow, LANES), BF16)] * 2,
        compiler_params=_params("parallel"),
        name="nsa_compress",
    )(kc, vc, pk, w1k, w2k, pv, w1v, w2v)


def _rank_lt(vals, axis, n, k):
    j = _iota(vals.shape, axis)
    rank = jnp.zeros(vals.shape, F32)
    for jp in range(n):
        row = lax.slice_in_dim(vals, jp, jp + 1, axis=axis)
        beats = jnp.where(row > vals, 1.0, jnp.where(row == vals, jnp.where(j > jp, 1.0, 0.0), 0.0))
        rank = rank + beats
    return jnp.where(rank < k, 1.0, 0.0)


def _penalty_rows(pen_t):
    n, nq = pen_t.shape
    parts = [jnp.zeros((AUX_LANE, nq), F32), pen_t]
    if LANES - AUX_LANE - n:
        parts.append(jnp.zeros((LANES - AUX_LANE - n, nq), F32))
    full = jnp.concatenate(parts, axis=0)
    halves = [full[:, c * LANES:(c + 1) * LANES].T for c in range(nq // LANES)]
    return halves[0] if len(halves) == 1 else jnp.concatenate(halves, axis=0)


def _swap_halves(acc):
    return pltpu.roll(acc, HEAD_DIM, axis=1)


def _store_normalized(o_ref, accs):
    lane = _iota((QB, LANES), 1)
    for p in range(len(accs) // 2):
        even, odd = accs[2 * p], accs[2 * p + 1]
        both = jnp.where(lane < HEAD_DIM, even / _swap_halves(even), _swap_halves(odd) / odd)
        o_ref[0, :, p * LANES:(p + 1) * LANES] = both.astype(BF16)


def _nsa_kernel(qr_ref, qn_ref, gt_ref, kc_ref, vc_ref, ks_ref, vs_ref, kw_ref, vw_ref, ovt_ref, ex_ref,
                o_ref, s_scr, q_scr, cmp_scr, m_scr, acc_scr):
    i = pl.program_id(1)
    q0 = i * QB
    rows = B_GROUP * QB
    n_slc = ovt_ref.shape[0]
    n_g = B_KV_HEADS
    gates = gt_ref[0]
    tq2 = q0 + _iota((QB, KT), 0)
    key2 = _iota((QB, KT), 1)

    def masked(s, ok):
        n = s.shape[1]
        return jnp.where(ok[None], s.reshape(B_GROUP, QB, n), NEG_INF).reshape(rows, n)

    m_c = ((_iota((QB, LANES), 1) * CMP_STRIDE + (CMP_LEN - 1)) <= q0 + _iota((QB, LANES), 0))[None]
    for g in range(n_g):
        hs = slice(g * B_GROUP, (g + 1) * B_GROUP)
        qn = qn_ref[0, hs].reshape(rows, LANES)
        s_c = _dot_nt(qn, kc_ref[0, g]).reshape(B_GROUP, QB, LANES)
        sm = jnp.where(m_c, s_c, NEG_INF)
        e = jnp.where(m_c, jnp.exp2(sm - jnp.max(sm, axis=-1, keepdims=True)), 0.0)
        den = jnp.sum(e, axis=-1, keepdims=True)
        p_cb = (e / jnp.where(den > 0.0, den, 1.0)).reshape(rows, LANES).astype(BF16)
        cmp_scr[g] = _dot(p_cb, vc_ref[0, g])

        imp = jnp.zeros((n_slc, QB), F32)
        for r in range(B_GROUP):
            imp = imp + _dot_nt(ovt_ref[...], p_cb[r * QB:(r + 1) * QB])
        jb = _iota((n_slc, QB), 0)
        tq = q0 + _iota((n_slc, QB), 1)
        forced = (jb == 0) | (jb == (tq >> 6))
        imp = jnp.where(jb * SLC_BLOCK <= tq, jnp.where(forced, FORCE_SCORE, imp), NEG_INF)
        pen = _penalty_rows((_rank_lt(imp, 0, n_slc, SLC_TOPN) - 1.0) * (-NEG_INF))
        qr = qr_ref[0, hs].astype(F32) + pen[None]
        q_scr[g] = qr.reshape(rows, LANES).astype(BF16)

    n_wt = (WINDOW + QB + KT - 1) // KT
    w_off = [QB - (n_wt - c) * KT for c in range(n_wt)]
    w_base = [pl.multiple_of(jnp.maximum(q0 + off, 0), min(QB, KT)) for off in w_off]

    def window_scores(interior):
        w_mx = [jnp.full((rows, LANES), NEG_INF, F32) for _ in range(n_g)]
        for c in range(n_wt):
            key = key2 + w_base[c]
            ok = None
            if not interior:
                ok = (key > tq2 - WINDOW) & (key <= jnp.minimum(tq2, q0 + (w_off[c] + KT - 1)))
            elif w_off[c] < QB - WINDOW and w_off[c] + KT - 1 > 0:
                ok = (key > tq2 - WINDOW) & (key <= tq2)
            elif w_off[c] < QB - WINDOW:
                ok = key > tq2 - WINDOW
            elif w_off[c] + KT - 1 > 0:
                ok = key <= tq2
            for g in range(n_g):
                s = _dot_nt(q_scr[g], kw_ref[0, g, pl.ds(w_base[c], KT), :])
                if ok is not None:
                    s = masked(s, ok)
                s_scr[c, g] = s
                w_mx[g] = jnp.maximum(w_mx[g], jnp.maximum(s[:, :LANES], s[:, LANES:]))
        for g in range(n_g):
            m_scr[g] = w_mx[g]

    first_interior = (n_wt * KT - QB + QB - 1) // QB
    pl.when(i >= first_interior)(functools.partial(window_scores, True))
    pl.when(i < first_interior)(functools.partial(window_scores, False))
    for g in range(n_g):
        w_max = jnp.max(m_scr[g], axis=-1, keepdims=True)
        acc_w = jnp.zeros((rows, LANES), F32)
        for c in range(n_wt):
            p = jnp.exp2(s_scr[c, g] - w_max).astype(BF16)
            acc_w = acc_w + _dot(p, vw_ref[0, g, pl.ds(w_base[c], KT), :])
        acc_scr[g] = acc_w

    n_tiles = (q0 + QB + KT - 1) // KT
    last = n_tiles - 1
    for g in range(n_g):
        m_scr[g] = jnp.full((rows, LANES), NEG_INF, F32)

    def sel_scores(c, causal):
        base = pl.multiple_of(c * KT, KT)
        for g in range(n_g):
            s = _dot_nt(q_scr[g], ks_ref[0, g, pl.ds(base, KT), :])
            if causal:
                s = masked(s, key2 + base <= tq2)
            s_scr[c, g] = s
            m_scr[g] = jnp.maximum(m_scr[g], jnp.maximum(s[:, :LANES], s[:, LANES:]))

    def sel_pass1(c, _):
        sel_scores(c, False)
        return 0

    lax.fori_loop(0, last, sel_pass1, 0)
    sel_scores(last, True)

    m_sel = [jnp.max(m_scr[g], axis=-1, keepdims=True) for g in range(n_g)]
    for g in range(n_g):
        m_scr[g] = jnp.zeros((rows, LANES), F32)

    def sel_pass2(c, _):
        base = pl.multiple_of(c * KT, KT)
        for g in range(n_g):
            p = jnp.exp2(s_scr[c, g] - m_sel[g]).astype(BF16)
            m_scr[g] += _dot(p, vs_ref[0, g, pl.ds(base, KT), :])
        return 0

    lax.fori_loop(0, n_tiles, sel_pass2, 0)

    g_hi = gates.astype(BF16)
    g_lo = (gates - g_hi.astype(F32)).astype(BF16)
    g_parts = jnp.concatenate([g_hi, g_lo], axis=1)
    lane = _iota((QB, LANES), 1)
    pair_cols = 2 * N_BRANCH * LANES
    for p in range(B_HEADS // 2):
        g, r = (2 * p) // B_GROUP, (2 * p) % B_GROUP
        ev, od = slice(r * QB, (r + 1) * QB), slice((r + 1) * QB, (r + 2) * QB)
        spread = _dot(g_parts, ex_ref[:, p * pair_cols:(p + 1) * pair_cols])
        gate = [spread[:, k * LANES:(k + 1) * LANES] for k in range(2 * N_BRANCH)]
        sel_e, win_e = m_scr[g, ev, :], acc_scr[g, ev, :]
        sel_o, win_o = m_scr[g, od, :], acc_scr[g, od, :]
        lower = (gate[0] * cmp_scr[g, ev, :] + gate[1] * (sel_e / _swap_halves(sel_e))
                 + gate[2] * (win_e / _swap_halves(win_e)))
        upper = (gate[3] * _swap_halves(cmp_scr[g, od, :]) + gate[4] * (_swap_halves(sel_o) / sel_o)
                 + gate[5] * (_swap_halves(win_o) / win_o))
        o_ref[0, :, p * LANES:(p + 1) * LANES] = jnp.where(lane < HEAD_DIM, lower, upper).astype(BF16)


def _overlap_t(n_rows, n_slc):
    n = np.arange(n_rows)
    c0 = n * CMP_STRIDE
    s0 = np.arange(n_slc) * SLC_BLOCK
    m = (c0[None, :] < s0[:, None] + SLC_BLOCK) & (c0[None, :] + CMP_LEN > s0[:, None])
    m = m & (n[None, :] < n_rows - 1)
    return jnp.asarray(m, dtype=BF16)


def _nsa(q_rot, q_raw, gates, k_cmp, v_cmp, k_slc, v_slc, k_win, v_win):
    bsz, _, seq, _ = q_rot.shape
    n_qb = seq // QB
    n_slc = seq // SLC_BLOCK
    n_cmp_rows = k_cmp.shape[2]
    ovt = _overlap_t(n_cmp_rows, n_slc)
    n_gate = B_HEADS * N_BRANCH
    spread = np.zeros((2 * LANES, n_gate * LANES), np.float32)
    for c in range(n_gate):
        spread[[c, LANES + c], c * LANES:(c + 1) * LANES] = 1.0
    spread = jnp.asarray(spread, dtype=BF16)
    q_spec = pl.BlockSpec((1, B_HEADS, QB, LANES), lambda b, i: (b, 0, i, 0))
    cmp_spec = pl.BlockSpec((1, B_KV_HEADS, n_cmp_rows, LANES), lambda b, i: (b, 0, 0, 0))
    kv_spec = pl.BlockSpec((1, B_KV_HEADS, seq, LANES), lambda b, i: (b, 0, 0, 0), pipeline_mode=pl.Buffered(1))
    rows = B_GROUP * QB
    return pl.pallas_call(
        _nsa_kernel,
        grid=(bsz, n_qb),
        in_specs=[q_spec, q_spec, pl.BlockSpec((1, QB, LANES), lambda b, i: (b, i, 0)),
                  cmp_spec, cmp_spec, kv_spec, kv_spec, kv_spec, kv_spec, _full(ovt.shape),
                  _resident(spread.shape)],
        out_specs=pl.BlockSpec((1, QB, B_WIDTH), lambda b, i: (b, i, 0)),
        out_shape=jax.ShapeDtypeStruct((bsz, seq, B_WIDTH), BF16),
        scratch_shapes=[pltpu.VMEM((seq // KT, B_KV_HEADS, rows, KT), F32),
                        pltpu.VMEM((B_KV_HEADS, rows, LANES), BF16),
                        pltpu.VMEM((B_KV_HEADS, rows, LANES), F32),
                        pltpu.VMEM((B_KV_HEADS, rows, LANES), F32),
                        pltpu.VMEM((B_KV_HEADS, rows, LANES), F32)],
        compiler_params=_params("parallel", "arbitrary"),
        name="nsa_attention",
    )(q_rot, q_raw, gates, k_cmp, v_cmp, k_slc, v_slc, k_win, v_win, ovt, spread)


def _dsa_kernel(q_ref, k_ref, v_ref, qi_ref, ki_ref, wi_ref, o_ref,
                sc_scr, bias_scr, s_scr, mx_scr, acc_scr, *, top_k):
    i = pl.program_id(1)
    q0 = i * QB
    n_tiles = (q0 + QB + KT - 1) // KT
    tile = (KT, QB)
    key_sub = _iota(tile, 0)
    tq = q0 + _iota(tile, 1)
    w_idx = wi_ref[0]
    qi = qi_ref[0].reshape(IDX_HEADS * QB, LANES)
    n_sub = KT // 8

    def fold(x, op):
        return op(op(x.reshape(n_sub // 4, 4, 8, QB), axis=0), axis=0)

    w_s = w_idx * (IDX_DIM ** -0.5 * IDX_HEADS ** -0.5)
    last = n_tiles - 1

    def score_tile(c, carry, masked):
        mn, mx, zge, zgt = carry
        base = pl.multiple_of(c * KT, KT)
        lg = _dot_nt(ki_ref[0, pl.ds(base, KT), :], qi)
        sc = jnp.zeros(tile, F32)
        for h in range(IDX_HEADS):
            sc = sc + w_s[h:h + 1, :] * jnp.maximum(lg[:, h * QB:(h + 1) * QB], 0.0)
        if masked:
            valid = key_sub + base <= tq
            lowest = jnp.where(valid, sc, -NEG_INF)
            sc = jnp.where(valid, sc, NEG_INF)
        else:
            lowest = sc
        sc_scr[c] = sc
        return (jnp.minimum(mn, fold(lowest, jnp.min)), jnp.maximum(mx, fold(sc, jnp.max)),
                zge + fold(jnp.where(sc >= 0.0, 1.0, 0.0), jnp.sum),
                zgt + fold(jnp.where(sc > 0.0, 1.0, 0.0), jnp.sum))

    stats = lax.fori_loop(0, last, lambda c, carry: score_tile(c, carry, False),
                          (jnp.full((8, QB), -NEG_INF, F32), jnp.full((8, QB), NEG_INF, F32),
                           jnp.zeros((8, QB), F32), jnp.zeros((8, QB), F32)))
    mn, mx, zge, zgt = score_tile(last, stats, True)
    mn = jnp.min(mn, axis=0, keepdims=True)
    mx = jnp.max(mx, axis=0, keepdims=True)
    zge = jnp.sum(zge, axis=0, keepdims=True)
    zgt = jnp.sum(zgt, axis=0, keepdims=True)
    kf = jnp.float32(top_k)
    n_valid = (tq[0:1, :] + 1).astype(F32)
    all_taken = n_valid <= kf
    zero_tied = jnp.logical_and(zgt < kf, zge >= kf)
    positive = zgt >= kf

    def search_step(carry, kind):
        lo_, hi_, clo, chi = carry
        if kind == "halve":
            frac = 0.5
        else:
            frac = jnp.clip((clo - kf + 0.5) / jnp.maximum(clo - chi, 1.0), 1.0 / 64, 63.0 / 64)
            if kind == "mixed":
                frac = jnp.where(clo - chi > SEARCH_WIDE, 0.5, frac)
        mid = lo_ + (hi_ - lo_) * frac
        cnt = lax.fori_loop(0, n_tiles,
                            lambda c, a: a + fold(jnp.where(sc_scr[c] >= mid, 1.0, 0.0), jnp.sum),
                            jnp.zeros((8, QB), F32))
        cm = jnp.sum(cnt, axis=0, keepdims=True)
        ge = cm >= kf
        return (jnp.where(ge, mid, lo_), jnp.where(ge, hi_, mid), jnp.where(ge, cm, clo), jnp.where(ge, chi, cm))

    def round_cond(carry):
        r, _, _, clo, chi = carry
        settled = jnp.where(all_taken, 1.0, jnp.where(zero_tied, 1.0,
                            jnp.where(clo == kf, 1.0, jnp.where(chi == kf - 1.0, 1.0, 0.0))))
        return jnp.logical_and(r < SEARCH_ROUNDS, jnp.min(settled) < 1.0)

    def round_body(carry):
        st = carry[1:]
        for kind in SEARCH_PATTERN:
            st = search_step(st, kind)
        return (carry[0] + 1,) + st

    top = mx + jnp.abs(mx) * 1e-3 + 1.0
    init = (jnp.int32(0), jnp.where(positive, 0.0, mn), jnp.where(positive, top, 0.0),
            jnp.where(positive, zge, n_valid), jnp.where(positive, 0.0, zge))
    _, lo, hi, c_lo, c_hi = lax.while_loop(round_cond, round_body, init)
    exact_cut = c_lo == kf

    def thr_body(c, vm):
        sc = sc_scr[c]
        return jnp.maximum(vm, fold(jnp.where(sc < hi, sc, NEG_INF), jnp.max))

    vm = lax.fori_loop(0, n_tiles, thr_body, jnp.full((8, QB), NEG_INF, F32))
    thr = jnp.where(zero_tied, 0.0, jnp.max(vm, axis=0, keepdims=True))
    need = kf - jnp.where(zero_tied, zgt, c_hi)
    tri = jnp.where(_iota((KT, KT), 1) <= _iota((KT, KT), 0), 1.0, 0.0).astype(BF16)

    def mask_body(c, seen):
        base = c * KT
        sc = sc_scr[c]
        eq = sc == thr
        prefix = _dot(tri, jnp.where(eq, 1.0, 0.0).astype(BF16)) + seen
        tied = jnp.where(eq, jnp.where(prefix <= need, 1.0, 0.0), 0.0)
        by_thr = jnp.where(sc > thr, 1.0, tied)
        chosen = jnp.where(exact_cut, jnp.where(sc >= lo, 1.0, 0.0), by_thr)
        valid = jnp.where(key_sub + base <= tq, 1.0, 0.0)
        bias_t = (jnp.where(all_taken, valid, chosen) - 1.0) * (-NEG_INF)
        for qh in range(QB // LANES):
            for kh in range(KT // LANES):
                bias_scr[c, qh * LANES:(qh + 1) * LANES, kh * LANES:(kh + 1) * LANES] = (
                    bias_t[kh * LANES:(kh + 1) * LANES, qh * LANES:(qh + 1) * LANES].T)
        return prefix[KT - 1:KT, :]

    lax.fori_loop(0, n_tiles, mask_body, jnp.zeros((1, QB), F32))

    rows = C_HEADS * QB
    q = q_ref[0].reshape(rows, LANES)
    mx_scr[...] = jnp.full((rows, LANES), NEG_INF, F32)

    def att_pass1(c, _):
        base = pl.multiple_of(c * KT, KT)
        s = _dot_nt(q, k_ref[0, pl.ds(base, KT), :])
        s = (s.reshape(C_HEADS, QB, KT) + bias_scr[c][None]).reshape(rows, KT)
        s_scr[c] = s
        mx_scr[...] = jnp.maximum(mx_scr[...], jnp.maximum(s[:, :LANES], s[:, LANES:]))
        return 0

    lax.fori_loop(0, n_tiles, att_pass1, 0)
    m = jnp.max(mx_scr[...], axis=-1, keepdims=True)
    acc_scr[...] = jnp.zeros((rows, LANES), F32)

    def att_pass2(c, _):
        base = pl.multiple_of(c * KT, KT)
        p = jnp.exp2(s_scr[c] - m).astype(BF16)
        acc_scr[...] += _dot(p, v_ref[0, pl.ds(base, KT), :])
        return 0

    lax.fori_loop(0, n_tiles, att_pass2, 0)
    _store_normalized(o_ref, [acc_scr[h * QB:(h + 1) * QB, :] for h in range(C_HEADS)])


def _dsa(q, k, v, q_idx, k_idx, w_idx_t):
    bsz, _, seq, _ = q.shape
    n_qb = seq // QB
    top_k = min(DSA_TOPK, seq // 4)
    rows = C_HEADS * QB
    single = pl.BlockSpec((1, seq, LANES), lambda b, i: (b, 0, 0))
    return pl.pallas_call(
        functools.partial(_dsa_kernel, top_k=top_k),
        grid=(bsz, n_qb),
        in_specs=[pl.BlockSpec((1, C_HEADS, QB, LANES), lambda b, i: (b, 0, i, 0)),
                  single, single,
                  pl.BlockSpec((1, IDX_HEADS, QB, LANES), lambda b, i: (b, 0, i, 0)),
                  single,
                  pl.BlockSpec((1, IDX_HEADS, QB), lambda b, i: (b, 0, i))],
        out_specs=pl.BlockSpec((1, QB, C_WIDTH), lambda b, i: (b, i, 0)),
        out_shape=jax.ShapeDtypeStruct((bsz, seq, C_WIDTH), BF16),
        scratch_shapes=[pltpu.VMEM((seq // KT, KT, QB), F32),
                        pltpu.VMEM((seq // KT, QB, KT), F32),
                        pltpu.VMEM((seq // KT, rows, KT), F32),
                        pltpu.VMEM((rows, LANES), F32),
                        pltpu.VMEM((rows, LANES), F32)],
        compiler_params=_params("parallel", "arbitrary"),
        name="dsa_attention",
    )(q, k, v, q_idx, k_idx, w_idx_t)


def _moba_kernel(q_ref, k_ref, v_ref, km_ref, o_ref, s_scr, m_scr, acc_scr, *, top_k):
    i = pl.program_id(1)
    q0 = i * QB
    own = q0 // MOBA_BLOCK
    n_blk = km_ref.shape[2]
    q = q_ref[0]
    gate = _bdot_nt(km_ref[0], q) * (1.0 / QK_SCALE)
    jb = _iota(gate.shape, 1)
    past = jb < own
    gate = jnp.where(past, gate, NEG_INF)
    sel_t = jnp.where(past, _rank_lt(gate, 1, n_blk, top_k), 0.0)
    pen_t = jnp.where(jb == own, 0.0, (sel_t - 1.0) * (-NEG_INF))
    pen = _penalty_rows(pen_t.reshape(D_HEADS * n_blk, QB))
    q_ext = (q.astype(F32) + pen[None]).astype(BF16)

    m_scr[...] = jnp.full(m_scr.shape, NEG_INF, F32)
    tq = q0 + _iota((QB, MOBA_BLOCK), 0)
    key = _iota((QB, MOBA_BLOCK), 1)

    def scores(j, masked):
        base = pl.multiple_of(j * MOBA_BLOCK, MOBA_BLOCK)
        s = _bdot_nt(q_ext, k_ref[0, :, pl.ds(base, MOBA_BLOCK), :])
        if masked:
            s = jnp.where((key + base <= tq)[None], s, NEG_INF)
        s_scr[j] = s
        m_scr[...] = jnp.maximum(m_scr[...], jnp.maximum(s[..., :LANES], s[..., LANES:]))

    def pass1(j, _):
        scores(j, False)
        return 0

    lax.fori_loop(0, own, pass1, 0)
    scores(own, True)
    m = jnp.max(m_scr[...], axis=-1, keepdims=True)
    acc_scr[...] = jnp.zeros(acc_scr.shape, F32)

    def pass2(j, _):
        base = pl.multiple_of(j * MOBA_BLOCK, MOBA_BLOCK)
        p = jnp.exp2(s_scr[j] - m).astype(BF16)
        acc_scr[...] += _bdot(p, v_ref[0, :, pl.ds(base, MOBA_BLOCK), :])
        return 0

    lax.fori_loop(0, own + 1, pass2, 0)
    _store_normalized(o_ref, [acc_scr[h] for h in range(D_HEADS)])


def _moba(q, k, v, k_mean):
    bsz, _, seq, _ = q.shape
    n_qb = seq // QB
    n_blk = seq // MOBA_BLOCK
    top_k = min(MOBA_TOPK, n_blk - 1)
    q_spec = pl.BlockSpec((1, D_HEADS, QB, LANES), lambda b, i: (b, 0, i, 0))
    kv_spec = pl.BlockSpec((1, D_HEADS, seq, LANES), lambda b, i: (b, 0, 0, 0))
    return pl.pallas_call(
        functools.partial(_moba_kernel, top_k=top_k),
        grid=(bsz, n_qb),
        in_specs=[q_spec, kv_spec, kv_spec,
                  pl.BlockSpec((1, D_HEADS, n_blk, LANES), lambda b, i: (b, 0, 0, 0))],
        out_specs=pl.BlockSpec((1, QB, D_WIDTH), lambda b, i: (b, i, 0)),
        out_shape=jax.ShapeDtypeStruct((bsz, seq, D_WIDTH), BF16),
        scratch_shapes=[pltpu.VMEM((n_blk, D_HEADS, QB, MOBA_BLOCK), F32),
                        pltpu.VMEM((D_HEADS, QB, LANES), F32),
                        pltpu.VMEM((D_HEADS, QB, LANES), F32)],
        compiler_params=_params("parallel", "arbitrary"),
        name="moba_attention",
    )(q, k, v, k_mean)


def _cols(w, start, size):
    return w[:, start:start + size]


def _even_mixer(x, bsz, seq, tab64, norm_g, w_in, sgu_norm, sgu_w, sgu_b,
                cmp_pos_k, cmp_w1_k, cmp_w2_k, cmp_pos_v, cmp_w1_v, cmp_w2_v, w_out):
    o_q = 2 * A_WIDTH
    o_kc = o_q + B_WIDTH
    o_vc, o_ksl, o_vsl, o_kw, o_vw = (o_kc + B_KV_WIDTH * n for n in range(1, 6))
    o_gl = o_kc + 6 * B_KV_WIDTH
    n_gate = B_HEADS * N_BRANCH
    w_a = _cols(w_in, 0, 2 * A_WIDTH)
    w_q = _cols(w_in, o_q, B_WIDTH)
    w_plain = jnp.concatenate([_cols(w_in, o, B_KV_WIDTH) for o in (o_kc, o_vc, o_vsl, o_vw)], axis=1)
    w_rope = jnp.concatenate([_cols(w_in, o, B_KV_WIDTH) for o in (o_ksl, o_kw)], axis=1)
    w_gate = jnp.pad(_cols(w_in, o_gl, n_gate), ((0, 0), (0, LANES - n_gate)))
    a_in, q_raw, q_rot, kc, vc, v_slc, v_win, k_slc, k_win, gates = _even_proj(
        x, norm_g, tab64, [w_a, w_q, w_plain, w_rope, w_gate], bsz, seq)

    a_out = _gmlp(a_in, sgu_norm, sgu_w, sgu_b)

    k_cmp, v_cmp = _compress(kc, vc, cmp_pos_k, cmp_w1_k, cmp_w2_k, cmp_pos_v, cmp_w1_v, cmp_w2_v)
    b_out = _nsa(q_rot, q_raw, gates.reshape(bsz, seq, LANES), k_cmp, v_cmp, k_slc, v_slc, k_win, v_win)
    return a_out, b_out.reshape(bsz * seq, B_WIDTH), w_out


def _odd_mixer(x, bsz, seq, tab64, tab32, norm_g, w_in, w_out):
    sizes = (C_WIDTH, HEAD_DIM, HEAD_DIM, IDX_HEADS * IDX_DIM, IDX_DIM, IDX_HEADS, D_WIDTH, D_WIDTH, D_WIDTH)
    offs = np.concatenate([[0], np.cumsum(sizes)])
    w_qc, w_kc, w_vc, w_qi, w_ki, w_wi, w_qd, w_kd, w_vd = (
        _cols(w_in, int(o), int(s)) for o, s in zip(offs[:-1], sizes))
    zpad = lambda n: jnp.zeros((D_MODEL, n), w_in.dtype)
    w_rope = jnp.concatenate([w_qc, w_kc, zpad(HEAD_DIM), w_qd, w_kd], axis=1)
    w_ropei = jnp.concatenate([w_qi, w_ki, zpad(OD_ROPEI - IDX_HEADS * IDX_DIM - IDX_DIM)], axis=1)
    w_plain = jnp.concatenate([w_vc, w_wi, zpad(HEAD_DIM - IDX_HEADS), w_vd], axis=1)
    qc, kc, vc, qi, ki, wi_t, qd, kd, vd, k_mean = _odd_proj(
        x, norm_g, tab64, tab32, [w_rope, w_ropei, w_plain], bsz, seq)

    c_out = _dsa(qc, kc, vc, qi, ki, wi_t)
    n_blk = seq // MOBA_BLOCK
    km = k_mean.reshape(bsz, n_blk, D_HEADS, HEAD_DIM).transpose(0, 2, 1, 3)
    km = jnp.pad(km, ((0, 0), (0, 0), (0, 0), (0, LANES - HEAD_DIM))).astype(BF16)
    d_out = _moba(qd, kd, vd, km)
    return c_out.reshape(bsz * seq, C_WIDTH), d_out.reshape(bsz * seq, D_WIDTH), w_out


def kernel(x, positions, ffn1_norm, ffn1_w_gate, ffn1_w_up, ffn1_w_down, mix_norm, ffn2_norm, ffn2_w_gate, ffn2_w_up, ffn2_w_down, ev_w_in, ev_sgu_norm, ev_sgu_w, ev_sgu_b, ev_cmp_pos_k, ev_cmp_w1_k, ev_cmp_w2_k, ev_cmp_pos_v, ev_cmp_w1_v, ev_cmp_w2_v, ev_w_out, od_w_in, od_w_out, final_norm):
    bsz, seq, _ = x.shape
    depth = ffn1_norm.shape[0]
    tab64 = _rope_tables(positions, ROT_DIM, HEAD_DIM)
    tab32 = _rope_tables(positions, IDX_ROT, IDX_DIM)
    x = x.reshape(bsz * seq, D_MODEL)
    ffn1_w = [w.astype(BF16) for w in (ffn1_w_gate, ffn1_w_up, ffn1_w_down)]
    ffn2_w = [w.astype(BF16) for w in (ffn2_w_gate, ffn2_w_up, ffn2_w_down)]
    for i in range(depth):
        x = _ffn(x, ffn1_norm[i], *ffn1_w, i)
        if i % 2 == 0:
            e = i // 2
            mix = _even_mixer(x, bsz, seq, tab64, mix_norm[i], ev_w_in[e], ev_sgu_norm[e], ev_sgu_w[e],
                              ev_sgu_b[e], ev_cmp_pos_k[e], ev_cmp_w1_k[e], ev_cmp_w2_k[e],
                              ev_cmp_pos_v[e], ev_cmp_w1_v[e], ev_cmp_w2_v[e], ev_w_out[e])
        else:
            o = i // 2
            mix = _odd_mixer(x, bsz, seq, tab64, tab32, mix_norm[i], od_w_in[o], od_w_out[o])
        x = _ffn(x, ffn2_norm[i], *ffn2_w, i, final_g=final_norm if i == depth - 1 else None, mix=mix)
    return x.reshape(bsz, seq, D_MODEL)
```
